```python
import jax
import jax.numpy as jnp
from jax import lax
import numpy as np

D_MODEL = 1024
BATCH = 8
SEQ = 2048
DEPTH = 2
DEC_BATCH = 128
DEC_SEQ = 8
PAST_LEN = 2048
PAGE_SIZE = 128

H_A = 4
DK_A = D_MODEL // H_A
DV_A = 2 * DK_A
RET_CHUNK = 128
H_B = 16
DH_B = D_MODEL // H_B
G_B = 4
HPG = H_B // G_B
L_CMP = 32
S_CMP = 16
L_SEL = 64
N_SEL = 16
WINDOW = 512
CMP_HID = 2 * DH_B
Q_BLOCK = 128
FORCE_BONUS = 1e4
D_FF = 2816
CONV_W = 3
EPS = 1e-6

kernel_name = 'yoco_retention_nsa_decode_step'


def rmsnorm(x, g):
    xf = x.astype(jnp.float32)
    y = xf * lax.rsqrt(jnp.mean(xf * xf, axis=-1, keepdims=True) + EPS)
    return (y * g.astype(jnp.float32)).astype(x.dtype)


def modulate(h, shift, scale):
    return h * (1.0 + scale[:, None, :]) + shift[:, None, :]


def masked_softmax(s, mask):
    s = jnp.where(mask, s.astype(jnp.float32), -jnp.inf)
    m = jnp.max(s, axis=-1, keepdims=True)
    m = jnp.where(jnp.isfinite(m), m, 0.0)
    e = jnp.where(mask, jnp.exp(s - m), 0.0)
    return e / jnp.maximum(jnp.sum(e, axis=-1, keepdims=True), 1e-30)


def alibi_slopes():
    return jnp.exp2(-8.0 * jnp.arange(1, H_B + 1, dtype=jnp.float32) / H_B)


def retention_log_decay():
    return jnp.log1p(-jnp.exp2(-5.0 - jnp.arange(H_A, dtype=jnp.float32)))


def retention_chunk(S, q, k, v, log_g):
    c = q.shape[1]
    i = jnp.arange(c, dtype=jnp.float32)
    diff = i[:, None] - i[None, :]
    dmask = jnp.where(diff >= 0, jnp.exp(jnp.maximum(diff, 0.0)[None] * log_g[:, None, None]), 0.0)
    scores = jnp.einsum('bihd,bjhd->bhij', q, k) * dmask
    o = jnp.einsum('bhij,bjhv->bihv', scores, v)
    o = o + jnp.einsum('bihd,bhdv->bihv', q, S) * jnp.exp((i[:, None] + 1.0) * log_g[None, :])[None, :, :, None]
    w = jnp.exp((c - 1.0 - i)[:, None] * log_g[None, :])
    kw = k * w[None, :, :, None]
    S_new = jnp.exp(c * log_g)[None, :, None, None] * S + jnp.einsum('bjhd,bjhv->bhdv', kw, v)
    return S_new, o


def retention_mixer(h, S0, w_in, w_out):
    b, t, _ = h.shape
    nq = H_A * DK_A
    nv = H_A * DV_A
    pr = h @ w_in
    q = pr[..., :nq].reshape(b, t, H_A, DK_A)
    k = pr[..., nq:2 * nq].reshape(b, t, H_A, DK_A) * (DK_A ** -0.5)
    v = pr[..., 2 * nq:2 * nq + nv].reshape(b, t, H_A, DV_A)
    g = pr[..., 2 * nq + nv:]
    log_g = retention_log_decay()
    c = RET_CHUNK if t % RET_CHUNK == 0 else t
    nc = t // c
    if nc == 1:
        S, o = retention_chunk(S0, q, k, v, log_g)
    else:
        def to_chunks(a):
            return a.reshape(b, nc, c, *a.shape[2:]).swapaxes(0, 1)

        def step(S, inp):
            qc, kc, vc = inp
            return retention_chunk(S, qc, kc, vc, log_g)

        S, o = lax.scan(step, S0, (to_chunks(q), to_chunks(k), to_chunks(v)))
        o = o.swapaxes(0, 1).reshape(b, t, H_A, DV_A)
    of = o.astype(jnp.float32)
    of = of * lax.rsqrt(jnp.mean(of * of, axis=-1, keepdims=True) + EPS)
    y = (of.reshape(b, t, nv).astype(h.dtype) * jax.nn.silu(g)) @ w_out
    return y, S


def conv_ffn(h, buf, w_in, cw, cb, w_out):
    t = h.shape[1]
    u = h @ w_in
    ext = jnp.concatenate([buf.astype(u.dtype), u], axis=1)
    z = cb
    for j in range(CONV_W):
        z = z + cw[j] * ext[:, j:j + t]
    a, g = jnp.split(z, 2, axis=-1)
    return (jax.nn.silu(g) * a) @ w_out, ext[:, -(CONV_W - 1):]


def compress_branch(rows, pe, w1, w2):
    bx, L = rows.shape[:2]
    r_n = L_CMP // S_CMP
    n_pc = L // S_CMP
    nb = n_pc - r_n + 1
    P = rows[:, :n_pc * S_CMP].reshape(bx, n_pc, S_CMP, G_B, DH_B).transpose(0, 1, 3, 2, 4)
    pe_r = pe.reshape(r_n, S_CMP, DH_B)
    w1_r = w1.reshape(r_n, S_CMP * DH_B, CMP_HID)
    z = 0.0
    for r in range(r_n):
        part = (P[:, r:r + nb] + pe_r[r][None, None, None]).reshape(bx, nb, G_B, S_CMP * DH_B)
        z = z + part @ w1_r[r]
    return jax.nn.silu(z) @ w2


def sel_overlap(nb, nsel):
    i = jnp.arange(nb)[:, None] * S_CMP
    j = jnp.arange(nsel)[None, :] * L_SEL
    return ((i <= j + L_SEL - 1) & (i + L_CMP - 1 >= j)).astype(jnp.float32)


def pad_to_sel(rows):
    L = rows.shape[1]
    nsel = -(-L // L_SEL)
    return jnp.pad(rows, ((0, 0), (0, nsel * L_SEL - L), (0, 0), (0, 0)))


def nsa_block(q, qpos, gates, kc, vc, cend, ks, vs, kw, vw, wpos):
    tq = q.shape[0]
    nb = kc.shape[0]
    nsel = ks.shape[0] // L_SEL
    slopes = alibi_slopes().reshape(G_B, HPG)[None, :, :, None]
    qg = q.reshape(tq, G_B, HPG, DH_B) * (DH_B ** -0.5)
    tpos = qpos[:, None, None, None]
    s = jnp.einsum('tghd,ngd->tghn', qg, kc).astype(jnp.float32) - slopes * (tpos - cend[None, None, None, :])
    ok = (cend[None, :] <= qpos[:, None])[:, None, None, :]
    p_cmp = masked_softmax(s, ok)
    o_cmp = jnp.einsum('tghn,ngd->tghd', p_cmp.astype(vc.dtype), vc)
    imp = jnp.einsum('tghn,nj->tgj', p_cmp, sel_overlap(nb, nsel))
    blk = jnp.arange(nsel)[None, :]
    cur = (qpos // L_SEL)[:, None]
    forced = (blk == 0) | (blk == cur) | (blk == cur - 1)
    score = jnp.where((blk <= cur)[:, None, :], imp + FORCE_BONUS * forced[:, None, :].astype(jnp.float32), -jnp.inf)
    _, sel = lax.top_k(score, min(N_SEL, nsel))
    g_idx = jnp.arange(G_B)[None, :, None]
    k_blk = ks.reshape(nsel, L_SEL, G_B, DH_B).transpose(2, 0, 1, 3)
    v_blk = vs.reshape(nsel, L_SEL, G_B, DH_B).transpose(2, 0, 1, 3)
    n_k = sel.shape[-1] * L_SEL
    k_sel = k_blk[g_idx, sel].reshape(tq, G_B, n_k, DH_B)
    v_sel = v_blk[g_idx, sel].reshape(tq, G_B, n_k, DH_B)
    spos = (sel[..., None] * L_SEL + jnp.arange(L_SEL)).reshape(tq, G_B, n_k)[:, :, None, :]
    s = jnp.einsum('tghd,tgsd->tghs', qg, k_sel).astype(jnp.float32) - slopes * (tpos - spos)
    p_slc = masked_softmax(s, spos <= tpos)
    o_slc = jnp.einsum('tghs,tgsd->tghd', p_slc.astype(v_sel.dtype), v_sel)
    dist = qpos[:, None] - wpos[None, :]
    okw = ((dist >= 0) & (dist <= WINDOW) & (wpos[None, :] >= 0))[:, None, None, :]
    s = jnp.einsum('tghd,sgd->tghs', qg, kw).astype(jnp.float32) - slopes * dist[:, None, None, :]
    p_win = masked_softmax(s, okw)
    o_win = jnp.einsum('tghs,sgd->tghd', p_win.astype(vw.dtype), vw)
    g = gates.reshape(tq, G_B, HPG, 3).astype(jnp.float32)
    out = g[..., 0:1] * o_cmp + g[..., 1:2] * o_slc + g[..., 2:3] * o_win
    return out.reshape(tq, H_B * DH_B).astype(q.dtype)


def nsa_prompt(q, gates, kc, vc, cend, ks, vs, kw, vw):
    nb_, t_ = q.shape[:2]
    nqb = t_ // Q_BLOCK
    pad = ((0, 0), (WINDOW, 0), (0, 0), (0, 0))
    kw_p = jnp.pad(kw, pad)
    vw_p = jnp.pad(vw, pad)
    span = WINDOW + Q_BLOCK

    def item(n):
        b = n // nqb
        s0 = (n % nqb) * Q_BLOCK
        qpos = s0 + jnp.arange(Q_BLOCK)
        wpos = s0 - WINDOW + jnp.arange(span)
        return nsa_block(lax.dynamic_slice_in_dim(q[b], s0, Q_BLOCK, 0), qpos,
                         lax.dynamic_slice_in_dim(gates[b], s0, Q_BLOCK, 0),
                         kc[b], vc[b], cend, ks[b], vs[b],
                         lax.dynamic_slice_in_dim(kw_p[b], s0, span, 0),
                         lax.dynamic_slice_in_dim(vw_p[b], s0, span, 0), wpos)

    o = lax.map(item, jnp.arange(nb_ * nqb))
    return o.reshape(nb_, t_, H_B * DH_B)


def nsa_sample(q, gates, kc, vc, cend, ks, vs, kw, vw, qpos, wpos):
    def item(b):
        return nsa_block(q[b], qpos, gates[b], kc[b], vc[b], cend, ks[b], vs[b], kw[b], vw[b], wpos)
    return lax.map(item, jnp.arange(q.shape[0]))


def trunk(x, c, S_ret, conv_buf, past_kv, win_buf, p, sample):
    b, t, _ = x.shape
    n_a = DEPTH // 2
    new_S, new_conv = [], []
    for l in range(DEPTH):
        mod = jax.nn.silu(c) @ p['w_ada'][l] + p['b_ada'][l]
        sh1, sc1, ga1, sh2, sc2, ga2 = jnp.split(mod, 6, axis=-1)
        if l == n_a:
            mkv = jax.nn.silu(c) @ p['w_ada_kv'] + p['b_ada_kv']
            sh_kv, sc_kv = jnp.split(mkv, 2, axis=-1)
            hk = modulate(rmsnorm(x, p['g_kv']), sh_kv, sc_kv)
            kv = (hk @ p['w_kv']).reshape(b, t, 6, G_B, DH_B)
            kv_rows, win_rows = kv[:, :, :4], kv[:, :, 4:]
            if sample:
                ctx = jnp.concatenate([past_kv.astype(kv.dtype), kv_rows], axis=1)
                win_ctx = jnp.concatenate([win_buf.astype(kv.dtype), win_rows], axis=1)
                past = past_kv.shape[1]
                qpos = past + jnp.arange(t)
                wpos = past - win_buf.shape[1] + jnp.arange(win_ctx.shape[1])
            else:
                ctx = kv_rows
                win_ctx = win_rows
            kc = compress_branch(ctx[:, :, 0], p['pe_ck'], p['w_ck1'], p['w_ck2'])
            vc = compress_branch(ctx[:, :, 1], p['pe_cv'], p['w_cv1'], p['w_cv2'])
            cend = jnp.arange(kc.shape[1]) * S_CMP + (L_CMP - 1)
            ks = pad_to_sel(ctx[:, :, 2])
            vs = pad_to_sel(ctx[:, :, 3])
            kw, vw = win_ctx[:, :, 0], win_ctx[:, :, 1]
            new_win = win_ctx[:, -min(WINDOW, win_ctx.shape[1]):]
        h = modulate(rmsnorm(x, p['g_mix'][l]), sh1, sc1)
        if l < n_a:
            y, S = retention_mixer(h, S_ret[l], p['w_ret_in'][l], p['w_ret_out'][l])
            new_S.append(S)
        else:
            lb = l - n_a
            pr = h @ p['w_nsa_in'][lb]
            q = pr[..., :H_B * DH_B].reshape(b, t, H_B, DH_B)
            gates = jax.nn.sigmoid(pr[..., H_B * DH_B:].astype(jnp.float32)).reshape(b, t, H_B, 3)
            if sample:
                o = nsa_sample(q, gates, kc, vc, cend, ks, vs, kw, vw, qpos, wpos)
            else:
                o = nsa_prompt(q, gates, kc, vc, cend, ks, vs, kw, vw)
            y = o @ p['w_nsa_out'][lb]
        x = x + ga1[:, None, :] * y
        h = modulate(rmsnorm(x, p['g_ffn'][l]), sh2, sc2)
        y, buf = conv_ffn(h, conv_buf[l], p['w_ffn_in'][l], p['conv_w'][l], p['conv_b'][l], p['w_ffn_out'][l])
        new_conv.append(buf)
        x = x + ga2[:, None, :] * y
    return rmsnorm(x, p['g_final']), jnp.stack(new_S), jnp.stack(new_conv), kv_rows, new_win


def setup_inputs(seed: int = 0) -> dict:
    key = jax.random.key(seed)
    kit = iter(list(jax.random.split(key, 40)))
    n_a = DEPTH // 2
    n_b = DEPTH - n_a
    n_pages = PAST_LEN // PAGE_SIZE
    n_used = DEC_BATCH * n_pages
    n_phys = n_used + max(1, n_used // 4)
    w_buf = min(WINDOW, PAST_LEN)
    d = D_MODEL
    f2 = 2 * D_FF
    ret_proj = H_A * (2 * DK_A + 2 * DV_A)

    def nrm(shape, scale):
        return jax.random.normal(next(kit), shape, jnp.float32) * scale

    x_prompt = nrm((BATCH, SEQ, d), 1.0)
    x_sample = nrm((DEC_BATCH, DEC_SEQ, d), 1.0)
    c_prompt = nrm((BATCH, d), 1.0)
    c_sample = nrm((DEC_BATCH, d), 1.0)
    state_ret = nrm((n_a, DEC_BATCH, H_A, DK_A, DV_A), 1.0)
    state_conv = nrm((DEPTH, DEC_BATCH, CONV_W - 1, f2), 1.0)
    cache_kv = nrm((n_phys, PAGE_SIZE, 4, G_B, DH_B), 1.0)
    state_win = nrm((DEC_BATCH, w_buf, 2, G_B, DH_B), 1.0)
    page_table = jax.random.permutation(next(kit), n_phys)[:n_used].reshape(DEC_BATCH, n_pages).astype(jnp.int32)
    return {
        'x_prompt': x_prompt, 'x_sample': x_sample, 'c_prompt': c_prompt, 'c_sample': c_sample,
        'state_ret': state_ret, 'state_conv': state_conv, 'cache_kv': cache_kv, 'state_win': state_win,
        'page_table': page_table,
        'w_ada': nrm((DEPTH, d, 6 * d), 0.5 * d ** -0.5),
        'b_ada': nrm((DEPTH, 6 * d), 0.02),
        'g_mix': 1.0 + nrm((DEPTH, d), 0.05),
        'g_ffn': 1.0 + nrm((DEPTH, d), 0.05),
        'w_ffn_in': nrm((DEPTH, d, f2), d ** -0.5),
        'conv_w': nrm((DEPTH, CONV_W, f2), CONV_W ** -0.5),
        'conv_b': nrm((DEPTH, f2), 0.02),
        'w_ffn_out': nrm((DEPTH, D_FF, d), D_FF ** -0.5),
        'w_ret_in': nrm((n_a, d, ret_proj), d ** -0.5),
        'w_ret_out': nrm((n_a, H_A * DV_A, d), (H_A * DV_A) ** -0.5),
        'g_kv': 1.0 + nrm((d,), 0.05),
        'w_ada_kv': nrm((d, 2 * d), 0.5 * d ** -0.5),
        'b_ada_kv': nrm((2 * d,), 0.02),
        'w_kv': nrm((d, 6 * G_B * DH_B), d ** -0.5),
        'pe_ck': nrm((L_CMP, DH_B), 0.1),
        'pe_cv': nrm((L_CMP, DH_B), 0.1),
        'w_ck1': nrm((L_CMP * DH_B, CMP_HID), (L_CMP * DH_B) ** -0.5),
        'w_ck2': nrm((CMP_HID, DH_B), CMP_HID ** -0.5),
        'w_cv1': nrm((L_CMP * DH_B, CMP_HID), (L_CMP * DH_B) ** -0.5),
        'w_cv2': nrm((CMP_HID, DH_B), CMP_HID ** -0.5),
        'w_nsa_in': nrm((n_b, d, H_B * DH_B + 3 * H_B), d ** -0.5),
        'w_nsa_out': nrm((n_b, H_B * DH_B, d), (H_B * DH_B) ** -0.5),
        'g_final': 1.0 + nrm((d,), 0.05),
    }


def reference(x_prompt, x_sample, c_prompt, c_sample, state_ret, state_conv, cache_kv, state_win, page_table,
              w_ada, b_ada, g_mix, g_ffn, w_ffn_in, conv_w, conv_b, w_ffn_out, w_ret_in, w_ret_out,
              g_kv, w_ada_kv, b_ada_kv, w_kv, pe_ck, pe_cv, w_ck1, w_ck2, w_cv1, w_cv2,
              w_nsa_in, w_nsa_out, g_final):
    p = dict(w_ada=w_ada, b_ada=b_ada, g_mix=g_mix, g_ffn=g_ffn, w_ffn_in=w_ffn_in, conv_w=conv_w,
             conv_b=conv_b, w_ffn_out=w_ffn_out, w_ret_in=w_ret_in, w_ret_out=w_ret_out, g_kv=g_kv,
             w_ada_kv=w_ada_kv, b_ada_kv=b_ada_kv, w_kv=w_kv, pe_ck=pe_ck, pe_cv=pe_cv, w_ck1=w_ck1,
             w_ck2=w_ck2, w_cv1=w_cv1, w_cv2=w_cv2, w_nsa_in=w_nsa_in, w_nsa_out=w_nsa_out, g_final=g_final)
    n_a = DEPTH // 2
    bp = x_prompt.shape[0]
    db = x_sample.shape[0]
    past = page_table.shape[1] * PAGE_SIZE
    past_kv = cache_kv[page_table].reshape(db, past, 4, G_B, DH_B)
    S0_p = jnp.zeros((n_a, bp, H_A, DK_A, DV_A), jnp.float32)
    conv0_p = jnp.zeros((DEPTH, bp, CONV_W - 1, 2 * D_FF), x_prompt.dtype)
    y_prompt, ret_prompt, conv_prompt, kv_prompt, win_prompt = trunk(
        x_prompt, c_prompt, S0_p, conv0_p, None, None, p, False)
    y_sample, ret_sample, conv_sample, kv_sample, win_sample = trunk(
        x_sample, c_sample, state_ret.astype(jnp.float32), state_conv, past_kv, state_win, p, True)
    return (y_prompt, y_sample, ret_prompt, ret_sample, conv_prompt, conv_sample,
            kv_prompt, kv_sample, win_prompt, win_sample)
```

```python
import functools

import numpy as np
import jax
import jax.numpy as jnp
from jax import lax
from jax.experimental import pallas as pl
from jax.experimental.pallas import tpu as pltpu

F32 = jnp.float32
BF16 = jnp.bfloat16

D_MODEL = 1024
H_A = 4
DK_A = D_MODEL // H_A
DV_A = 2 * DK_A
RET_CHUNK = 128
H_B = 16
DH_B = D_MODEL // H_B
G_B = 4
HPG = H_B // G_B
GD = G_B * DH_B
L_CMP = 32
S_CMP = 16
L_SEL = 64
N_SEL = 16
WINDOW = 512
CMP_HID = 2 * DH_B
Q_BLOCK = 128
FORCE_BONUS = 1e4
D_FF = 2816
CONV_W = 3
EPS = 1e-6
PAGE_SIZE = 128
NEG = -1e30

VMEM_LIMIT = 56 * 1024 * 1024
FF_CHUNK = 256
SLC_TILE = 512

_NT = (((1,), (1,)), ((), ()))


def _sigmoid(x):
    return 1.0 / (1.0 + jnp.exp(-x))


def _silu(x):
    return x * _sigmoid(x)


def _dot(a, b):
    return jnp.dot(a, b, preferred_element_type=F32)


def _dot_nt(a, b):
    return lax.dot_general(a, b, _NT, preferred_element_type=F32)


def _split3(x):
    hi = x.astype(BF16)
    r = x - hi.astype(F32)
    mid = r.astype(BF16)
    lo = (r - mid.astype(F32)).astype(BF16)
    return hi, mid, lo


def _params(*sem):
    return pltpu.CompilerParams(dimension_semantics=sem, vmem_limit_bytes=VMEM_LIMIT)


def _const_spec(shape):
    nd = len(shape)
    return pl.BlockSpec(shape, lambda *a: (0,) * nd, pipeline_mode=pl.Buffered(1))


def _mod_arg(v, m, tm, rows_per_batch):
    b, d = v.shape
    if rows_per_batch % tm == 0:
        tpb = rows_per_batch // tm
        return v.reshape(b, 1, d), (1, 1, d), (lambda i: (i // tpb, 0, 0))
    assert tm % rows_per_batch == 0
    arr = jnp.repeat(v, rows_per_batch, axis=0).reshape(m // tm, tm, d)
    return arr, (1, tm, d), (lambda i: (i, 0, 0))


def _ada_kernel(c_ref, w_ref, b_ref, o_ref):
    c = c_ref[...]
    o_ref[...] = _dot(_silu(c).astype(BF16), w_ref[...]) + b_ref[...]


def _ada_call(c, w, b, tn=2048):
    m, d = c.shape
    n = w.shape[1]
    assert n % tn == 0
    return pl.pallas_call(
        _ada_kernel,
        grid=(n // tn,),
        in_specs=[pl.BlockSpec((m, d), lambda j: (0, 0)),
                  pl.BlockSpec((d, tn), lambda j: (0, j)),
                  pl.BlockSpec((1, tn), lambda j: (0, j))],
        out_specs=pl.BlockSpec((m, tn), lambda j: (0, j)),
        out_shape=jax.ShapeDtypeStruct((m, n), F32),
        compiler_params=_params("parallel"),
        name="ada",
    )(c, w, b)


def _norm_mod(x, g, sh, sc):
    y = x * lax.rsqrt(jnp.mean(x * x, axis=-1, keepdims=True) + EPS) * g
    return y * (1.0 + sc) + sh


def _nmm_kernel(x_ref, g_ref, sh_ref, sc_ref, w_ref, o_ref, h_scr, *, act):
    @pl.when(pl.program_id(1) == 0)
    def _():
        h_scr[...] = _norm_mod(x_ref[...], g_ref[...], sh_ref[0], sc_ref[0]).astype(BF16)

    acc = _dot(h_scr[...], w_ref[...])
    if act == "sigmoid":
        acc = _sigmoid(acc)
    o_ref[...] = acc.astype(o_ref.dtype)


def _nmm_call(x, g, sh, sc, w, rows_per_batch, tm, tn, out_dtype, act=None, name="nmm"):
    m, d = x.shape
    n = w.shape[1]
    assert m % tm == 0 and n % tn == 0
    sh_a, mshape, mmap = _mod_arg(sh, m, tm, rows_per_batch)
    sc_a, _, _ = _mod_arg(sc, m, tm, rows_per_batch)
    return pl.pallas_call(
        functools.partial(_nmm_kernel, act=act),
        grid=(m // tm, n // tn),
        in_specs=[pl.BlockSpec((tm, d), lambda i, j: (i, 0)),
                  pl.BlockSpec((1, d), lambda i, j: (0, 0)),
                  pl.BlockSpec(mshape, lambda i, j: mmap(i)),
                  pl.BlockSpec(mshape, lambda i, j: mmap(i)),
                  pl.BlockSpec((d, tn), lambda i, j: (0, j))],
        out_specs=pl.BlockSpec((tm, tn), lambda i, j: (i, j)),
        out_shape=jax.ShapeDtypeStruct((m, n), out_dtype),
        scratch_shapes=[pltpu.VMEM((tm, d), BF16)],
        compiler_params=_params("parallel", "arbitrary"),
        name=name,
    )(x, g.reshape(1, d), sh_a, sc_a, w)


def _mm_res_kernel(a_ref, w_ref, res_ref, ga_ref, o_ref):
    y = _dot(a_ref[...].astype(BF16), w_ref[...])
    o_ref[...] = res_ref[...] + ga_ref[0] * y


def _mm_res_call(a, w, res, gate, rows_per_batch, tm, name="mm_res"):
    m, k = a.shape
    d = w.shape[1]
    ga_a, mshape, mmap = _mod_arg(gate, m, tm, rows_per_batch)
    return pl.pallas_call(
        _mm_res_kernel,
        grid=(m // tm,),
        in_specs=[pl.BlockSpec((tm, k), lambda i: (i, 0)),
                  _const_spec((k, d)),
                  pl.BlockSpec((tm, d), lambda i: (i, 0)),
                  pl.BlockSpec(mshape, lambda i: mmap(i))],
        out_specs=pl.BlockSpec((tm, d), lambda i: (i, 0)),
        out_shape=jax.ShapeDtypeStruct((m, d), F32),
        compiler_params=_params("parallel"),
        name=name,
    )(a, w, res, ga_a)


def _ret_kernel(*refs, c, nc, has_s0):
    if has_s0:
        lg_ref, q_ref, k_ref, v_ref, g_ref, s0_ref, o_ref, so_ref, s_scr = refs
    else:
        lg_ref, q_ref, k_ref, v_ref, g_ref, o_ref, so_ref, s_scr = refs
        s0_ref = None
    n = pl.program_id(2)

    @pl.when(n == 0)
    def _():
        if has_s0:
            s_scr[...] = s0_ref[0, 0]
        else:
            s_scr[...] = jnp.zeros_like(s_scr)

    cp = max(c, RET_CHUNK)

    def padded(ref):
        a = ref[...].astype(F32)
        if cp > c:
            a = jnp.concatenate([a, jnp.zeros((cp - c, a.shape[1]), F32)], axis=0)
        return a

    lg = lg_ref[0][:, 0:1]
    q = padded(q_ref).astype(BF16)
    kf = padded(k_ref) * (DK_A ** -0.5)
    v = padded(v_ref).astype(BF16)
    i = lax.broadcasted_iota(jnp.int32, (cp, 1), 0).astype(F32)
    j = lax.broadcasted_iota(jnp.int32, (1, cp), 1).astype(F32)
    diff = i - j
    dmask = jnp.where(diff >= 0, jnp.exp(jnp.maximum(diff, 0.0) * lg), 0.0)
    scores = _dot_nt(q, kf.astype(BF16)) * dmask
    s_old = s_scr[...]
    o = _dot(scores.astype(BF16), v) + _dot(q, s_old.astype(BF16)) * jnp.exp((i + 1.0) * lg)
    w = jnp.exp((c - 1.0 - i) * lg)
    kw_t = (kf * w).T.astype(BF16)
    s_new = jnp.exp(c * lg) * s_old + _dot(kw_t, v)
    s_scr[...] = s_new
    of = o * lax.rsqrt(jnp.mean(o * o, axis=-1, keepdims=True) + EPS)
    gate = g_ref[...].astype(F32)
    o_ref[...] = (of[0:c] * _silu(gate)).astype(o_ref.dtype)

    @pl.when(n == nc - 1)
    def _():
        so_ref[0, 0] = s_new


def _ret_call(pr, s0, lg_tab, b, t, out_dtype):
    c = RET_CHUNK if t % RET_CHUNK == 0 else t
    nc = t // c
    has_s0 = s0 is not None
    qb, vb = H_A, (2 * H_A * DK_A) // DV_A
    gb = vb + H_A
    in_specs = [pl.BlockSpec((1, 1, 128), lambda bi, h, n: (h, 0, 0)),
                pl.BlockSpec((c, DK_A), lambda bi, h, n: (bi * nc + n, h)),
                pl.BlockSpec((c, DK_A), lambda bi, h, n: (bi * nc + n, qb + h)),
                pl.BlockSpec((c, DV_A), lambda bi, h, n: (bi * nc + n, vb + h)),
                pl.BlockSpec((c, DV_A), lambda bi, h, n: (bi * nc + n, gb + h))]
    args = [lg_tab, pr, pr, pr, pr]
    if has_s0:
        in_specs.append(pl.BlockSpec((1, 1, DK_A, DV_A), lambda bi, h, n: (bi, h, 0, 0)))
        args.append(s0)
    return pl.pallas_call(
        functools.partial(_ret_kernel, c=c, nc=nc, has_s0=has_s0),
        grid=(b, H_A, nc),
        in_specs=in_specs,
        out_specs=[pl.BlockSpec((c, DV_A), lambda bi, h, n: (bi * nc + n, h)),
                   pl.BlockSpec((1, 1, DK_A, DV_A), lambda bi, h, n: (bi, h, 0, 0))],
        out_shape=[jax.ShapeDtypeStruct((b * t, H_A * DV_A), out_dtype),
                   jax.ShapeDtypeStruct((b, H_A, DK_A, DV_A), F32)],
        scratch_shapes=[pltpu.VMEM((DK_A, DV_A), F32)],
        compiler_params=_params("parallel", "parallel", "arbitrary"),
        name="retention",
    )(*args)


def _ffn_kernel(*refs, t_in, nbt, final):
    if final:
        (x_ref, g_ref, sh_ref, sc_ref, ga_ref, win_ref, cw_ref, cb_ref, cbuf_ref, wout_ref, gf_ref,
         o_ref, nc_ref, h_scr, carry_scr, acc_scr) = refs
    else:
        (x_ref, g_ref, sh_ref, sc_ref, ga_ref, win_ref, cw_ref, cb_ref, cbuf_ref, wout_ref,
         o_ref, nc_ref, h_scr, carry_scr, acc_scr) = refs
        gf_ref = None
    tm = x_ref.shape[0]

    @pl.when(pl.program_id(1) == 0)
    def _():
        carry_scr[...] = cbuf_ref[...]

    x = x_ref[...]
    h_scr[...] = _norm_mod(x, g_ref[...], sh_ref[0], sc_ref[0]).astype(BF16)
    cwd = FF_CHUNK
    tpos = lax.broadcasted_iota(jnp.int32, (nbt, t_in, cwd), 1)
    is0 = tpos == 0
    is1 = tpos == 1

    def conv_part(off):
        u = _dot(h_scr[...], win_ref[:, off:off + cwd]).reshape(nbt, t_in, cwd)
        prev = carry_scr[:, :, off:off + cwd]
        p0 = prev[:, 0:1, :]
        p1 = prev[:, 1:2, :]
        r1 = pltpu.roll(u, 1, axis=1)
        r2 = pltpu.roll(u, 2, axis=1)
        um1 = jnp.where(is0, p1, r1)
        um2 = jnp.where(is0, p0, jnp.where(is1, p1, r2))
        newc = r2[:, 0:2, :]
        carry_scr[:, :, off:off + cwd] = newc
        nc_ref[:, :, off:off + cwd] = newc
        cw = cw_ref[:, off:off + cwd]
        z = cb_ref[:, off:off + cwd] + cw[0:1] * um2
        z = z + cw[1:2] * um1
        z = z + cw[2:3] * u
        return z.reshape(tm, cwd)

    for ch in range(D_FF // cwd):
        za = conv_part(ch * cwd)
        zg = conv_part(D_FF + ch * cwd)
        act = (_silu(zg) * za).astype(BF16)
        y = _dot(act, wout_ref[ch * cwd:(ch + 1) * cwd, :])
        if ch == 0:
            acc_scr[...] = y
        else:
            acc_scr[...] += y
    xn = x + ga_ref[0] * acc_scr[...]
    if final:
        xn = xn * lax.rsqrt(jnp.mean(xn * xn, axis=-1, keepdims=True) + EPS) * gf_ref[...]
    o_ref[...] = xn


def _ffn_call(x, g, sh, sc, ga, w_in, cw, cb, cbuf, w_out, b, t, tm, g_final=None, name="ffn"):
    m, d = x.shape
    f2 = w_in.shape[1]
    if t % tm == 0:
        t_in, nbt, nb_tiles, nt = tm, 1, b, t // tm
    else:
        assert tm % t == 0 and m % tm == 0
        t_in, nbt, nb_tiles, nt = t, tm // t, m // tm, 1
    sh_a, mshape, mmap = _mod_arg(sh, m, tm, t)
    sc_a, _, _ = _mod_arg(sc, m, tm, t)
    ga_a, _, _ = _mod_arg(ga, m, tm, t)
    final = g_final is not None
    in_specs = [pl.BlockSpec((tm, d), lambda bi, ti: (bi * nt + ti, 0)),
                pl.BlockSpec((1, d), lambda bi, ti: (0, 0)),
                pl.BlockSpec(mshape, lambda bi, ti: mmap(bi * nt + ti)),
                pl.BlockSpec(mshape, lambda bi, ti: mmap(bi * nt + ti)),
                pl.BlockSpec(mshape, lambda bi, ti: mmap(bi * nt + ti)),
                _const_spec((d, f2)),
                _const_spec((CONV_W, f2)),
                _const_spec((1, f2)),
                pl.BlockSpec((nbt, CONV_W - 1, f2), lambda bi, ti: (bi, 0, 0)),
                _const_spec((D_FF, d))]
    args = [x, g.reshape(1, d), sh_a, sc_a, ga_a, w_in, cw, cb.reshape(1, f2), cbuf, w_out]
    if final:
        in_specs.append(pl.BlockSpec((1, d), lambda bi, ti: (0, 0)))
        args.append(g_final.reshape(1, d))
    return pl.pallas_call(
        functools.partial(_ffn_kernel, t_in=t_in, nbt=nbt, final=final),
        grid=(nb_tiles, nt),
        in_specs=in_specs,
        out_specs=[pl.BlockSpec((tm, d), lambda bi, ti: (bi * nt + ti, 0)),
                   pl.BlockSpec((nbt, CONV_W - 1, f2), lambda bi, ti: (bi, 0, 0))],
        out_shape=[jax.ShapeDtypeStruct((m, d), F32),
                   jax.ShapeDtypeStruct((b, CONV_W - 1, f2), F32)],
        scratch_shapes=[pltpu.VMEM((tm, d), BF16),
                        pltpu.VMEM((nbt, CONV_W - 1, f2), F32),
                        pltpu.VMEM((tm, d), F32)],
        compiler_params=_params("parallel", "arbitrary"),
        name=name,
    )(*args)


N_PAGES_ARG = 16


def _cmp_kernel(pt_ref, *refs):
    pages = refs[:N_PAGES_ARG]
    pe_ref, w1_ref, w2_ref, o_ref = refs[N_PAGES_ARG:]
    npc = PAGE_SIZE // S_CMP
    r_n = L_CMP // S_CMP
    for kind in range(2):
        rows = jnp.concatenate([pg[0, :, kind * GD:(kind + 1) * GD] for pg in pages], axis=0)
        rows = pltpu.einshape("nsl->snl", rows.reshape(len(pages) * npc, S_CMP, GD))
        pieces = [rows[s] for s in range(S_CMP)]
        z = None
        for r in range(r_n):
            a = jnp.concatenate([(pieces[s] + pe_ref[kind, r, s:s + 1, :]).astype(BF16)
                                 for s in range(S_CMP)], axis=1)
            y = _dot(a, w1_ref[kind, r])
            if r > 0:
                y = pltpu.roll(y, y.shape[0] - r, axis=0)
            z = y if z is None else z + y
        o_ref[0, kind] = _dot(_silu(z).astype(BF16), w2_ref[kind]).astype(o_ref.dtype)


def _cmp_call(rows3, page_tab, pe, w1, w2, col_blocks):
    b, npg = page_tab.shape
    assert npg == N_PAGES_ARG
    n_pieces = npg * PAGE_SIZE // S_CMP
    page_specs = [pl.BlockSpec((1, PAGE_SIZE, 2 * GD), functools.partial(lambda bi, pt, p: (pt[bi * N_PAGES_ARG + p], 0, 0), p=p))
                  for p in range(npg)]
    grid_spec = pltpu.PrefetchScalarGridSpec(
        num_scalar_prefetch=1,
        grid=(b,),
        in_specs=page_specs + [
            pl.BlockSpec(pe.shape, lambda bi, pt: (0, 0, 0, 0), pipeline_mode=pl.Buffered(1)),
            pl.BlockSpec(w1.shape, lambda bi, pt: (0, 0, 0, 0), pipeline_mode=pl.Buffered(1)),
            pl.BlockSpec(w2.shape, lambda bi, pt: (0, 0, 0), pipeline_mode=pl.Buffered(1))],
        out_specs=pl.BlockSpec((1, 2, n_pieces, GD), lambda bi, pt: (bi, 0, 0, 0)),
    )
    return pl.pallas_call(
        _cmp_kernel,
        grid_spec=grid_spec,
        out_shape=jax.ShapeDtypeStruct((b, 2, n_pieces, GD), BF16),
        compiler_params=_params("parallel"),
        name="compress",
    )(page_tab.reshape(-1), *([rows3] * npg), pe, w1, w2)


def _masked_softmax(s, ok):
    sm = jnp.where(ok, s, NEG)
    m = jnp.max(sm, axis=-1, keepdims=True)
    e = jnp.where(ok, jnp.exp(sm - m), 0.0)
    return e / jnp.maximum(jnp.sum(e, axis=-1, keepdims=True), 1e-30)


def _topk_mask_t(imp_t, cur, nsel, ncol):
    rows = imp_t.shape[0]
    jj = lax.broadcasted_iota(jnp.int32, (rows, ncol), 0)
    forced = (jj == 0) | (jj == cur) | (jj == cur - 1)
    score = jnp.where(jj <= cur, imp_t + FORCE_BONUS * forced.astype(F32), NEG)
    rank = jnp.zeros((rows, ncol), F32)
    for i in range(nsel):
        row = score[i:i + 1, :]
        lower = jnp.where(jj > i, 1.0, 0.0)
        rank = rank + jnp.where(row > score, 1.0, jnp.where(row == score, lower, 0.0))
    return jnp.where((rank < float(min(N_SEL, nsel))) & (jj < nsel), 1.0, 0.0)


def _eye_bf16(n):
    return jnp.where(lax.broadcasted_iota(jnp.int32, (n, n), 0) == lax.broadcasted_iota(jnp.int32, (n, n), 1),
                     1.0, 0.0).astype(BF16)


def _pad_rows(a, rows):
    if a.shape[0] == rows:
        return a
    return jnp.concatenate([a, jnp.zeros((rows - a.shape[0],) + a.shape[1:], a.dtype)], axis=0)


def _online_step(sm, v, m_old, l_old, acc_old):
    m_new = jnp.maximum(m_old, jnp.max(sm, axis=-1, keepdims=True))
    alpha = jnp.exp(m_old - m_new)
    e = jnp.where(sm > 0.5 * NEG, jnp.exp(sm - m_new), 0.0)
    l_new = alpha * l_old + jnp.sum(e, axis=-1, keepdims=True)
    acc_new = alpha * acc_old + _dot(e.astype(BF16), v)
    return m_new, l_new, acc_new


def _nsa_prompt_kernel(slope_ref, q_ref, gate_ref, cmp_ref, kv_ref, win_ref, eg_ref, ovt_ref, o_ref,
                       qq_scr, gexp_scr, m_scr, l_scr, acc_scr, oslc_scr, out_scr, *, t_len):
    qb = pl.program_id(1)
    s0 = qb * Q_BLOCK
    nsel = t_len // L_SEL
    hi, mid, lo = _split3(gate_ref[...])
    eg = eg_ref[...]
    gexp_scr[...] = _dot(hi, eg) + _dot(mid, eg) + _dot(lo, eg)
    lane_g = lax.broadcasted_iota(jnp.int32, (1, GD), 1) // DH_B
    for g in range(G_B):
        for h in range(HPG):
            qh = q_ref[:, h * GD:(h + 1) * GD]
            qq_scr[(g * HPG + h) * Q_BLOCK:(g * HPG + h + 1) * Q_BLOCK, :] = jnp.where(lane_g == g, qh, jnp.zeros_like(qh))
    out_scr[...] = jnp.zeros_like(out_scr)
    kc = cmp_ref[0, 0]
    vc = cmp_ref[0, 1]
    ovt = ovt_ref[...]
    eye = _eye_bf16(Q_BLOCK)
    tpos_i = s0 + lax.broadcasted_iota(jnp.int32, (Q_BLOCK, 1), 0)
    tpos = tpos_i.astype(F32)
    n_f = lax.broadcasted_iota(jnp.int32, (1, Q_BLOCK), 1).astype(F32)
    dist_c = tpos - (n_f * S_CMP + (L_CMP - 1.0))
    ok_c = dist_c >= 0.0
    rows_g = HPG * Q_BLOCK

    def group_body(g, carry):
        row0 = pl.multiple_of(g * rows_g, rows_g)
        qg = qq_scr[pl.ds(row0, rows_g), :]
        slopes = [slope_ref[g * HPG + h] for h in range(HPG)]
        s_c = _dot_nt(qg, kc)
        ps = [_masked_softmax(s_c[h * Q_BLOCK:(h + 1) * Q_BLOCK] - slopes[h] * dist_c, ok_c) for h in range(HPG)]
        o_cmp = _dot(jnp.concatenate(ps, axis=0).astype(BF16), vc)
        psum = (ps[0] + ps[1]) + (ps[2] + ps[3])
        phi, pmid, plo = _split3(psum)
        imp_t = _dot_nt(ovt, phi) + _dot_nt(ovt, pmid) + _dot_nt(ovt, plo)
        nrow = ((nsel + 7) // 8) * 8
        cur_t = (s0 + lax.broadcasted_iota(jnp.int32, (nrow, Q_BLOCK), 1)) // L_SEL
        sel_t = _topk_mask_t(imp_t[0:nrow], cur_t, nsel, Q_BLOCK)
        sel = _dot_nt(eye, _pad_rows(sel_t, Q_BLOCK).astype(BF16)).astype(BF16)

        m_scr[...] = jnp.full_like(m_scr, NEG)
        l_scr[...] = jnp.zeros_like(l_scr)
        acc_scr[...] = jnp.zeros_like(acc_scr)
        jblk = lax.broadcasted_iota(jnp.int32, (Q_BLOCK, SLC_TILE), 0)
        scol = lax.broadcasted_iota(jnp.int32, (Q_BLOCK, SLC_TILE), 1)
        s_iota = lax.broadcasted_iota(jnp.int32, (1, SLC_TILE), 1)

        def slc_body(kti, c2):
            k0 = pl.multiple_of(kti * SLC_TILE, SLC_TILE)
            kt = kv_ref[0, pl.ds(k0, SLC_TILE), 0:GD].astype(BF16)
            vt = kv_ref[0, pl.ds(k0, SLC_TILE), GD:2 * GD].astype(BF16)
            s = _dot_nt(qg, kt)
            esel = jnp.where(jblk == kti * (SLC_TILE // L_SEL) + scol // L_SEL, 1.0, 0.0).astype(BF16)
            dist = tpos - (k0 + s_iota).astype(F32)
            valid = jnp.where(dist >= 0.0, _dot(sel, esel), 0.0) > 0.5
            sm = jnp.concatenate([jnp.where(valid, s[h * Q_BLOCK:(h + 1) * Q_BLOCK] - slopes[h] * dist, NEG)
                                  for h in range(HPG)], axis=0)
            m_new, l_new, acc_new = _online_step(sm, vt, m_scr[...], l_scr[...], acc_scr[...])
            m_scr[...] = m_new
            l_scr[...] = l_new
            acc_scr[...] = acc_new
            return c2

        lax.fori_loop(0, (s0 + Q_BLOCK + SLC_TILE - 1) // SLC_TILE, slc_body, 0)
        oslc_scr[...] = acc_scr[...] / jnp.maximum(l_scr[...], 1e-30)

        m_scr[...] = jnp.full_like(m_scr, NEG)
        l_scr[...] = jnp.zeros_like(l_scr)
        acc_scr[...] = jnp.zeros_like(acc_scr)
        w_iota = lax.broadcasted_iota(jnp.int32, (1, Q_BLOCK), 1)
        n_wt = WINDOW // Q_BLOCK

        def win_body(w, c2):
            k0 = pl.multiple_of((qb - n_wt + w) * Q_BLOCK, Q_BLOCK)
            kt = win_ref[0, pl.ds(k0, Q_BLOCK), 0:GD].astype(BF16)
            vt = win_ref[0, pl.ds(k0, Q_BLOCK), GD:2 * GD].astype(BF16)
            s = _dot_nt(qg, kt)
            dist = tpos - (k0 + w_iota).astype(F32)
            valid = jnp.abs(dist - 0.5 * WINDOW) <= 0.5 * WINDOW
            sm = jnp.concatenate([jnp.where(valid, s[h * Q_BLOCK:(h + 1) * Q_BLOCK] - slopes[h] * dist, NEG)
                                  for h in range(HPG)], axis=0)
            m_new, l_new, acc_new = _online_step(sm, vt, m_scr[...], l_scr[...], acc_scr[...])
            m_scr[...] = m_new
            l_scr[...] = l_new
            acc_scr[...] = acc_new
            return c2

        lax.fori_loop(jnp.maximum(n_wt - qb, 0), n_wt + 1, win_body, 0)
        o_win = acc_scr[...] / jnp.maximum(l_scr[...], 1e-30)
        o_slc = oslc_scr[...]

        gmask = lane_g == g
        for h in range(HPG):
            r = slice(h * Q_BLOCK, (h + 1) * Q_BLOCK)
            c = slice(h * GD, (h + 1) * GD)
            comb = gexp_scr[:, h * GD:(h + 1) * GD] * o_cmp[r]
            comb = comb + gexp_scr[:, D_MODEL + h * GD:D_MODEL + (h + 1) * GD] * o_slc[r]
            comb = comb + gexp_scr[:, 2 * D_MODEL + h * GD:2 * D_MODEL + (h + 1) * GD] * o_win[r]
            out_scr[:, c] += jnp.where(gmask, comb, 0.0)
        return carry

    lax.fori_loop(0, G_B, group_body, 0)
    o_ref[...] = out_scr[...].astype(o_ref.dtype)


def _nsa_prompt_call(slopes, q, gates, cmp, kv3, win3, eg, ovt, b, t):
    nqb = t // Q_BLOCK
    rows = H_B * Q_BLOCK
    return pl.pallas_call(
        functools.partial(_nsa_prompt_kernel, t_len=t),
        grid=(b, nqb),
        in_specs=[pl.BlockSpec(memory_space=pltpu.SMEM),
                  pl.BlockSpec((Q_BLOCK, D_MODEL), lambda bi, qi: (bi * nqb + qi, 0)),
                  pl.BlockSpec((Q_BLOCK, 128), lambda bi, qi: (bi * nqb + qi, 0)),
                  pl.BlockSpec((1, 2, cmp.shape[2], GD), lambda bi, qi: (bi, 0, 0, 0)),
                  pl.BlockSpec((1, t, 2 * GD), lambda bi, qi: (bi, 0, 1)),
                  pl.BlockSpec((1, t, 2 * GD), lambda bi, qi: (bi, 0, 0)),
                  _const_spec(eg.shape),
                  _const_spec(ovt.shape)],
        out_specs=pl.BlockSpec((Q_BLOCK, D_MODEL), lambda bi, qi: (bi * nqb + qi, 0)),
        out_shape=jax.ShapeDtypeStruct((b * t, D_MODEL), BF16),
        scratch_shapes=[pltpu.VMEM((rows, GD), BF16),
                        pltpu.VMEM((Q_BLOCK, 3 * D_MODEL), F32),
                        pltpu.VMEM((HPG * Q_BLOCK, 1), F32),
                        pltpu.VMEM((HPG * Q_BLOCK, 1), F32),
                        pltpu.VMEM((HPG * Q_BLOCK, GD), F32),
                        pltpu.VMEM((HPG * Q_BLOCK, GD), F32),
                        pltpu.VMEM((Q_BLOCK, D_MODEL), F32)],
        compiler_params=_params("parallel", "arbitrary"),
        name="nsa_prompt",
    )(slopes, q, gates, cmp, kv3, win3, eg, ovt)


def _nsa_sample_kernel(pt_ref, *refs, past, ts):
    pages = refs[:N_PAGES_ARG]
    (slope_ref, q_ref, gate_ref, cmp_ref, kvn_ref, win_ref, winn_ref, eg_ref, ovt_ref, o_ref) = refs[N_PAGES_ARG:]
    rows = H_B * ts
    nsel = -(-(past + ts) // L_SEL)
    lane_g = lax.broadcasted_iota(jnp.int32, (1, GD), 1) // DH_B
    q = q_ref[...]
    qq = jnp.concatenate([jnp.where(lane_g == g, q[:, h * GD:(h + 1) * GD], 0.0)
                          for g in range(G_B) for h in range(HPG)], axis=0).astype(BF16)
    slope = slope_ref[...]
    t_row = lax.broadcasted_iota(jnp.int32, (rows, 1), 0) % ts
    tpos = (past + t_row).astype(F32)
    kc = cmp_ref[0, 0]
    vc = cmp_ref[0, 1]
    ncb = kc.shape[0]
    n_f = lax.broadcasted_iota(jnp.int32, (1, ncb), 1).astype(F32)
    dist_c = tpos - (n_f * S_CMP + (L_CMP - 1.0))
    p = _masked_softmax(_dot_nt(qq, kc) - slope * dist_c, dist_c >= 0.0)
    o_cmp = _dot(p.astype(BF16), vc)
    ps = []
    for g in range(G_B):
        blk = [p[(g * HPG + h) * ts:(g * HPG + h + 1) * ts] for h in range(HPG)]
        ps.append((blk[0] + blk[1]) + (blk[2] + blk[3]))
    psum = _pad_rows(jnp.concatenate(ps, axis=0), ncb)
    phi, pmid, plo = _split3(psum)
    ovt = ovt_ref[...]
    imp_t = _dot_nt(ovt, phi) + _dot_nt(ovt, pmid) + _dot_nt(ovt, plo)
    nrow = ((nsel + 7) // 8) * 8
    cur_t = (past + lax.broadcasted_iota(jnp.int32, (nrow, ncb), 1) % ts) // L_SEL
    sel_t = _topk_mask_t(imp_t[0:nrow], cur_t, nsel, ncb)
    sel_gq = _dot_nt(_eye_bf16(ncb), _pad_rows(sel_t, ncb).astype(BF16))
    sel = jnp.concatenate([sel_gq[g * ts:(g + 1) * ts] for g in range(G_B) for _ in range(HPG)], axis=0).astype(BF16)

    jblk = lax.broadcasted_iota(jnp.int32, (ncb, PAGE_SIZE), 0)
    scol = lax.broadcasted_iota(jnp.int32, (ncb, PAGE_SIZE), 1)
    s_iota = lax.broadcasted_iota(jnp.int32, (1, PAGE_SIZE), 1)
    bpp = PAGE_SIZE // L_SEL

    def attend(key_tiles):
        m = jnp.full((rows, 1), NEG, F32)
        l = jnp.zeros((rows, 1), F32)
        acc = jnp.zeros((rows, GD), F32)
        for kt, vt, k0, n_real, use_sel in key_tiles:
            s = _dot_nt(qq, kt)
            dist = tpos - (k0 + s_iota).astype(F32)
            dist_m = dist if n_real == PAGE_SIZE else jnp.where(s_iota < n_real, dist, -1.0)
            if use_sel:
                esel = jnp.where(jblk == k0 // L_SEL + scol // L_SEL, 1.0, 0.0).astype(BF16)
                ok = jnp.where(dist_m >= 0.0, _dot(sel, esel), 0.0) > 0.5
            else:
                ok = jnp.abs(dist_m - 0.5 * WINDOW) <= 0.5 * WINDOW
            sm = jnp.where(ok, s - slope * dist, NEG)
            m, l, acc = _online_step(sm, vt, m, l, acc)
        return acc / jnp.maximum(l, 1e-30)

    slc_tiles = [(pg[0, :, 0:GD].astype(BF16), pg[0, :, GD:2 * GD].astype(BF16), pi * PAGE_SIZE, PAGE_SIZE, True)
                 for pi, pg in enumerate(pages)]
    kvn = kvn_ref[...]
    slc_tiles.append((_pad_rows(kvn[:, 0:GD], PAGE_SIZE).astype(BF16),
                      _pad_rows(kvn[:, GD:2 * GD], PAGE_SIZE).astype(BF16), past, ts, True))
    o_slc = attend(slc_tiles)

    wbuf = win_ref.shape[1]
    win_tiles = [(win_ref[0, wi * PAGE_SIZE:(wi + 1) * PAGE_SIZE, 0:GD].astype(BF16),
                  win_ref[0, wi * PAGE_SIZE:(wi + 1) * PAGE_SIZE, GD:2 * GD].astype(BF16),
                  past - wbuf + wi * PAGE_SIZE, PAGE_SIZE, False) for wi in range(wbuf // PAGE_SIZE)]
    wn = winn_ref[...]
    win_tiles.append((_pad_rows(wn[:, 0:GD], PAGE_SIZE).astype(BF16),
                      _pad_rows(wn[:, GD:2 * GD], PAGE_SIZE).astype(BF16), past, ts, False))
    o_win = attend(win_tiles)

    hi, mid, lo = _split3(gate_ref[...])
    eg = eg_ref[...]
    gexp = _dot(hi, eg) + _dot(mid, eg) + _dot(lo, eg)

    def gate_rows(br):
        return jnp.concatenate([gexp[:, br * D_MODEL + h * GD:br * D_MODEL + (h + 1) * GD]
                                for _ in range(G_B) for h in range(HPG)], axis=0)

    comb = gate_rows(0) * o_cmp + gate_rows(1) * o_slc + gate_rows(2) * o_win
    outs = []
    for h in range(HPG):
        acc = None
        for g in range(G_B):
            piece = jnp.where(lane_g == g, comb[(g * HPG + h) * ts:(g * HPG + h + 1) * ts], 0.0)
            acc = piece if acc is None else acc + piece
        outs.append(acc)
    o_ref[...] = jnp.concatenate(outs, axis=1).astype(o_ref.dtype)


def _nsa_sample_call(page_tab, cache3, slope_rows, q, gates, cmp, kvn, win3, winn, eg, ovt, b, ts):
    npg = page_tab.shape[1]
    assert npg == N_PAGES_ARG
    past = npg * PAGE_SIZE
    page_specs = [pl.BlockSpec((1, PAGE_SIZE, 2 * GD), functools.partial(lambda bi, pt, p: (pt[bi * N_PAGES_ARG + p], 0, 1), p=p))
                  for p in range(npg)]
    wbuf = win3.shape[1]
    grid_spec = pltpu.PrefetchScalarGridSpec(
        num_scalar_prefetch=1,
        grid=(b,),
        in_specs=page_specs + [
            pl.BlockSpec(slope_rows.shape, lambda bi, pt: (0, 0), pipeline_mode=pl.Buffered(1)),
            pl.BlockSpec((ts, D_MODEL), lambda bi, pt: (bi, 0)),
            pl.BlockSpec((ts, 128), lambda bi, pt: (bi, 0)),
            pl.BlockSpec((1, 2, cmp.shape[2], GD), lambda bi, pt: (bi, 0, 0, 0)),
            pl.BlockSpec((ts, 2 * GD), lambda bi, pt: (bi, 1)),
            pl.BlockSpec((1, wbuf, 2 * GD), lambda bi, pt: (bi, 0, 0)),
            pl.BlockSpec((ts, 2 * GD), lambda bi, pt: (bi, 0)),
            pl.BlockSpec(eg.shape, lambda bi, pt: (0, 0), pipeline_mode=pl.Buffered(1)),
            pl.BlockSpec(ovt.shape, lambda bi, pt: (0, 0), pipeline_mode=pl.Buffered(1))],
        out_specs=pl.BlockSpec((ts, D_MODEL), lambda bi, pt: (bi, 0)),
    )
    return pl.pallas_call(
        functools.partial(_nsa_sample_kernel, past=past, ts=ts),
        grid_spec=grid_spec,
        out_shape=jax.ShapeDtypeStruct((b * ts, D_MODEL), F32),
        compiler_params=_params("parallel"),
        name="nsa_sample",
    )(page_tab.reshape(-1), *([cache3] * npg), slope_rows, q, gates, cmp, kvn, win3, winn, eg, ovt)


def _overlap_t(nsel):
    n = np.arange(128)[None, :] * S_CMP
    j = np.arange(128)[:, None] * L_SEL
    ov = (n <= j + L_SEL - 1) & (n + L_CMP - 1 >= j) & (np.arange(128)[None, :] < 127) & (np.arange(128)[:, None] < nsel)
    return jnp.asarray(ov.astype(np.float32), dtype=BF16)


def _gate_expand():
    e = np.zeros((128, 3 * D_MODEL), np.float32)
    for br in range(3):
        for hh in range(H_B):
            e[br * H_B + hh, br * D_MODEL + hh * DH_B:br * D_MODEL + (hh + 1) * DH_B] = 1.0
    return jnp.asarray(e, dtype=BF16)


def _cmp_weights(pe, w1, w2):
    r_n = L_CMP // S_CMP
    pe_t = jnp.tile(pe.reshape(r_n, S_CMP, 1, DH_B), (1, 1, G_B, 1)).reshape(r_n, S_CMP, GD)
    eye = jnp.eye(G_B, dtype=F32)
    w1r = w1.reshape(r_n, S_CMP, DH_B, CMP_HID)
    w1b = jnp.einsum("rsdh,gk->rsgdkh", w1r, eye).reshape(r_n, S_CMP * GD, G_B * CMP_HID)
    w2b = jnp.einsum("hd,gk->ghkd", w2, eye).reshape(G_B * CMP_HID, GD)
    return pe_t, w1b.astype(BF16), w2b.astype(BF16)


def kernel(x_prompt, x_sample, c_prompt, c_sample, state_ret, state_conv, cache_kv, state_win, page_table, w_ada, b_ada, g_mix, g_ffn, w_ffn_in, conv_w, conv_b, w_ffn_out, w_ret_in, w_ret_out, g_kv, w_ada_kv, b_ada_kv, w_kv, pe_ck, pe_cv, w_ck1, w_ck2, w_cv1, w_cv2, w_nsa_in, w_nsa_out, g_final):
    bp, tp, d = x_prompt.shape
    bs, ts, _ = x_sample.shape
    depth = w_ada.shape[0]
    assert depth == 2 and d == D_MODEL and tp % SLC_TILE == 0 and ts == 8
    npg = page_table.shape[1]
    past = npg * PAGE_SIZE
    f2 = 2 * D_FF

    w_ada_all = jnp.concatenate([w_ada[0], w_ada[1], w_ada_kv], axis=1).astype(BF16)
    b_ada_all = jnp.concatenate([b_ada[0], b_ada[1], b_ada_kv], axis=0).reshape(1, -1)
    w_ret_in_b = w_ret_in[0].astype(BF16)
    w_ret_out_b = w_ret_out[0].astype(BF16)
    w_ffn_in_b = w_ffn_in.astype(BF16)
    w_ffn_out_b = w_ffn_out.astype(BF16)
    w_kv_b = w_kv.astype(BF16)
    w_kv_rows, w_kv_win = w_kv_b[:, :4 * GD], w_kv_b[:, 4 * GD:]
    wn = w_nsa_in[0]
    w_q = (wn[:, :H_B * DH_B].reshape(d, G_B, HPG, DH_B).transpose(0, 2, 1, 3).reshape(d, H_B * DH_B)
           * (DH_B ** -0.5)).astype(BF16)
    w_g = wn[:, H_B * DH_B:].reshape(d, G_B, HPG, 3).transpose(0, 3, 2, 1).reshape(d, 3 * H_B)
    w_g = jnp.pad(w_g, ((0, 0), (0, 128 - 3 * H_B))).astype(BF16)
    w_o = w_nsa_out[0].reshape(G_B, HPG, DH_B, d).transpose(1, 0, 2, 3).reshape(H_B * DH_B, d).astype(BF16)
    pe_k, w1_k, w2_k = _cmp_weights(pe_ck, w_ck1, w_ck2)
    pe_v, w1_v, w2_v = _cmp_weights(pe_cv, w_cv1, w_cv2)
    pe_c = jnp.stack([pe_k, pe_v])
    w1_c = jnp.stack([w1_k, w1_v])
    w2_c = jnp.stack([w2_k, w2_v])
    slopes = jnp.exp2(-8.0 * jnp.arange(1, H_B + 1, dtype=F32) / H_B)
    log_g = jnp.log1p(-jnp.exp2(-5.0 - jnp.arange(H_A, dtype=F32)))
    lg_tab = jnp.broadcast_to(log_g[:, None, None], (H_A, 1, 128))
    eg = _gate_expand()

    c_all = jnp.concatenate([c_prompt, c_sample], axis=0)
    mod = _ada_call(c_all, w_ada_all, b_ada_all)

    def mods(lo, hi):
        def cols(k):
            return mod[lo:hi, k * d:(k + 1) * d]
        return [[cols(l * 6 + k) for k in range(6)] for l in range(depth)] + [[cols(12), cols(13)]]

    def trunk(x3, mod_l, s_ret, conv_buf, sample):
        b, t, _ = x3.shape
        m = b * t
        x = x3.reshape(m, d)
        tm = 256 if sample else 512
        act_dt = F32 if sample else BF16
        sh1, sc1, ga1, sh2, sc2, ga2 = mod_l[0]
        pr = _nmm_call(x, g_mix[0], sh1, sc1, w_ret_in_b, t, tm, 1536, F32, name="ret_in")
        og, s_new = _ret_call(pr, s_ret, lg_tab, b, t, act_dt)
        x = _mm_res_call(og, w_ret_out_b, x, ga1, t, tm, name="ret_out")
        x, conv0 = _ffn_call(x, g_ffn[0], sh2, sc2, ga2, w_ffn_in_b[0], conv_w[0], conv_b[0], conv_buf[0],
                             w_ffn_out_b[0], b, t, tm, name="ffn0")
        sh1, sc1, ga1, sh2, sc2, ga2 = mod_l[1]
        sh_kv, sc_kv = mod_l[2]
        kv_rows = _nmm_call(x, g_kv, sh_kv, sc_kv, w_kv_rows, t, tm, 4 * GD, F32, name="kv_rows")
        win_rows = _nmm_call(x, g_kv, sh_kv, sc_kv, w_kv_win, t, tm, 2 * GD, F32, name="win_rows")
        q = _nmm_call(x, g_mix[1], sh1, sc1, w_q, t, tm, H_B * DH_B, act_dt, name="q_proj")
        gates = _nmm_call(x, g_mix[1], sh1, sc1, w_g, t, tm, 128, F32, act="sigmoid", name="gate_proj")
        if sample:
            cache3 = cache_kv.reshape(cache_kv.shape[0], PAGE_SIZE, 4 * GD)
            cmp = _cmp_call(cache3, page_table, pe_c, w1_c, w2_c, 2)
            slope_rows = jnp.repeat(slopes, t).reshape(H_B * t, 1)
            win3 = state_win.reshape(b, state_win.shape[1], 2 * GD)
            o = _nsa_sample_call(page_table, cache3, slope_rows, q, gates, cmp, kv_rows, win3, win_rows,
                                 eg, _overlap_t(-(-(past + t) // L_SEL)), b, t)
            new_win = jnp.concatenate([win3, win_rows.reshape(b, t, 2 * GD)], axis=1)[:, -min(WINDOW, win3.shape[1] + t):]
        else:
            assert t == N_PAGES_ARG * PAGE_SIZE
            rows3 = kv_rows.reshape(b * N_PAGES_ARG, PAGE_SIZE, 4 * GD)
            ident = jnp.arange(b * N_PAGES_ARG, dtype=jnp.int32).reshape(b, N_PAGES_ARG)
            cmp = _cmp_call(rows3, ident, pe_c, w1_c, w2_c, 2)
            o = _nsa_prompt_call(slopes, q, gates, cmp, kv_rows.reshape(b, t, 4 * GD), win_rows.reshape(b, t, 2 * GD),
                                 eg, _overlap_t(t // L_SEL), b, t)
            new_win = win_rows.reshape(b, t, 2 * GD)[:, -min(WINDOW, t):]
        x = _mm_res_call(o, w_o, x, ga1, t, tm, name="nsa_out")
        y, conv1 = _ffn_call(x, g_ffn[1], sh2, sc2, ga2, w_ffn_in_b[1], conv_w[1], conv_b[1], conv_buf[1],
                             w_ffn_out_b[1], b, t, tm, g_final=g_final, name="ffn1")
        return (y.reshape(b, t, d), s_new[None], jnp.stack([conv0, conv1]),
                kv_rows.reshape(b, t, 4, G_B, DH_B), new_win.reshape(b, new_win.shape[1], 2, G_B, DH_B))

    conv0_p = jnp.zeros((depth, bp, CONV_W - 1, f2), F32)
    y_p, ret_p, conv_p, kv_p, win_p = trunk(x_prompt, mods(0, bp), None, conv0_p, False)
    y_s, ret_s, conv_s, kv_s, win_s = trunk(x_sample, mods(bp, bp + bs), state_ret[0].astype(F32), state_conv, True)
    return (y_p, y_s, ret_p, ret_s, conv_p, conv_s, kv_p, kv_s, win_p, win_s)
```

```python
import functools

import numpy as np
import jax
import jax.numpy as jnp
from jax import lax
from jax.experimental import pallas as pl
from jax.experimental.pallas import tpu as pltpu

F32 = jnp.float32
BF16 = jnp.bfloat16

D_MODEL = 1024
H_A = 4
DK_A = D_MODEL // H_A
DV_A = 2 * DK_A
RET_CHUNK = 128
H_B = 16
DH_B = D_MODEL // H_B
G_B = 4
HPG = H_B // G_B
GD = G_B * DH_B
L_CMP = 32
S_CMP = 16
L_SEL = 64
N_SEL = 16
WINDOW = 512
CMP_HID = 2 * DH_B
Q_BLOCK = 128
FORCE_BONUS = 1e4
D_FF = 2816
CONV_W = 3
EPS = 1e-6
PAGE_SIZE = 128
NEG = -1e30
BIG = 2.0 ** 100
M_INIT = -(2.0 ** 101)

LANES = 128
VMEM_LIMIT = 56 * 1024 * 1024
FF_CHUNK = 256
SLC_TILE = 512
N_PAGES_ARG = 16

F_POS = DH_B
F_ONE = DH_B + 6
F_SEL = DH_B + 8
A_W = 2 * DH_B

_NT = (((1,), (1,)), ((), ()))


def _sigmoid(x):
    return 1.0 / (1.0 + jnp.exp(-x))


def _silu(x):
    return x * _sigmoid(x)


def _dot(a, b):
    return jnp.dot(a, b, preferred_element_type=F32)


def _dot_nt(a, b):
    return lax.dot_general(a, b, _NT, preferred_element_type=F32)


def _split3(x):
    hi = x.astype(BF16)
    r = x - hi.astype(F32)
    mid = r.astype(BF16)
    lo = (r - mid.astype(F32)).astype(BF16)
    return hi, mid, lo


def _params(*sem):
    return pltpu.CompilerParams(dimension_semantics=sem, vmem_limit_bytes=VMEM_LIMIT)


def _const_spec(shape):
    nd = len(shape)
    return pl.BlockSpec(shape, lambda *a: (0,) * nd, pipeline_mode=pl.Buffered(1))


def _mod_arg(v, m, tm, rows_per_batch):
    b, d = v.shape
    if rows_per_batch % tm == 0:
        tpb = rows_per_batch // tm
        return v.reshape(b, 1, d), (1, 1, d), (lambda i: (i // tpb, 0, 0))
    assert tm % rows_per_batch == 0
    arr = jnp.repeat(v, rows_per_batch, axis=0).reshape(m // tm, tm, d)
    return arr, (1, tm, d), (lambda i: (i, 0, 0))


def _ada_kernel(c_ref, w_ref, b_ref, o_ref):
    c = c_ref[...]
    o_ref[...] = _dot(_silu(c).astype(BF16), w_ref[...]) + b_ref[...]


def _ada_call(c, w, b, tn=2048):
    m, d = c.shape
    n = w.shape[1]
    assert n % tn == 0
    return pl.pallas_call(
        _ada_kernel,
        grid=(n // tn,),
        in_specs=[pl.BlockSpec((m, d), lambda j: (0, 0)),
                  pl.BlockSpec((d, tn), lambda j: (0, j)),
                  pl.BlockSpec((1, tn), lambda j: (0, j))],
        out_specs=pl.BlockSpec((m, tn), lambda j: (0, j)),
        out_shape=jax.ShapeDtypeStruct((m, n), F32),
        compiler_params=_params("parallel"),
        name="ada",
    )(c, w, b)


def _norm_mod(x, g, sh, sc):
    y = x * lax.rsqrt(jnp.mean(x * x, axis=-1, keepdims=True) + EPS) * g
    return y * (1.0 + sc) + sh


def _nmm_kernel(x_ref, g_ref, sh_ref, sc_ref, w_ref, o_ref, h_scr, *, act):
    @pl.when(pl.program_id(1) == 0)
    def _():
        h_scr[...] = _norm_mod(x_ref[...], g_ref[...], sh_ref[0], sc_ref[0]).astype(BF16)

    acc = _dot(h_scr[...], w_ref[...])
    if act == "sigmoid":
        acc = _sigmoid(acc)
    o_ref[...] = acc.astype(o_ref.dtype)


def _nmm_call(x, g, sh, sc, w, rows_per_batch, tm, tn, out_dtype, act=None, name="nmm"):
    m, d = x.shape
    n = w.shape[1]
    assert m % tm == 0 and n % tn == 0
    sh_a, mshape, mmap = _mod_arg(sh, m, tm, rows_per_batch)
    sc_a, _, _ = _mod_arg(sc, m, tm, rows_per_batch)
    return pl.pallas_call(
        functools.partial(_nmm_kernel, act=act),
        grid=(m // tm, n // tn),
        in_specs=[pl.BlockSpec((tm, d), lambda i, j: (i, 0)),
                  pl.BlockSpec((1, d), lambda i, j: (0, 0)),
                  pl.BlockSpec(mshape, lambda i, j: mmap(i)),
                  pl.BlockSpec(mshape, lambda i, j: mmap(i)),
                  pl.BlockSpec((d, tn), lambda i, j: (0, j))],
        out_specs=pl.BlockSpec((tm, tn), lambda i, j: (i, j)),
        out_shape=jax.ShapeDtypeStruct((m, n), out_dtype),
        scratch_shapes=[pltpu.VMEM((tm, d), BF16)],
        compiler_params=_params("parallel", "arbitrary"),
        name=name,
    )(x, g.reshape(1, d), sh_a, sc_a, w)


def _kvt_kernel(x_ref, g_ref, sh_ref, sc_ref, wt_ref, kv_ref, win_ref):
    h = _norm_mod(x_ref[...], g_ref[...], sh_ref[0], sc_ref[0]).astype(BF16)
    n_kv = kv_ref.shape[1]
    kv_ref[0] = _dot_nt(wt_ref[0:n_kv, :], h)
    win_ref[0] = _dot_nt(wt_ref[n_kv:, :], h)


def _kvt_call(x, g, sh, sc, w_t, b, t, tm):
    m, d = x.shape
    n = w_t.shape[0]
    nt = t // tm
    sh_a, mshape, mmap = _mod_arg(sh, m, tm, t)
    sc_a, _, _ = _mod_arg(sc, m, tm, t)
    return pl.pallas_call(
        _kvt_kernel,
        grid=(b, nt),
        in_specs=[pl.BlockSpec((tm, d), lambda bi, ti: (bi * nt + ti, 0)),
                  pl.BlockSpec((1, d), lambda bi, ti: (0, 0)),
                  pl.BlockSpec(mshape, lambda bi, ti: mmap(bi * nt + ti)),
                  pl.BlockSpec(mshape, lambda bi, ti: mmap(bi * nt + ti)),
                  _const_spec((n, d))],
        out_specs=[pl.BlockSpec((1, 4 * GD, tm), lambda bi, ti: (bi, 0, ti)),
                   pl.BlockSpec((1, 2 * GD, tm), lambda bi, ti: (bi, 0, ti))],
        out_shape=[jax.ShapeDtypeStruct((b, 4 * GD, t), F32),
                   jax.ShapeDtypeStruct((b, 2 * GD, t), F32)],
        compiler_params=_params("parallel", "parallel"),
        name="kv_proj_t",
    )(x, g.reshape(1, d), sh_a, sc_a, w_t)


def _mm_res_kernel(a_ref, w_ref, res_ref, ga_ref, o_ref):
    y = _dot(a_ref[...].astype(BF16), w_ref[...])
    o_ref[...] = res_ref[...] + ga_ref[0] * y


def _mm_res_call(a, w, res, gate, rows_per_batch, tm, name="mm_res"):
    m, k = a.shape
    d = w.shape[1]
    ga_a, mshape, mmap = _mod_arg(gate, m, tm, rows_per_batch)
    return pl.pallas_call(
        _mm_res_kernel,
        grid=(m // tm,),
        in_specs=[pl.BlockSpec((tm, k), lambda i: (i, 0)),
                  _const_spec((k, d)),
                  pl.BlockSpec((tm, d), lambda i: (i, 0)),
                  pl.BlockSpec(mshape, lambda i: mmap(i))],
        out_specs=pl.BlockSpec((tm, d), lambda i: (i, 0)),
        out_shape=jax.ShapeDtypeStruct((m, d), F32),
        compiler_params=_params("parallel"),
        name=name,
    )(a, w, res, ga_a)


def _ret_kernel(*refs, c, nc, has_s0):
    if has_s0:
        lg_ref, q_ref, k_ref, v_ref, g_ref, s0_ref, o_ref, so_ref, s_scr = refs
    else:
        lg_ref, q_ref, k_ref, v_ref, g_ref, o_ref, so_ref, s_scr = refs
        s0_ref = None
    n = pl.program_id(2)

    @pl.when(n == 0)
    def _():
        if has_s0:
            s_scr[...] = s0_ref[0, 0]
        else:
            s_scr[...] = jnp.zeros_like(s_scr)

    cp = max(c, RET_CHUNK)

    def padded(ref):
        a = ref[...].astype(F32)
        if cp > c:
            a = jnp.concatenate([a, jnp.zeros((cp - c, a.shape[1]), F32)], axis=0)
        return a

    lg = lg_ref[0][:, 0:1]
    q = padded(q_ref).astype(BF16)
    kf = padded(k_ref) * (DK_A ** -0.5)
    v = padded(v_ref).astype(BF16)
    i = lax.broadcasted_iota(jnp.int32, (cp, 1), 0).astype(F32)
    j = lax.broadcasted_iota(jnp.int32, (1, cp), 1).astype(F32)
    diff = i - j
    dmask = jnp.where(diff >= 0, jnp.exp(jnp.maximum(diff, 0.0) * lg), 0.0)
    scores = _dot_nt(q, kf.astype(BF16)) * dmask
    s_old = s_scr[...]
    o = _dot(scores.astype(BF16), v) + _dot(q, s_old.astype(BF16)) * jnp.exp((i + 1.0) * lg)
    w = jnp.exp((c - 1.0 - i) * lg)
    kw_t = (kf * w).T.astype(BF16)
    s_new = jnp.exp(c * lg) * s_old + _dot(kw_t, v)
    s_scr[...] = s_new
    of = o * lax.rsqrt(jnp.mean(o * o, axis=-1, keepdims=True) + EPS)
    gate = g_ref[...].astype(F32)
    o_ref[...] = (of[0:c] * _silu(gate)).astype(o_ref.dtype)

    @pl.when(n == nc - 1)
    def _():
        so_ref[0, 0] = s_new


def _ret_call(pr, s0, lg_tab, b, t, out_dtype):
    c = RET_CHUNK if t % RET_CHUNK == 0 else t
    nc = t // c
    has_s0 = s0 is not None
    qb, vb = H_A, (2 * H_A * DK_A) // DV_A
    gb = vb + H_A
    in_specs = [pl.BlockSpec((1, 1, LANES), lambda bi, h, n: (h, 0, 0)),
                pl.BlockSpec((c, DK_A), lambda bi, h, n: (bi * nc + n, h)),
                pl.BlockSpec((c, DK_A), lambda bi, h, n: (bi * nc + n, qb + h)),
                pl.BlockSpec((c, DV_A), lambda bi, h, n: (bi * nc + n, vb + h)),
                pl.BlockSpec((c, DV_A), lambda bi, h, n: (bi * nc + n, gb + h))]
    args = [lg_tab, pr, pr, pr, pr]
    if has_s0:
        in_specs.append(pl.BlockSpec((1, 1, DK_A, DV_A), lambda bi, h, n: (bi, h, 0, 0)))
        args.append(s0)
    return pl.pallas_call(
        functools.partial(_ret_kernel, c=c, nc=nc, has_s0=has_s0),
        grid=(b, H_A, nc),
        in_specs=in_specs,
        out_specs=[pl.BlockSpec((c, DV_A), lambda bi, h, n: (bi * nc + n, h)),
                   pl.BlockSpec((1, 1, DK_A, DV_A), lambda bi, h, n: (bi, h, 0, 0))],
        out_shape=[jax.ShapeDtypeStruct((b * t, H_A * DV_A), out_dtype),
                   jax.ShapeDtypeStruct((b, H_A, DK_A, DV_A), F32)],
        scratch_shapes=[pltpu.VMEM((DK_A, DV_A), F32)],
        compiler_params=_params("parallel", "parallel", "arbitrary"),
        name="retention",
    )(*args)


def _ffn_kernel(*refs, t_in, nbt, final):
    if final:
        (x_ref, g_ref, sh_ref, sc_ref, ga_ref, win_ref, cw_ref, cb_ref, cbuf_ref, wout_ref, gf_ref,
         o_ref, nc_ref, h_scr, carry_scr, acc_scr) = refs
    else:
        (x_ref, g_ref, sh_ref, sc_ref, ga_ref, win_ref, cw_ref, cb_ref, cbuf_ref, wout_ref,
         o_ref, nc_ref, h_scr, carry_scr, acc_scr) = refs
        gf_ref = None
    tm = x_ref.shape[0]

    @pl.when(pl.program_id(1) == 0)
    def _():
        carry_scr[...] = cbuf_ref[...]

    x = x_ref[...]
    h_scr[...] = _norm_mod(x, g_ref[...], sh_ref[0], sc_ref[0]).astype(BF16)
    cwd = FF_CHUNK
    tpos = lax.broadcasted_iota(jnp.int32, (nbt, t_in, cwd), 1)
    is0 = tpos == 0
    is1 = tpos == 1

    def conv_part(off):
        u = _dot(h_scr[...], win_ref[:, off:off + cwd]).reshape(nbt, t_in, cwd)
        prev = carry_scr[:, :, off:off + cwd]
        p0 = prev[:, 0:1, :]
        p1 = prev[:, 1:2, :]
        r1 = pltpu.roll(u, 1, axis=1)
        r2 = pltpu.roll(u, 2, axis=1)
        um1 = jnp.where(is0, p1, r1)
        um2 = jnp.where(is0, p0, jnp.where(is1, p1, r2))
        newc = r2[:, 0:2, :]
        carry_scr[:, :, off:off + cwd] = newc
        nc_ref[:, :, off:off + cwd] = newc
        cw = cw_ref[:, off:off + cwd]
        z = cb_ref[:, off:off + cwd] + cw[0:1] * um2
        z = z + cw[1:2] * um1
        z = z + cw[2:3] * u
        return z.reshape(tm, cwd)

    for ch in range(D_FF // cwd):
        za = conv_part(ch * cwd)
        zg = conv_part(D_FF + ch * cwd)
        act = (_silu(zg) * za).astype(BF16)
        y = _dot(act, wout_ref[ch * cwd:(ch + 1) * cwd, :])
        if ch == 0:
            acc_scr[...] = y
        else:
            acc_scr[...] += y
    xn = x + ga_ref[0] * acc_scr[...]
    if final:
        xn = xn * lax.rsqrt(jnp.mean(xn * xn, axis=-1, keepdims=True) + EPS) * gf_ref[...]
    o_ref[...] = xn


def _ffn_call(x, g, sh, sc, ga, w_in, cw, cb, cbuf, w_out, b, t, tm, g_final=None, name="ffn"):
    m, d = x.shape
    f2 = w_in.shape[1]
    if t % tm == 0:
        t_in, nbt, nb_tiles, nt = tm, 1, b, t // tm
    else:
        assert tm % t == 0 and m % tm == 0
        t_in, nbt, nb_tiles, nt = t, tm // t, m // tm, 1
    sh_a, mshape, mmap = _mod_arg(sh, m, tm, t)
    sc_a, _, _ = _mod_arg(sc, m, tm, t)
    ga_a, _, _ = _mod_arg(ga, m, tm, t)
    final = g_final is not None
    in_specs = [pl.BlockSpec((tm, d), lambda bi, ti: (bi * nt + ti, 0)),
                pl.BlockSpec((1, d), lambda bi, ti: (0, 0)),
                pl.BlockSpec(mshape, lambda bi, ti: mmap(bi * nt + ti)),
                pl.BlockSpec(mshape, lambda bi, ti: mmap(bi * nt + ti)),
                pl.BlockSpec(mshape, lambda bi, ti: mmap(bi * nt + ti)),
                _const_spec((d, f2)),
                _const_spec((CONV_W, f2)),
                _const_spec((1, f2)),
                pl.BlockSpec((nbt, CONV_W - 1, f2), lambda bi, ti: (bi, 0, 0)),
                _const_spec((D_FF, d))]
    args = [x, g.reshape(1, d), sh_a, sc_a, ga_a, w_in, cw, cb.reshape(1, f2), cbuf, w_out]
    if final:
        in_specs.append(pl.BlockSpec((1, d), lambda bi, ti: (0, 0)))
        args.append(g_final.reshape(1, d))
    return pl.pallas_call(
        functools.partial(_ffn_kernel, t_in=t_in, nbt=nbt, final=final),
        grid=(nb_tiles, nt),
        in_specs=in_specs,
        out_specs=[pl.BlockSpec((tm, d), lambda bi, ti: (bi * nt + ti, 0)),
                   pl.BlockSpec((nbt, CONV_W - 1, f2), lambda bi, ti: (bi, 0, 0))],
        out_shape=[jax.ShapeDtypeStruct((m, d), F32),
                   jax.ShapeDtypeStruct((b, CONV_W - 1, f2), F32)],
        scratch_shapes=[pltpu.VMEM((tm, d), BF16),
                        pltpu.VMEM((nbt, CONV_W - 1, f2), F32),
                        pltpu.VMEM((tm, d), F32)],
        compiler_params=_params("parallel", "arbitrary"),
        name=name,
    )(*args)


def _cmp_kernel(pt_ref, *refs):
    pages = refs[:N_PAGES_ARG]
    pe_ref, w1_ref, w2t_ref, o_ref = refs[N_PAGES_ARG:]
    npc = PAGE_SIZE // S_CMP
    r_n = L_CMP // S_CMP
    for kind in range(2):
        rows = jnp.concatenate([pg[0, kind * GD:(kind + 1) * GD, :].T for pg in pages], axis=0)
        rows = jnp.swapaxes(rows.reshape(len(pages) * npc, S_CMP, GD), 0, 1)
        pieces = [rows[s] for s in range(S_CMP)]
        z = None
        for r in range(r_n):
            a = jnp.concatenate([(pieces[s] + pe_ref[kind, r, s:s + 1, :]).astype(BF16)
                                 for s in range(S_CMP)], axis=1)
            y = _dot(a, w1_ref[kind, r])
            if r > 0:
                y = pltpu.roll(y, y.shape[0] - r, axis=0)
            z = y if z is None else z + y
        o_ref[0, kind] = _dot_nt(w2t_ref[kind], _silu(z).astype(BF16)).astype(o_ref.dtype)


def _cmp_call(rows_t, page_map, page_tab, pe, w1, w2t):
    b, npg = page_tab.shape
    assert npg == N_PAGES_ARG
    n_pieces = npg * PAGE_SIZE // S_CMP
    page_specs = [pl.BlockSpec((1, 2 * GD, PAGE_SIZE), functools.partial(lambda bi, pt, p: page_map(bi, p, pt), p=p))
                  for p in range(npg)]
    grid_spec = pltpu.PrefetchScalarGridSpec(
        num_scalar_prefetch=1,
        grid=(b,),
        in_specs=page_specs + [
            pl.BlockSpec(pe.shape, lambda bi, pt: (0, 0, 0, 0), pipeline_mode=pl.Buffered(1)),
            pl.BlockSpec(w1.shape, lambda bi, pt: (0, 0, 0, 0), pipeline_mode=pl.Buffered(1)),
            pl.BlockSpec(w2t.shape, lambda bi, pt: (0, 0, 0), pipeline_mode=pl.Buffered(1))],
        out_specs=pl.BlockSpec((1, 2, GD, n_pieces), lambda bi, pt: (bi, 0, 0, 0)),
    )
    return pl.pallas_call(
        _cmp_kernel,
        grid_spec=grid_spec,
        out_shape=jax.ShapeDtypeStruct((b, 2, GD, n_pieces), BF16),
        compiler_params=_params("parallel"),
        name="compress",
    )(page_tab.reshape(-1), *([rows_t] * npg), pe, w1, w2t)


def _masked_softmax(s, ok):
    sm = jnp.where(ok, s, NEG)
    m = jnp.max(sm, axis=-1, keepdims=True)
    e = jnp.where(ok, jnp.exp(sm - m), 0.0)
    return e / jnp.maximum(jnp.sum(e, axis=-1, keepdims=True), 1e-30)


def _topk_mask_t(imp_t, cur, nsel, ncol):
    rows = imp_t.shape[0]
    jj = lax.broadcasted_iota(jnp.int32, (rows, ncol), 0)
    forced = (jj == 0) | (jj == cur) | (jj == cur - 1)
    score = jnp.where(jj <= cur, imp_t + FORCE_BONUS * forced.astype(F32), NEG)
    rank = jnp.zeros((rows, ncol), F32)
    for i in range(nsel):
        row = score[i:i + 1, :]
        lower = jnp.where(jj > i, 1.0, 0.0)
        rank = rank + jnp.where(row > score, 1.0, jnp.where(row == score, lower, 0.0))
    return jnp.where((rank < float(min(N_SEL, nsel))) & (jj < nsel), 1.0, 0.0)


def _eye_bf16(n):
    return jnp.where(lax.broadcasted_iota(jnp.int32, (n, n), 0) == lax.broadcasted_iota(jnp.int32, (n, n), 1),
                     1.0, 0.0).astype(BF16)


def _pad_rows(a, rows):
    if a.shape[0] == rows:
        return a
    return jnp.concatenate([a, jnp.zeros((rows - a.shape[0],) + a.shape[1:], a.dtype)], axis=0)


def _online_step(sm, v, m_old, l_old, acc_old, v_transposed):
    m_new = jnp.maximum(m_old, jnp.max(sm, axis=-1, keepdims=True))
    alpha = jnp.exp(m_old - m_new)
    e = jnp.where(sm > 0.5 * NEG, jnp.exp(sm - m_new), 0.0)
    l_new = alpha * l_old + jnp.sum(e, axis=-1, keepdims=True)
    pv = _dot_nt(e.astype(BF16), v) if v_transposed else _dot(e.astype(BF16), v)
    return m_new, l_new, alpha * acc_old + pv


def _block_importance_t(ps_sum, ovt):
    hi, mid, lo = _split3(ps_sum)
    return _dot_nt(ovt, hi) + _dot_nt(ovt, mid) + _dot_nt(ovt, lo)


def _nsa_prompt_kernel(q_ref, gate_ref, cmp_ref, kv_ref, win_ref, featc_ref, fslc_ref, fwin_ref, fcmp_ref,
                       ovt_ref, band_ref, o_ref,
                       kb_slc, vt_slc, kb_win, vt_win, kb_cmp, a_scr, g_scr, out_scr, *, t_len):
    qb = pl.program_id(1)
    s0 = qb * Q_BLOCK
    nsel = t_len // L_SEL
    n_st = t_len // SLC_TILE
    n_wt = WINDOW // Q_BLOCK
    n_wtiles = kb_win.shape[1]
    rows_g = HPG * Q_BLOCK

    @pl.when(qb == 0)
    def _():
        for g in range(G_B):
            for kt in range(n_st):
                cs = slice(kt * SLC_TILE, (kt + 1) * SLC_TILE)
                kb_slc[g, kt, 0:DH_B, :] = kv_ref[0, g * DH_B:(g + 1) * DH_B, cs].astype(BF16)
                kb_slc[g, kt, DH_B:A_W, :] = fslc_ref[:, cs]
            for wt in range(n_wtiles):
                cs = slice((wt - n_wt) * Q_BLOCK, (wt - n_wt + 1) * Q_BLOCK)
                if wt < n_wt:
                    kb_win[g, wt, 0:DH_B, :] = jnp.zeros((DH_B, Q_BLOCK), BF16)
                else:
                    kb_win[g, wt, 0:DH_B, :] = win_ref[0, g * DH_B:(g + 1) * DH_B, cs].astype(BF16)
                kb_win[g, wt, DH_B:A_W, :] = fwin_ref[:, wt * Q_BLOCK:(wt + 1) * Q_BLOCK]
            kb_cmp[g, 0:DH_B, :] = cmp_ref[0, 0, g * DH_B:(g + 1) * DH_B, :]
            kb_cmp[g, DH_B:A_W, :] = fcmp_ref[...]
        for kt in range(n_st):
            vt_slc[kt] = kv_ref[0, GD:2 * GD, kt * SLC_TILE:(kt + 1) * SLC_TILE].astype(BF16)
        for wt in range(n_wtiles):
            if wt < n_wt:
                vt_win[wt] = jnp.zeros((GD, Q_BLOCK), BF16)
            else:
                vt_win[wt] = win_ref[0, GD:2 * GD, (wt - n_wt) * Q_BLOCK:(wt - n_wt + 1) * Q_BLOCK].astype(BF16)

    lane = lax.broadcasted_iota(jnp.int32, (1, A_W), 1)
    for head in range(H_B):
        g, hl = divmod(head, HPG)
        a_scr[g, hl * Q_BLOCK:(hl + 1) * Q_BLOCK, :] = jnp.where(
            lane < DH_B, q_ref[:, head * A_W:(head + 1) * A_W], featc_ref[head:head + 1, :].astype(BF16))
    for g in range(G_B):
        g_scr[g] = gate_ref[:, g * LANES:(g + 1) * LANES]

    ovt = ovt_ref[...]
    eye = _eye_bf16(Q_BLOCK)
    t_loc = lax.broadcasted_iota(jnp.int32, (Q_BLOCK, 1), 0)
    tpos = (s0 + t_loc).astype(F32)
    n_f = lax.broadcasted_iota(jnp.int32, (1, Q_BLOCK), 1).astype(F32)
    ok_c = (tpos - (n_f * S_CMP + (L_CMP - 1.0))) >= 0.0
    band = band_ref[...]
    nrow = ((nsel + 7) // 8) * 8

    def group_body(g, carry):
        ag = a_scr[g]
        row_g = pl.multiple_of(g * DH_B, DH_B)
        s_c = _dot(ag, kb_cmp[g]).reshape(HPG, Q_BLOCK, Q_BLOCK)
        p3 = _masked_softmax(s_c, ok_c[None])
        o_cmp = _dot_nt(p3.reshape(rows_g, Q_BLOCK).astype(BF16), cmp_ref[0, 1, pl.ds(row_g, DH_B), :])
        imp_t = _block_importance_t((p3[0] + p3[1]) + (p3[2] + p3[3]), ovt)
        cur_t = (s0 + lax.broadcasted_iota(jnp.int32, (nrow, Q_BLOCK), 1)) // L_SEL
        sel_t = _topk_mask_t(imp_t[0:nrow], cur_t, nsel, Q_BLOCK)
        nsel_pad = jnp.concatenate([jnp.zeros((F_SEL, Q_BLOCK), F32), 1.0 - sel_t,
                                    jnp.zeros((A_W - F_SEL - nrow, Q_BLOCK), F32)], axis=0).astype(BF16)
        nsl = _dot_nt(eye, nsel_pad).astype(BF16)
        ag2 = ag + jnp.concatenate([nsl] * HPG, axis=0)

        def slc_tile(kti, m, l, acc, diag):
            s = _dot(ag2, kb_slc[g, kti])
            if diag:
                u = lax.broadcasted_iota(jnp.int32, (Q_BLOCK, SLC_TILE), 1)
                ok = u <= (s0 - kti * SLC_TILE) + t_loc
                s = jnp.where(ok[None], s.reshape(HPG, Q_BLOCK, SLC_TILE), -BIG).reshape(rows_g, SLC_TILE)
            m_new = jnp.maximum(m, jnp.max(s, axis=-1, keepdims=True))
            alpha = jnp.exp(m - m_new)
            e = jnp.exp(s - m_new)
            l_new = alpha * l + jnp.sum(e, axis=-1, keepdims=True)
            acc_new = alpha * acc + _dot_nt(e.astype(BF16), vt_slc[kti, pl.ds(row_g, DH_B), :])
            return m_new, l_new, acc_new

        init = (jnp.full((rows_g, 1), M_INIT, F32), jnp.zeros((rows_g, 1), F32), jnp.zeros((rows_g, DH_B), F32))
        kd = s0 // SLC_TILE
        m, l, acc = lax.fori_loop(0, kd, lambda kti, c: slc_tile(kti, *c, diag=False), init)
        m, l, acc = slc_tile(kd, m, l, acc, diag=True)
        o_slc = acc / jnp.maximum(l, 1e-30)

        kw = jnp.concatenate([kb_win[g, qb + w] for w in range(n_wt + 1)], axis=1)
        vw = jnp.concatenate([vt_win[qb + w, pl.ds(row_g, DH_B), :] for w in range(n_wt + 1)], axis=1)
        s_w = (_dot(ag2, kw).reshape(HPG, Q_BLOCK, WINDOW + Q_BLOCK) + band[None]).reshape(rows_g, WINDOW + Q_BLOCK)
        e_w = jnp.exp(s_w - jnp.max(s_w, axis=-1, keepdims=True))
        o_win = _dot_nt(e_w.astype(BF16), vw) / jnp.maximum(jnp.sum(e_w, axis=-1, keepdims=True), 1e-30)

        gt = g_scr[g]
        pieces = []
        for hl in range(HPG):
            r = slice(hl * Q_BLOCK, (hl + 1) * Q_BLOCK)
            comb = gt[:, hl:hl + 1] * o_cmp[r]
            comb = comb + gt[:, HPG + hl:HPG + hl + 1] * o_slc[r]
            comb = comb + gt[:, 2 * HPG + hl:2 * HPG + hl + 1] * o_win[r]
            pieces.append(comb)
        out_scr[g] = jnp.concatenate(pieces, axis=1)
        return carry

    lax.fori_loop(0, G_B, group_body, 0)
    o_ref[...] = jnp.concatenate([out_scr[g] for g in range(G_B)], axis=1).astype(o_ref.dtype)


def _nsa_prompt_call(q, gates, cmp_t, kv_t, win_t, consts, b, t):
    featc, fslc, fwin, fcmp, ovt, band = consts
    nqb = t // Q_BLOCK
    n_wtiles = (WINDOW + t) // Q_BLOCK
    return pl.pallas_call(
        functools.partial(_nsa_prompt_kernel, t_len=t),
        grid=(b, nqb),
        in_specs=[pl.BlockSpec((Q_BLOCK, H_B * A_W), lambda bi, qi: (bi * nqb + qi, 0)),
                  pl.BlockSpec((Q_BLOCK, G_B * LANES), lambda bi, qi: (bi * nqb + qi, 0)),
                  pl.BlockSpec((1, 2, GD, cmp_t.shape[3]), lambda bi, qi: (bi, 0, 0, 0)),
                  pl.BlockSpec((1, 2 * GD, t), lambda bi, qi: (bi, 1, 0)),
                  pl.BlockSpec((1, 2 * GD, t), lambda bi, qi: (bi, 0, 0)),
                  _const_spec(featc.shape), _const_spec(fslc.shape), _const_spec(fwin.shape),
                  _const_spec(fcmp.shape), _const_spec(ovt.shape), _const_spec(band.shape)],
        out_specs=pl.BlockSpec((Q_BLOCK, D_MODEL), lambda bi, qi: (bi * nqb + qi, 0)),
        out_shape=jax.ShapeDtypeStruct((b * t, D_MODEL), BF16),
        scratch_shapes=[pltpu.VMEM((G_B, t // SLC_TILE, A_W, SLC_TILE), BF16),
                        pltpu.VMEM((t // SLC_TILE, GD, SLC_TILE), BF16),
                        pltpu.VMEM((G_B, n_wtiles, A_W, Q_BLOCK), BF16),
                        pltpu.VMEM((n_wtiles, GD, Q_BLOCK), BF16),
                        pltpu.VMEM((G_B, A_W, Q_BLOCK), BF16),
                        pltpu.VMEM((G_B, HPG * Q_BLOCK, A_W), BF16),
                        pltpu.VMEM((G_B, Q_BLOCK, LANES), F32),
                        pltpu.VMEM((G_B, Q_BLOCK, HPG * DH_B), F32)],
        compiler_params=_params("parallel", "arbitrary"),
        name="nsa_prompt",
    )(q, gates, cmp_t, kv_t, win_t, featc, fslc, fwin, fcmp, ovt, band)


def _nsa_prompt_consts(t):
    slopes = jnp.exp2(-8.0 * jnp.arange(1, H_B + 1, dtype=F32) / H_B)
    featc = jnp.zeros((H_B, A_W), F32).at[:, F_ONE].set(1.0)
    for k, term in enumerate(_split3(slopes)):
        featc = featc.at[:, F_POS + 2 * k].set(term.astype(F32)).at[:, F_POS + 2 * k + 1].set(term.astype(F32))

    def pos_rows(pos):
        f = np.zeros((A_W - DH_B, pos.shape[0]), np.float32)
        hi = (pos // 256) * 256
        lo = pos % 256
        for k in range(3):
            f[F_POS - DH_B + 2 * k] = hi
            f[F_POS - DH_B + 2 * k + 1] = lo
        return f

    nsel = t // L_SEL
    spos = np.arange(t)
    fslc = pos_rows(spos)
    for j in range(nsel):
        fslc[F_SEL - DH_B + j] = np.where(spos // L_SEL == j, -BIG, 0.0)
    wcol = np.arange(WINDOW + t)
    fwin = pos_rows(wcol)
    fwin[F_ONE - DH_B] = np.where(wcol < WINDOW, -BIG, 0.0)
    fcmp = pos_rows(np.arange(Q_BLOCK) * S_CMP + (L_CMP - 1))
    u = np.arange(WINDOW + Q_BLOCK)[None, :]
    tl = np.arange(Q_BLOCK)[:, None]
    band = np.where((u >= tl) & (u <= tl + WINDOW), 0.0, -BIG).astype(np.float32)
    return (featc, jnp.asarray(fslc, dtype=BF16), jnp.asarray(fwin, dtype=BF16),
            jnp.asarray(fcmp, dtype=BF16), _overlap_t(nsel), jnp.asarray(band))


def _nsa_sample_kernel(pt_ref, *refs, past, ts):
    pages = refs[:N_PAGES_ARG]
    (slope_ref, q_ref, gate_ref, cmp_ref, kvn_ref, win_ref, winn_ref, eg_ref, ovt_ref, o_ref) = refs[N_PAGES_ARG:]
    rows = H_B * ts
    nsel = -(-(past + ts) // L_SEL)
    lane_g = lax.broadcasted_iota(jnp.int32, (1, GD), 1) // DH_B
    q = q_ref[...]
    qq = jnp.concatenate([jnp.where(lane_g == g, q[:, h * GD:(h + 1) * GD], 0.0)
                          for g in range(G_B) for h in range(HPG)], axis=0).astype(BF16)
    slope = slope_ref[...]
    t_row = lax.broadcasted_iota(jnp.int32, (rows, 1), 0) % ts
    tpos = (past + t_row).astype(F32)
    kc_t = cmp_ref[0, 0]
    vc_t = cmp_ref[0, 1]
    ncb = kc_t.shape[1]
    n_f = lax.broadcasted_iota(jnp.int32, (1, ncb), 1).astype(F32)
    dist_c = tpos - (n_f * S_CMP + (L_CMP - 1.0))
    p = _masked_softmax(_dot(qq, kc_t) - slope * dist_c, dist_c >= 0.0)
    o_cmp = _dot_nt(p.astype(BF16), vc_t)
    ps = []
    for g in range(G_B):
        blk = [p[(g * HPG + h) * ts:(g * HPG + h + 1) * ts] for h in range(HPG)]
        ps.append((blk[0] + blk[1]) + (blk[2] + blk[3]))
    psum = _pad_rows(jnp.concatenate(ps, axis=0), ncb)
    imp_t = _block_importance_t(psum, ovt_ref[...])
    nrow = ((nsel + 7) // 8) * 8
    cur_t = (past + lax.broadcasted_iota(jnp.int32, (nrow, ncb), 1) % ts) // L_SEL
    sel_t = _topk_mask_t(imp_t[0:nrow], cur_t, nsel, ncb)
    sel_gq = _dot_nt(_eye_bf16(ncb), _pad_rows(sel_t, ncb).astype(BF16))
    sel = jnp.concatenate([sel_gq[g * ts:(g + 1) * ts] for g in range(G_B) for _ in range(HPG)], axis=0).astype(BF16)

    jblk = lax.broadcasted_iota(jnp.int32, (ncb, PAGE_SIZE), 0)
    scol = lax.broadcasted_iota(jnp.int32, (ncb, PAGE_SIZE), 1)
    s_iota = lax.broadcasted_iota(jnp.int32, (1, PAGE_SIZE), 1)

    def attend(key_tiles):
        m = jnp.full((rows, 1), NEG, F32)
        l = jnp.zeros((rows, 1), F32)
        acc = jnp.zeros((rows, GD), F32)
        for kt, vt, k0, n_real, use_sel, transposed in key_tiles:
            s = _dot(qq, kt) if transposed else _dot_nt(qq, kt)
            dist = tpos - (k0 + s_iota).astype(F32)
            dist_m = dist if n_real == PAGE_SIZE else jnp.where(s_iota < n_real, dist, -1.0)
            if use_sel:
                esel = jnp.where(jblk == k0 // L_SEL + scol // L_SEL, 1.0, 0.0).astype(BF16)
                ok = jnp.where(dist_m >= 0.0, _dot(sel, esel), 0.0) > 0.5
            else:
                ok = jnp.abs(dist_m - 0.5 * WINDOW) <= 0.5 * WINDOW
            sm = jnp.where(ok, s - slope * dist, NEG)
            m, l, acc = _online_step(sm, vt, m, l, acc, transposed)
        return acc / jnp.maximum(l, 1e-30)

    slc_tiles = [(pg[0, 0:GD, :].astype(BF16), pg[0, GD:2 * GD, :].astype(BF16), pi * PAGE_SIZE, PAGE_SIZE, True, True)
                 for pi, pg in enumerate(pages)]
    kvn = kvn_ref[...]
    slc_tiles.append((_pad_rows(kvn[:, 0:GD], PAGE_SIZE).astype(BF16),
                      _pad_rows(kvn[:, GD:2 * GD], PAGE_SIZE).astype(BF16), past, ts, True, False))
    o_slc = attend(slc_tiles)

    wbuf = win_ref.shape[2]
    win_tiles = [(win_ref[0, 0:GD, wi * PAGE_SIZE:(wi + 1) * PAGE_SIZE].astype(BF16),
                  win_ref[0, GD:2 * GD, wi * PAGE_SIZE:(wi + 1) * PAGE_SIZE].astype(BF16),
                  past - wbuf + wi * PAGE_SIZE, PAGE_SIZE, False, True) for wi in range(wbuf // PAGE_SIZE)]
    wn = winn_ref[...]
    win_tiles.append((_pad_rows(wn[:, 0:GD], PAGE_SIZE).astype(BF16),
                      _pad_rows(wn[:, GD:2 * GD], PAGE_SIZE).astype(BF16), past, ts, False, False))
    o_win = attend(win_tiles)

    hi, mid, lo = _split3(gate_ref[...])
    eg = eg_ref[...]
    gexp = _dot(hi, eg) + _dot(mid, eg) + _dot(lo, eg)

    def gate_rows(br):
        return jnp.concatenate([gexp[:, br * D_MODEL + h * GD:br * D_MODEL + (h + 1) * GD]
                                for _ in range(G_B) for h in range(HPG)], axis=0)

    comb = gate_rows(0) * o_cmp + gate_rows(1) * o_slc + gate_rows(2) * o_win
    outs = []
    for h in range(HPG):
        acc = None
        for g in range(G_B):
            piece = jnp.where(lane_g == g, comb[(g * HPG + h) * ts:(g * HPG + h + 1) * ts], 0.0)
            acc = piece if acc is None else acc + piece
        outs.append(acc)
    o_ref[...] = jnp.concatenate(outs, axis=1).astype(o_ref.dtype)


def _nsa_sample_call(page_tab, cache_t, slope_rows, q, gates, cmp_t, kvn, win_t, winn, eg, ovt, b, ts):
    npg = page_tab.shape[1]
    assert npg == N_PAGES_ARG
    past = npg * PAGE_SIZE
    page_specs = [pl.BlockSpec((1, 2 * GD, PAGE_SIZE), functools.partial(lambda bi, pt, p: (pt[bi * N_PAGES_ARG + p], 1, 0), p=p))
                  for p in range(npg)]
    wbuf = win_t.shape[2]
    grid_spec = pltpu.PrefetchScalarGridSpec(
        num_scalar_prefetch=1,
        grid=(b,),
        in_specs=page_specs + [
            pl.BlockSpec(slope_rows.shape, lambda bi, pt: (0, 0), pipeline_mode=pl.Buffered(1)),
            pl.BlockSpec((ts, D_MODEL), lambda bi, pt: (bi, 0)),
            pl.BlockSpec((ts, LANES), lambda bi, pt: (bi, 0)),
            pl.BlockSpec((1, 2, GD, cmp_t.shape[3]), lambda bi, pt: (bi, 0, 0, 0)),
            pl.BlockSpec((ts, 2 * GD), lambda bi, pt: (bi, 1)),
            pl.BlockSpec((1, 2 * GD, wbuf), lambda bi, pt: (bi, 0, 0)),
            pl.BlockSpec((ts, 2 * GD), lambda bi, pt: (bi, 0)),
            pl.BlockSpec(eg.shape, lambda bi, pt: (0, 0), pipeline_mode=pl.Buffered(1)),
            pl.BlockSpec(ovt.shape, lambda bi, pt: (0, 0), pipeline_mode=pl.Buffered(1))],
        out_specs=pl.BlockSpec((ts, D_MODEL), lambda bi, pt: (bi, 0)),
    )
    return pl.pallas_call(
        functools.partial(_nsa_sample_kernel, past=past, ts=ts),
        grid_spec=grid_spec,
        out_shape=jax.ShapeDtypeStruct((b * ts, D_MODEL), F32),
        compiler_params=_params("parallel"),
        name="nsa_sample",
    )(page_tab.reshape(-1), *([cache_t] * npg), slope_rows, q, gates, cmp_t, kvn, win_t, winn, eg, ovt)


def _overlap_t(nsel):
    n = np.arange(LANES)[None, :] * S_CMP
    j = np.arange(LANES)[:, None] * L_SEL
    ov = (n <= j + L_SEL - 1) & (n + L_CMP - 1 >= j) & (np.arange(LANES)[None, :] < LANES - 1) & (np.arange(LANES)[:, None] < nsel)
    return jnp.asarray(ov.astype(np.float32), dtype=BF16)


def _gate_expand():
    e = np.zeros((LANES, 3 * D_MODEL), np.float32)
    for br in range(3):
        for hh in range(H_B):
            e[br * H_B + hh, br * D_MODEL + hh * DH_B:br * D_MODEL + (hh + 1) * DH_B] = 1.0
    return jnp.asarray(e, dtype=BF16)


def _cmp_weights(pe, w1, w2):
    r_n = L_CMP // S_CMP
    pe_t = jnp.tile(pe.reshape(r_n, S_CMP, 1, DH_B), (1, 1, G_B, 1)).reshape(r_n, S_CMP, GD)
    eye = jnp.eye(G_B, dtype=F32)
    w1r = w1.reshape(r_n, S_CMP, DH_B, CMP_HID)
    w1b = jnp.einsum("rsdh,gk->rsgdkh", w1r, eye).reshape(r_n, S_CMP * GD, G_B * CMP_HID)
    w2bt = jnp.einsum("hd,gk->kdgh", w2, eye).reshape(GD, G_B * CMP_HID)
    return pe_t, w1b.astype(BF16), w2bt.astype(BF16)


def _rows_minor(x):
    b, r = x.shape[:2]
    return jnp.moveaxis(x, 1, -1).reshape(b, -1, r)


def _rows_major(x_t, feat_shape):
    b, _, r = x_t.shape
    return jnp.moveaxis(x_t.reshape((b,) + feat_shape + (r,)), -1, 1)


def kernel(x_prompt, x_sample, c_prompt, c_sample, state_ret, state_conv, cache_kv, state_win, page_table, w_ada, b_ada, g_mix, g_ffn, w_ffn_in, conv_w, conv_b, w_ffn_out, w_ret_in, w_ret_out, g_kv, w_ada_kv, b_ada_kv, w_kv, pe_ck, pe_cv, w_ck1, w_ck2, w_cv1, w_cv2, w_nsa_in, w_nsa_out, g_final):
    bp, tp, d = x_prompt.shape
    bs, ts, _ = x_sample.shape
    depth = w_ada.shape[0]
    assert depth == 2 and d == D_MODEL and tp % SLC_TILE == 0 and ts == 8
    npg = page_table.shape[1]
    past = npg * PAGE_SIZE
    f2 = 2 * D_FF
    hd = H_B * DH_B

    w_ada_all = jnp.concatenate([w_ada[0], w_ada[1], w_ada_kv], axis=1).astype(BF16)
    b_ada_all = jnp.concatenate([b_ada[0], b_ada[1], b_ada_kv], axis=0).reshape(1, -1)
    w_ret_in_b = w_ret_in[0].astype(BF16)
    w_ret_out_b = w_ret_out[0].astype(BF16)
    w_ffn_in_b = w_ffn_in.astype(BF16)
    w_ffn_out_b = w_ffn_out.astype(BF16)
    w_kv_b = w_kv.astype(BF16)
    w_kv_t = w_kv_b.T
    w_kv_rows, w_kv_win = w_kv_b[:, :4 * GD], w_kv_b[:, 4 * GD:]
    wn = w_nsa_in[0]
    wq_s = wn[:, :hd] * (DH_B ** -0.5)
    w_q_pad = jnp.pad(wq_s.reshape(d, H_B, DH_B), ((0, 0), (0, 0), (0, A_W - DH_B))).reshape(d, H_B * A_W).astype(BF16)
    w_g_grp = jnp.pad(wn[:, hd:].reshape(d, G_B, HPG, 3).transpose(0, 1, 3, 2).reshape(d, G_B, 3 * HPG),
                      ((0, 0), (0, 0), (0, LANES - 3 * HPG))).reshape(d, G_B * LANES).astype(BF16)
    w_q_s = wq_s.reshape(d, G_B, HPG, DH_B).transpose(0, 2, 1, 3).reshape(d, hd).astype(BF16)
    w_g_s = jnp.pad(wn[:, hd:].reshape(d, G_B, HPG, 3).transpose(0, 3, 2, 1).reshape(d, 3 * H_B),
                    ((0, 0), (0, LANES - 3 * H_B))).astype(BF16)
    w_o_b = w_nsa_out[0].astype(BF16)
    w_o_s = w_nsa_out[0].reshape(G_B, HPG, DH_B, d).transpose(1, 0, 2, 3).reshape(hd, d).astype(BF16)
    pe_k, w1_k, w2_k = _cmp_weights(pe_ck, w_ck1, w_ck2)
    pe_v, w1_v, w2_v = _cmp_weights(pe_cv, w_cv1, w_cv2)
    pe_c = jnp.stack([pe_k, pe_v])
    w1_c = jnp.stack([w1_k, w1_v])
    w2t_c = jnp.stack([w2_k, w2_v])
    slopes = jnp.exp2(-8.0 * jnp.arange(1, H_B + 1, dtype=F32) / H_B)
    log_g = jnp.log1p(-jnp.exp2(-5.0 - jnp.arange(H_A, dtype=F32)))
    lg_tab = jnp.broadcast_to(log_g[:, None, None], (H_A, 1, LANES))

    c_all = jnp.concatenate([c_prompt, c_sample], axis=0)
    mod = _ada_call(c_all, w_ada_all, b_ada_all)

    def mods(lo, hi):
        def cols(k):
            return mod[lo:hi, k * d:(k + 1) * d]
        return [[cols(l * 6 + k) for k in range(6)] for l in range(depth)] + [[cols(12), cols(13)]]

    def layer0(x, mod_l, s_ret, conv_buf, b, t, tm, act_dt):
        sh1, sc1, ga1, sh2, sc2, ga2 = mod_l
        pr = _nmm_call(x, g_mix[0], sh1, sc1, w_ret_in_b, t, tm, 1536, F32, name="ret_in")
        og, s_new = _ret_call(pr, s_ret, lg_tab, b, t, act_dt)
        x = _mm_res_call(og, w_ret_out_b, x, ga1, t, tm, name="ret_out")
        x, conv0 = _ffn_call(x, g_ffn[0], sh2, sc2, ga2, w_ffn_in_b[0], conv_w[0], conv_b[0], conv_buf,
                             w_ffn_out_b[0], b, t, tm, name="ffn0")
        return x, s_new, conv0

    def layer1_tail(x, o, w_o, mod_l, conv_buf, b, t, tm):
        _, _, ga1, sh2, sc2, ga2 = mod_l
        x = _mm_res_call(o, w_o, x, ga1, t, tm, name="nsa_out")
        return _ffn_call(x, g_ffn[1], sh2, sc2, ga2, w_ffn_in_b[1], conv_w[1], conv_b[1], conv_buf,
                         w_ffn_out_b[1], b, t, tm, g_final=g_final, name="ffn1")

    tm = 512
    mp = mods(0, bp)
    x = x_prompt.reshape(bp * tp, d)
    x, ret_p, conv0_p = layer0(x, mp[0], None, jnp.zeros((bp, CONV_W - 1, f2), F32), bp, tp, tm, BF16)
    sh1, sc1 = mp[1][0], mp[1][1]
    kv_t, win_t = _kvt_call(x, g_kv, mp[2][0], mp[2][1], w_kv_t, bp, tp, tm)
    q = _nmm_call(x, g_mix[1], sh1, sc1, w_q_pad, tp, tm, hd, BF16, name="q_proj")
    gates = _nmm_call(x, g_mix[1], sh1, sc1, w_g_grp, tp, tm, G_B * LANES, F32, act="sigmoid", name="gate_proj")
    assert tp == N_PAGES_ARG * PAGE_SIZE
    ident = jnp.zeros((bp, N_PAGES_ARG), jnp.int32)
    cmp_t = _cmp_call(kv_t, lambda bi, p, pt: (bi, 0, p), ident, pe_c, w1_c, w2t_c)
    o = _nsa_prompt_call(q, gates, cmp_t, kv_t, win_t, _nsa_prompt_consts(tp), bp, tp)
    y_p, conv1_p = layer1_tail(x, o, w_o_b, mp[1], jnp.zeros((bp, CONV_W - 1, f2), F32), bp, tp, tm)
    kv_p = _rows_major(kv_t, (4, G_B, DH_B))
    win_p = _rows_major(win_t[:, :, tp - min(WINDOW, tp):], (2, G_B, DH_B))

    tm = 256
    ms = mods(bp, bp + bs)
    x = x_sample.reshape(bs * ts, d)
    x, ret_s, conv0_s = layer0(x, ms[0], state_ret[0].astype(F32), state_conv[0], bs, ts, tm, F32)
    sh1, sc1 = ms[1][0], ms[1][1]
    kv_rows = _nmm_call(x, g_kv, ms[2][0], ms[2][1], w_kv_rows, ts, tm, 4 * GD, F32, name="kv_rows")
    win_rows = _nmm_call(x, g_kv, ms[2][0], ms[2][1], w_kv_win, ts, tm, 2 * GD, F32, name="win_rows")
    q = _nmm_call(x, g_mix[1], sh1, sc1, w_q_s, ts, tm, hd, F32, name="q_proj")
    gates = _nmm_call(x, g_mix[1], sh1, sc1, w_g_s, ts, tm, LANES, F32, act="sigmoid", name="gate_proj")
    cache_t = _rows_minor(cache_kv)
    state_win_t = _rows_minor(state_win)
    cmp_t = _cmp_call(cache_t, lambda bi, p, pt: (pt[bi * N_PAGES_ARG + p], 0, 0), page_table, pe_c, w1_c, w2t_c)
    slope_rows = jnp.repeat(slopes, ts).reshape(H_B * ts, 1)
    o = _nsa_sample_call(page_table, cache_t, slope_rows, q, gates, cmp_t, kv_rows, state_win_t, win_rows,
                         _gate_expand(), _overlap_t(-(-(past + ts) // L_SEL)), bs, ts)
    y_s, conv1_s = layer1_tail(x, o, w_o_s, ms[1], state_conv[1], bs, ts, tm)
    kv_s = kv_rows.reshape(bs, ts, 4, G_B, DH_B)
    win_new_t = jnp.concatenate([state_win_t, jnp.swapaxes(win_rows.reshape(bs, ts, 2 * GD), 1, 2)], axis=2)
    win_s = _rows_major(win_new_t[:, :, win_new_t.shape[2] - min(WINDOW, win_new_t.shape[2]):], (2, G_B, DH_B))

    return (y_p.reshape(bp, tp, d), y_s.reshape(bs, ts, d), ret_p[None], ret_s[None],
            jnp.stack([conv0_p, conv1_p]), jnp.stack([conv0_s, conv1_s]), kv_p, kv_s, win_p, win_s)
```

```python
import functools

import numpy as np
import jax
import jax.numpy as jnp
from jax import lax
from jax.experimental import pallas as pl
from jax.experimental.pallas import tpu as pltpu

F32 = jnp.float32
BF16 = jnp.bfloat16

D_MODEL = 1024
H_A = 4
DK_A = D_MODEL // H_A
DV_A = 2 * DK_A
RET_CHUNK = 128
H_B = 16
DH_B = D_MODEL // H_B
G_B = 4
HPG = H_B // G_B
GD = G_B * DH_B
L_CMP = 32
S_CMP = 16
L_SEL = 64
N_SEL = 16
WINDOW = 512
CMP_HID = 2 * DH_B
Q_BLOCK = 128
FORCE_BONUS = 1e4
D_FF = 2816
CONV_W = 3
EPS = 1e-6
PAGE_SIZE = 128
NEG = -1e30
BIG = 2.0 ** 100
M_INIT = -(2.0 ** 101)

LANES = 128
VMEM_LIMIT = 56 * 1024 * 1024
FF_CHUNK = 256
SLC_TILE = 512
SM_ROWS = 32
N_PAGES_ARG = 16

F_POS = DH_B
F_ONE = DH_B + 6
F_SEL = DH_B + 8
A_W = 2 * DH_B

_NT = (((1,), (1,)), ((), ()))


def _sigmoid(x):
    return 1.0 / (1.0 + jnp.exp(-x))


def _silu(x):
    return x * _sigmoid(x)


def _dot(a, b):
    return jnp.dot(a, b, preferred_element_type=F32)


def _dot_nt(a, b):
    return lax.dot_general(a, b, _NT, preferred_element_type=F32)


def _split3(x):
    hi = x.astype(BF16)
    r = x - hi.astype(F32)
    mid = r.astype(BF16)
    lo = (r - mid.astype(F32)).astype(BF16)
    return hi, mid, lo


def _params(*sem):
    return pltpu.CompilerParams(dimension_semantics=sem, vmem_limit_bytes=VMEM_LIMIT)


def _const_spec(shape):
    nd = len(shape)
    return pl.BlockSpec(shape, lambda *a: (0,) * nd, pipeline_mode=pl.Buffered(1))


def _mod_arg(v, m, tm, rows_per_batch):
    b, d = v.shape
    if rows_per_batch % tm == 0:
        tpb = rows_per_batch // tm
        return v.reshape(b, 1, d), (1, 1, d), (lambda i: (i // tpb, 0, 0))
    assert tm % rows_per_batch == 0
    arr = jnp.repeat(v, rows_per_batch, axis=0).reshape(m // tm, tm, d)
    return arr, (1, tm, d), (lambda i: (i, 0, 0))


def _ada_kernel(c_ref, w_ref, b_ref, o_ref):
    c = c_ref[...]
    o_ref[...] = _dot(_silu(c).astype(BF16), w_ref[...]) + b_ref[...]


def _ada_call(c, w, b, tn=2048):
    m, d = c.shape
    n = w.shape[1]
    assert n % tn == 0
    return pl.pallas_call(
        _ada_kernel,
        grid=(n // tn,),
        in_specs=[pl.BlockSpec((m, d), lambda j: (0, 0)),
                  pl.BlockSpec((d, tn), lambda j: (0, j)),
                  pl.BlockSpec((1, tn), lambda j: (0, j))],
        out_specs=pl.BlockSpec((m, tn), lambda j: (0, j)),
        out_shape=jax.ShapeDtypeStruct((m, n), F32),
        compiler_params=_params("parallel"),
        name="ada",
    )(c, w, b)


def _norm_mod(x, g, sh, sc):
    y = x * lax.rsqrt(jnp.mean(x * x, axis=-1, keepdims=True) + EPS) * g
    return y * (1.0 + sc) + sh


def _nmm_kernel(x_ref, g_ref, sh_ref, sc_ref, w_ref, o_ref, h_scr, *, act):
    @pl.when(pl.program_id(1) == 0)
    def _():
        h_scr[...] = _norm_mod(x_ref[...], g_ref[...], sh_ref[0], sc_ref[0]).astype(BF16)

    acc = _dot(h_scr[...], w_ref[...])
    if act == "sigmoid":
        acc = _sigmoid(acc)
    o_ref[...] = acc.astype(o_ref.dtype)


def _nmm_call(x, g, sh, sc, w, rows_per_batch, tm, tn, out_dtype, act=None, name="nmm"):
    m, d = x.shape
    n = w.shape[1]
    assert m % tm == 0 and n % tn == 0
    sh_a, mshape, mmap = _mod_arg(sh, m, tm, rows_per_batch)
    sc_a, _, _ = _mod_arg(sc, m, tm, rows_per_batch)
    return pl.pallas_call(
        functools.partial(_nmm_kernel, act=act),
        grid=(m // tm, n // tn),
        in_specs=[pl.BlockSpec((tm, d), lambda i, j: (i, 0)),
                  pl.BlockSpec((1, d), lambda i, j: (0, 0)),
                  pl.BlockSpec(mshape, lambda i, j: mmap(i)),
                  pl.BlockSpec(mshape, lambda i, j: mmap(i)),
                  pl.BlockSpec((d, tn), lambda i, j: (0, j))],
        out_specs=pl.BlockSpec((tm, tn), lambda i, j: (i, j)),
        out_shape=jax.ShapeDtypeStruct((m, n), out_dtype),
        scratch_shapes=[pltpu.VMEM((tm, d), BF16)],
        compiler_params=_params("parallel", "arbitrary"),
        name=name,
    )(x, g.reshape(1, d), sh_a, sc_a, w)


def _kvt_kernel(x_ref, g_ref, sh_ref, sc_ref, wt_ref, kv_ref, win_ref):
    h = _norm_mod(x_ref[...], g_ref[...], sh_ref[0], sc_ref[0]).astype(BF16)
    n_kv = kv_ref.shape[1]
    kv_ref[0] = _dot_nt(wt_ref[0:n_kv, :], h)
    win_ref[0] = _dot_nt(wt_ref[n_kv:, :], h)


def _kvt_call(x, g, sh, sc, w_t, b, t, tm):
    m, d = x.shape
    n = w_t.shape[0]
    nt = t // tm
    sh_a, mshape, mmap = _mod_arg(sh, m, tm, t)
    sc_a, _, _ = _mod_arg(sc, m, tm, t)
    return pl.pallas_call(
        _kvt_kernel,
        grid=(b, nt),
        in_specs=[pl.BlockSpec((tm, d), lambda bi, ti: (bi * nt + ti, 0)),
                  pl.BlockSpec((1, d), lambda bi, ti: (0, 0)),
                  pl.BlockSpec(mshape, lambda bi, ti: mmap(bi * nt + ti)),
                  pl.BlockSpec(mshape, lambda bi, ti: mmap(bi * nt + ti)),
                  _const_spec((n, d))],
        out_specs=[pl.BlockSpec((1, 4 * GD, tm), lambda bi, ti: (bi, 0, ti)),
                   pl.BlockSpec((1, 2 * GD, tm), lambda bi, ti: (bi, 0, ti))],
        out_shape=[jax.ShapeDtypeStruct((b, 4 * GD, t), F32),
                   jax.ShapeDtypeStruct((b, 2 * GD, t), F32)],
        compiler_params=_params("parallel", "parallel"),
        name="kv_proj_t",
    )(x, g.reshape(1, d), sh_a, sc_a, w_t)


def _mm_res_kernel(a_ref, w_ref, res_ref, ga_ref, o_ref):
    y = _dot(a_ref[...].astype(BF16), w_ref[...])
    o_ref[...] = res_ref[...] + ga_ref[0] * y


def _mm_res_call(a, w, res, gate, rows_per_batch, tm, name="mm_res"):
    m, k = a.shape
    d = w.shape[1]
    ga_a, mshape, mmap = _mod_arg(gate, m, tm, rows_per_batch)
    return pl.pallas_call(
        _mm_res_kernel,
        grid=(m // tm,),
        in_specs=[pl.BlockSpec((tm, k), lambda i: (i, 0)),
                  _const_spec((k, d)),
                  pl.BlockSpec((tm, d), lambda i: (i, 0)),
                  pl.BlockSpec(mshape, lambda i: mmap(i))],
        out_specs=pl.BlockSpec((tm, d), lambda i: (i, 0)),
        out_shape=jax.ShapeDtypeStruct((m, d), F32),
        compiler_params=_params("parallel"),
        name=name,
    )(a, w, res, ga_a)


def _ret_kernel(*refs, c, nc, has_s0):
    if has_s0:
        lg_ref, q_ref, k_ref, v_ref, g_ref, s0_ref, o_ref, so_ref, s_scr = refs
    else:
        lg_ref, q_ref, k_ref, v_ref, g_ref, o_ref, so_ref, s_scr = refs
        s0_ref = None
    n = pl.program_id(1)

    @pl.when(n == 0)
    def _():
        if has_s0:
            s_scr[...] = s0_ref[0]
        else:
            s_scr[...] = jnp.zeros_like(s_scr)

    cp = max(c, RET_CHUNK)

    def padded(a):
        a = a.astype(F32)
        if cp > c:
            a = jnp.concatenate([a, jnp.zeros((cp - c, a.shape[1]), F32)], axis=0)
        return a

    i = lax.broadcasted_iota(jnp.int32, (cp, 1), 0).astype(F32)
    j = lax.broadcasted_iota(jnp.int32, (1, cp), 1).astype(F32)
    diff = i - j
    for h in range(H_A):
        lg = lg_ref[h][:, 0:1]
        q = padded(q_ref[:, h * DK_A:(h + 1) * DK_A]).astype(BF16)
        kf = padded(k_ref[:, h * DK_A:(h + 1) * DK_A]) * (DK_A ** -0.5)
        v = padded(v_ref[:, h * DV_A:(h + 1) * DV_A]).astype(BF16)
        dmask = jnp.where(diff >= 0, jnp.exp(jnp.maximum(diff, 0.0) * lg), 0.0)
        scores = _dot_nt(q, kf.astype(BF16)) * dmask
        s_old = s_scr[h]
        o = _dot(scores.astype(BF16), v) + _dot(q, s_old.astype(BF16)) * jnp.exp((i + 1.0) * lg)
        w = jnp.exp((c - 1.0 - i) * lg)
        kw_t = (kf * w).T.astype(BF16)
        s_new = jnp.exp(c * lg) * s_old + _dot(kw_t, v)
        s_scr[h] = s_new
        of = o * lax.rsqrt(jnp.mean(o * o, axis=-1, keepdims=True) + EPS)
        gate = g_ref[:, h * DV_A:(h + 1) * DV_A].astype(F32)
        o_ref[:, h * DV_A:(h + 1) * DV_A] = (of[0:c] * _silu(gate)).astype(o_ref.dtype)

    @pl.when(n == nc - 1)
    def _():
        so_ref[0] = s_scr[...]


def _ret_call(pr, s0, lg_tab, b, t, out_dtype):
    c = RET_CHUNK if t % RET_CHUNK == 0 else t
    nc = t // c
    has_s0 = s0 is not None
    nq, nv = H_A * DK_A, H_A * DV_A
    assert nv == 2 * nq
    in_specs = [pl.BlockSpec((H_A, 1, LANES), lambda bi, n: (0, 0, 0)),
                pl.BlockSpec((c, nq), lambda bi, n: (bi * nc + n, 0)),
                pl.BlockSpec((c, nq), lambda bi, n: (bi * nc + n, 1)),
                pl.BlockSpec((c, nv), lambda bi, n: (bi * nc + n, 1)),
                pl.BlockSpec((c, nv), lambda bi, n: (bi * nc + n, 2))]
    args = [lg_tab, pr, pr, pr, pr]
    if has_s0:
        in_specs.append(pl.BlockSpec((1, H_A, DK_A, DV_A), lambda bi, n: (bi, 0, 0, 0)))
        args.append(s0)
    return pl.pallas_call(
        functools.partial(_ret_kernel, c=c, nc=nc, has_s0=has_s0),
        grid=(b, nc),
        in_specs=in_specs,
        out_specs=[pl.BlockSpec((c, nv), lambda bi, n: (bi * nc + n, 0)),
                   pl.BlockSpec((1, H_A, DK_A, DV_A), lambda bi, n: (bi, 0, 0, 0))],
        out_shape=[jax.ShapeDtypeStruct((b * t, nv), out_dtype),
                   jax.ShapeDtypeStruct((b, H_A, DK_A, DV_A), F32)],
        scratch_shapes=[pltpu.VMEM((H_A, DK_A, DV_A), F32)],
        compiler_params=_params("parallel", "arbitrary"),
        name="retention",
    )(*args)


def _ffn_kernel(*refs, t_in, nbt, final):
    if final:
        (x_ref, g_ref, sh_ref, sc_ref, ga_ref, win_ref, cw_ref, cb_ref, cbuf_ref, wout_ref, gf_ref,
         o_ref, nc_ref, h_scr, carry_scr, acc_scr) = refs
    else:
        (x_ref, g_ref, sh_ref, sc_ref, ga_ref, win_ref, cw_ref, cb_ref, cbuf_ref, wout_ref,
         o_ref, nc_ref, h_scr, carry_scr, acc_scr) = refs
        gf_ref = None
    tm = x_ref.shape[0]

    @pl.when(pl.program_id(1) == 0)
    def _():
        carry_scr[...] = cbuf_ref[...]

    x = x_ref[...]
    h_scr[...] = _norm_mod(x, g_ref[...], sh_ref[0], sc_ref[0]).astype(BF16)
    cwd = FF_CHUNK
    tpos = lax.broadcasted_iota(jnp.int32, (nbt, t_in, cwd), 1)
    is0 = tpos == 0
    is1 = tpos == 1

    def conv_part(off):
        u = _dot(h_scr[...], win_ref[:, off:off + cwd]).reshape(nbt, t_in, cwd)
        prev = carry_scr[:, :, off:off + cwd]
        p0 = prev[:, 0:1, :]
        p1 = prev[:, 1:2, :]
        r1 = pltpu.roll(u, 1, axis=1)
        r2 = pltpu.roll(u, 2, axis=1)
        um1 = jnp.where(is0, p1, r1)
        um2 = jnp.where(is0, p0, jnp.where(is1, p1, r2))
        newc = r2[:, 0:2, :]
        carry_scr[:, :, off:off + cwd] = newc
        nc_ref[:, :, off:off + cwd] = newc
        cw = cw_ref[:, off:off + cwd]
        z = cb_ref[:, off:off + cwd] + cw[0:1] * um2
        z = z + cw[1:2] * um1
        z = z + cw[2:3] * u
        return z.reshape(tm, cwd)

    for ch in range(D_FF // cwd):
        za = conv_part(ch * cwd)
        zg = conv_part(D_FF + ch * cwd)
        act = (_silu(zg) * za).astype(BF16)
        y = _dot(act, wout_ref[ch * cwd:(ch + 1) * cwd, :])
        if ch == 0:
            acc_scr[...] = y
        else:
            acc_scr[...] += y
    xn = x + ga_ref[0] * acc_scr[...]
    if final:
        xn = xn * lax.rsqrt(jnp.mean(xn * xn, axis=-1, keepdims=True) + EPS) * gf_ref[...]
    o_ref[...] = xn


def _ffn_call(x, g, sh, sc, ga, w_in, cw, cb, cbuf, w_out, b, t, tm, g_final=None, name="ffn"):
    m, d = x.shape
    f2 = w_in.shape[1]
    if t % tm == 0:
        t_in, nbt, nb_tiles, nt = tm, 1, b, t // tm
    else:
        assert tm % t == 0 and m % tm == 0
        t_in, nbt, nb_tiles, nt = t, tm // t, m // tm, 1
    sh_a, mshape, mmap = _mod_arg(sh, m, tm, t)
    sc_a, _, _ = _mod_arg(sc, m, tm, t)
    ga_a, _, _ = _mod_arg(ga, m, tm, t)
    final = g_final is not None
    in_specs = [pl.BlockSpec((tm, d), lambda bi, ti: (bi * nt + ti, 0)),
                pl.BlockSpec((1, d), lambda bi, ti: (0, 0)),
                pl.BlockSpec(mshape, lambda bi, ti: mmap(bi * nt + ti)),
                pl.BlockSpec(mshape, lambda bi, ti: mmap(bi * nt + ti)),
                pl.BlockSpec(mshape, lambda bi, ti: mmap(bi * nt + ti)),
                _const_spec((d, f2)),
                _const_spec((CONV_W, f2)),
                _const_spec((1, f2)),
                pl.BlockSpec((nbt, CONV_W - 1, f2), lambda bi, ti: (bi, 0, 0)),
                _const_spec((D_FF, d))]
    args = [x, g.reshape(1, d), sh_a, sc_a, ga_a, w_in, cw, cb.reshape(1, f2), cbuf, w_out]
    if final:
        in_specs.append(pl.BlockSpec((1, d), lambda bi, ti: (0, 0)))
        args.append(g_final.reshape(1, d))
    return pl.pallas_call(
        functools.partial(_ffn_kernel, t_in=t_in, nbt=nbt, final=final),
        grid=(nb_tiles, nt),
        in_specs=in_specs,
        out_specs=[pl.BlockSpec((tm, d), lambda bi, ti: (bi * nt + ti, 0)),
                   pl.BlockSpec((nbt, CONV_W - 1, f2), lambda bi, ti: (bi, 0, 0))],
        out_shape=[jax.ShapeDtypeStruct((m, d), F32),
                   jax.ShapeDtypeStruct((b, CONV_W - 1, f2), F32)],
        scratch_shapes=[pltpu.VMEM((tm, d), BF16),
                        pltpu.VMEM((nbt, CONV_W - 1, f2), F32),
                        pltpu.VMEM((tm, d), F32)],
        compiler_params=_params("parallel", "arbitrary"),
        name=name,
    )(*args)


def _cmp_kernel(pt_ref, *refs):
    pages = refs[:N_PAGES_ARG]
    pe_ref, w1_ref, w2t_ref, o_ref = refs[N_PAGES_ARG:]
    npc = PAGE_SIZE // S_CMP
    r_n = L_CMP // S_CMP
    for kind in range(2):
        rows = jnp.concatenate([pg[0, kind * GD:(kind + 1) * GD, :].T for pg in pages], axis=0)
        rows = jnp.swapaxes(rows.reshape(len(pages) * npc, S_CMP, GD), 0, 1)
        pieces = [rows[s] for s in range(S_CMP)]
        z = None
        for r in range(r_n):
            ys = []
            for gp in range(G_B // 2):
                ls = slice(gp * LANES, (gp + 1) * LANES)
                a = jnp.concatenate([(pieces[s][:, ls] + pe_ref[kind, r, s:s + 1, ls]).astype(BF16)
                                     for s in range(S_CMP)], axis=1)
                ys.append(_dot(a, w1_ref[kind, r]))
            y = jnp.concatenate(ys, axis=1)
            if r > 0:
                y = pltpu.roll(y, y.shape[0] - r, axis=0)
            z = y if z is None else z + y
        o_ref[0, kind] = _dot_nt(w2t_ref[kind], _silu(z).astype(BF16)).astype(o_ref.dtype)


def _cmp_call(rows_t, page_map, page_tab, pe, w1, w2t):
    b, npg = page_tab.shape
    assert npg == N_PAGES_ARG
    n_pieces = npg * PAGE_SIZE // S_CMP
    page_specs = [pl.BlockSpec((1, 2 * GD, PAGE_SIZE), functools.partial(lambda bi, pt, p: page_map(bi, p, pt), p=p))
                  for p in range(npg)]
    grid_spec = pltpu.PrefetchScalarGridSpec(
        num_scalar_prefetch=1,
        grid=(b,),
        in_specs=page_specs + [
            pl.BlockSpec(pe.shape, lambda bi, pt: (0, 0, 0, 0), pipeline_mode=pl.Buffered(1)),
            pl.BlockSpec(w1.shape, lambda bi, pt: (0, 0, 0, 0), pipeline_mode=pl.Buffered(1)),
            pl.BlockSpec(w2t.shape, lambda bi, pt: (0, 0, 0), pipeline_mode=pl.Buffered(1))],
        out_specs=pl.BlockSpec((1, 2, GD, n_pieces), lambda bi, pt: (bi, 0, 0, 0)),
    )
    return pl.pallas_call(
        _cmp_kernel,
        grid_spec=grid_spec,
        out_shape=jax.ShapeDtypeStruct((b, 2, GD, n_pieces), BF16),
        compiler_params=_params("parallel"),
        name="compress",
    )(page_tab.reshape(-1), *([rows_t] * npg), pe, w1, w2t)


def _masked_softmax(s, ok):
    sm = jnp.where(ok, s, NEG)
    m = jnp.max(sm, axis=-1, keepdims=True)
    e = jnp.where(ok, jnp.exp(sm - m), 0.0)
    return e / jnp.maximum(jnp.sum(e, axis=-1, keepdims=True), 1e-30)


def _topk_mask_t(imp_t, cur, nsel, ncol):
    rows = imp_t.shape[0]
    jj = lax.broadcasted_iota(jnp.int32, (rows, ncol), 0)
    forced = (jj == 0) | (jj == cur) | (jj == cur - 1)
    score = jnp.where(jj <= cur, imp_t + FORCE_BONUS * forced.astype(F32), NEG)
    rank = jnp.zeros((rows, ncol), F32)
    for i in range(nsel):
        row = score[i:i + 1, :]
        lower = jnp.where(jj > i, 1.0, 0.0)
        rank = rank + jnp.where(row > score, 1.0, jnp.where(row == score, lower, 0.0))
    return jnp.where((rank < float(min(N_SEL, nsel))) & (jj < nsel), 1.0, 0.0)


def _eye_bf16(n):
    return jnp.where(lax.broadcasted_iota(jnp.int32, (n, n), 0) == lax.broadcasted_iota(jnp.int32, (n, n), 1),
                     1.0, 0.0).astype(BF16)


def _pad_rows(a, rows):
    if a.shape[0] == rows:
        return a
    return jnp.concatenate([a, jnp.zeros((rows - a.shape[0],) + a.shape[1:], a.dtype)], axis=0)


def _online_step(sm, v, m_old, l_old, acc_old, v_transposed):
    m_new = jnp.maximum(m_old, jnp.max(sm, axis=-1, keepdims=True))
    alpha = jnp.exp(m_old - m_new)
    e = jnp.where(sm > 0.5 * NEG, jnp.exp(sm - m_new), 0.0)
    l_new = alpha * l_old + jnp.sum(e, axis=-1, keepdims=True)
    pv = _dot_nt(e.astype(BF16), v) if v_transposed else _dot(e.astype(BF16), v)
    return m_new, l_new, alpha * acc_old + pv


def _block_importance_t(ps_sum, ovt):
    hi, mid, lo = _split3(ps_sum)
    return _dot_nt(ovt, hi) + _dot_nt(ovt, mid) + _dot_nt(ovt, lo)


def _nsa_prompt_kernel(q_ref, gate_ref, cmp_ref, kv_ref, win_ref, featc_ref, fslc_ref, fwin_ref, fcmp_ref,
                       ovt_ref, band_ref, o_ref,
                       kb_slc, vt_slc, kb_win, vt_win, kb_cmp, a_scr, s_scr, e_scr, m_scr, acc_scr, gb_scr, *, t_len):
    qb = pl.program_id(1)
    s0 = qb * Q_BLOCK
    nsel = t_len // L_SEL
    n_st = t_len // SLC_TILE
    n_wt = WINDOW // Q_BLOCK
    n_wtiles = kb_win.shape[1]
    rows_g = HPG * Q_BLOCK

    def ones_rows(width):
        r = lax.broadcasted_iota(jnp.int32, (A_W - DH_B, width), 0)
        return jnp.where(r == 0, 1.0, 0.0).astype(BF16)

    @pl.when(qb == 0)
    def _():
        for g in range(G_B):
            for kt in range(n_st):
                cs = slice(kt * SLC_TILE, (kt + 1) * SLC_TILE)
                kb_slc[g, kt, 0:DH_B, :] = kv_ref[0, g * DH_B:(g + 1) * DH_B, cs].astype(BF16)
                kb_slc[g, kt, DH_B:A_W, :] = fslc_ref[:, cs]
            for wt in range(n_wtiles):
                cs = slice((wt - n_wt) * Q_BLOCK, (wt - n_wt + 1) * Q_BLOCK)
                if wt < n_wt:
                    kb_win[g, wt, 0:DH_B, :] = jnp.zeros((DH_B, Q_BLOCK), BF16)
                else:
                    kb_win[g, wt, 0:DH_B, :] = win_ref[0, g * DH_B:(g + 1) * DH_B, cs].astype(BF16)
                kb_win[g, wt, DH_B:A_W, :] = fwin_ref[:, wt * Q_BLOCK:(wt + 1) * Q_BLOCK]
            kb_cmp[g, 0:DH_B, :] = cmp_ref[0, 0, g * DH_B:(g + 1) * DH_B, :]
            kb_cmp[g, DH_B:A_W, :] = fcmp_ref[...]
            for kt in range(n_st):
                cs = slice(kt * SLC_TILE, (kt + 1) * SLC_TILE)
                vt_slc[g, kt, 0:DH_B, :] = kv_ref[0, GD + g * DH_B:GD + (g + 1) * DH_B, cs].astype(BF16)
                vt_slc[g, kt, DH_B:A_W, :] = ones_rows(SLC_TILE)
            for wt in range(n_wtiles):
                cs = slice((wt - n_wt) * Q_BLOCK, (wt - n_wt + 1) * Q_BLOCK)
                if wt < n_wt:
                    vt_win[g, wt, 0:DH_B, :] = jnp.zeros((DH_B, Q_BLOCK), BF16)
                else:
                    vt_win[g, wt, 0:DH_B, :] = win_ref[0, GD + g * DH_B:GD + (g + 1) * DH_B, cs].astype(BF16)
                vt_win[g, wt, DH_B:A_W, :] = ones_rows(Q_BLOCK)

    lane = lax.broadcasted_iota(jnp.int32, (1, A_W), 1)
    ags = [jnp.concatenate([jnp.where(lane < DH_B, q_ref[:, (g * HPG + hl) * A_W:(g * HPG + hl + 1) * A_W],
                                      featc_ref[g * HPG + hl:g * HPG + hl + 1, :].astype(BF16))
                            for hl in range(HPG)], axis=0) for g in range(G_B)]
    for g in range(G_B):
        gt = gate_ref[:, g * LANES:(g + 1) * LANES]
        for c in range(3 * HPG):
            gb_scr[g * 3 * HPG + c] = jnp.broadcast_to(gt[:, c:c + 1], (Q_BLOCK, LANES))

    ovt = ovt_ref[...]
    eye = _eye_bf16(Q_BLOCK)
    t_loc = lax.broadcasted_iota(jnp.int32, (Q_BLOCK, 1), 0)
    tpos = (s0 + t_loc).astype(F32)
    n_f = lax.broadcasted_iota(jnp.int32, (1, Q_BLOCK), 1).astype(F32)
    ok_c = (tpos - (n_f * S_CMP + (L_CMP - 1.0))) >= 0.0
    nrow = ((nsel + 7) // 8) * 8
    groups = range(G_B)

    cur_t = (s0 + lax.broadcasted_iota(jnp.int32, (nrow, Q_BLOCK), 1)) // L_SEL
    o_cmp, ags_sel = [], []
    for g in groups:
        s_c = _dot(ags[g], kb_cmp[g]).reshape(HPG, Q_BLOCK, Q_BLOCK)
        p3 = _masked_softmax(s_c, ok_c[None])
        o_cmp.append(_dot_nt(p3.reshape(rows_g, Q_BLOCK).astype(BF16), cmp_ref[0, 1, g * DH_B:(g + 1) * DH_B, :]))
        imp_t = _block_importance_t((p3[0] + p3[1]) + (p3[2] + p3[3]), ovt)
        sel_t = _topk_mask_t(imp_t[0:nrow], cur_t, nsel, Q_BLOCK)
        nsel_pad = jnp.concatenate([jnp.zeros((F_SEL, Q_BLOCK), F32), 1.0 - sel_t,
                                    jnp.zeros((A_W - F_SEL - nrow, Q_BLOCK), F32)], axis=0).astype(BF16)
        nsl = _dot_nt(eye, nsel_pad).astype(BF16)
        ags_sel.append(ags[g] + jnp.concatenate([nsl] * HPG, axis=0))
    for g in groups:
        a_scr[g] = ags_sel[g]

    def softmax_blocks(g, width, bias_fn):
        m_old_all = m_scr[g]
        m_news, alphas = [], []
        for rb in range(rows_g // SM_ROWS):
            rows = slice(rb * SM_ROWS, (rb + 1) * SM_ROWS)
            s = s_scr[g, rows, 0:width]
            if bias_fn is not None:
                s = bias_fn(s, (rb * SM_ROWS) % Q_BLOCK)
            m_old = m_old_all[rows]
            m_new = jnp.maximum(m_old, jnp.max(s, axis=-1, keepdims=True))
            e = jnp.exp(s - jnp.concatenate([m_new] * (width // LANES), axis=1))
            e_scr[g, rows, 0:width] = e.astype(BF16)
            m_news.append(m_new)
            alphas.append(jnp.exp(m_old - m_new))
        m_scr[g] = jnp.concatenate(m_news, axis=0)
        return jnp.concatenate(alphas, axis=0)

    def reset_stats():
        m_scr[...] = jnp.full_like(m_scr, M_INIT)
        acc_scr[...] = jnp.zeros_like(acc_scr)

    def attend(width, key_fn, val_fn, bias_fn):
        s_scr[0, :, 0:width] = _dot(a_scr[0], key_fn(0))
        for g in groups:
            if g + 1 < G_B:
                s_scr[g + 1, :, 0:width] = _dot(a_scr[g + 1], key_fn(g + 1))
            alpha = softmax_blocks(g, width, bias_fn)
            acc_scr[g] = alpha * acc_scr[g] + _dot_nt(e_scr[g, :, 0:width], val_fn(g))

    def branch_out(g):
        acc = acc_scr[g]
        return acc[:, 0:DH_B] / jnp.maximum(acc[:, DH_B:DH_B + 1], 1e-30)

    def causal_bias(kti):
        def fn(s, t0):
            u = lax.broadcasted_iota(jnp.int32, s.shape, 1)
            t = t0 + lax.broadcasted_iota(jnp.int32, s.shape, 0)
            return jnp.where(u <= (s0 - kti * SLC_TILE) + t, s, -BIG)
        return fn

    def slc_tile(kti, diag):
        attend(SLC_TILE, lambda g: kb_slc[g, kti], lambda g: vt_slc[g, kti], causal_bias(kti) if diag else None)

    reset_stats()
    kd = s0 // SLC_TILE

    def slc_full(kti, c):
        slc_tile(kti, False)
        return c

    lax.fori_loop(0, kd, slc_full, 0)
    slc_tile(kd, True)
    o_slc = [branch_out(g) for g in groups]

    reset_stats()
    w_win = WINDOW + Q_BLOCK
    attend(w_win,
           lambda g: jnp.concatenate([kb_win[g, qb + w] for w in range(n_wt + 1)], axis=1),
           lambda g: jnp.concatenate([vt_win[g, qb + w] for w in range(n_wt + 1)], axis=1),
           lambda s, t0: s + band_ref[t0:t0 + SM_ROWS, :])
    o_win = [branch_out(g) for g in groups]

    pieces = []
    for g in groups:
        for hl in range(HPG):
            r = slice(hl * Q_BLOCK, (hl + 1) * Q_BLOCK)
            gb = [gb_scr[g * 3 * HPG + br * HPG + hl, :, 0:DH_B] for br in range(3)]
            comb = gb[0] * o_cmp[g][r]
            comb = comb + gb[1] * o_slc[g][r]
            comb = comb + gb[2] * o_win[g][r]
            pieces.append(comb)
    o_ref[...] = jnp.concatenate(pieces, axis=1).astype(o_ref.dtype)


def _nsa_prompt_call(q, gates, cmp_t, kv_t, win_t, consts, b, t):
    featc, fslc, fwin, fcmp, ovt, band = consts
    nqb = t // Q_BLOCK
    n_wtiles = (WINDOW + t) // Q_BLOCK
    return pl.pallas_call(
        functools.partial(_nsa_prompt_kernel, t_len=t),
        grid=(b, nqb),
        in_specs=[pl.BlockSpec((Q_BLOCK, H_B * A_W), lambda bi, qi: (bi * nqb + qi, 0)),
                  pl.BlockSpec((Q_BLOCK, G_B * LANES), lambda bi, qi: (bi * nqb + qi, 0)),
                  pl.BlockSpec((1, 2, GD, cmp_t.shape[3]), lambda bi, qi: (bi, 0, 0, 0)),
                  pl.BlockSpec((1, 2 * GD, t), lambda bi, qi: (bi, 1, 0)),
                  pl.BlockSpec((1, 2 * GD, t), lambda bi, qi: (bi, 0, 0)),
                  _const_spec(featc.shape), _const_spec(fslc.shape), _const_spec(fwin.shape),
                  _const_spec(fcmp.shape), _const_spec(ovt.shape), _const_spec(band.shape)],
        out_specs=pl.BlockSpec((Q_BLOCK, D_MODEL), lambda bi, qi: (bi * nqb + qi, 0)),
        out_shape=jax.ShapeDtypeStruct((b * t, D_MODEL), BF16),
        scratch_shapes=[pltpu.VMEM((G_B, t // SLC_TILE, A_W, SLC_TILE), BF16),
                        pltpu.VMEM((G_B, t // SLC_TILE, A_W, SLC_TILE), BF16),
                        pltpu.VMEM((G_B, n_wtiles, A_W, Q_BLOCK), BF16),
                        pltpu.VMEM((G_B, n_wtiles, A_W, Q_BLOCK), BF16),
                        pltpu.VMEM((G_B, A_W, Q_BLOCK), BF16),
                        pltpu.VMEM((G_B, HPG * Q_BLOCK, A_W), BF16),
                        pltpu.VMEM((G_B, HPG * Q_BLOCK, WINDOW + Q_BLOCK), F32),
                        pltpu.VMEM((G_B, HPG * Q_BLOCK, WINDOW + Q_BLOCK), BF16),
                        pltpu.VMEM((G_B, HPG * Q_BLOCK, LANES), F32),
                        pltpu.VMEM((G_B, HPG * Q_BLOCK, A_W), F32),
                        pltpu.VMEM((3 * H_B, Q_BLOCK, LANES), F32)],
        compiler_params=_params("parallel", "arbitrary"),
        name="nsa_prompt",
    )(q, gates, cmp_t, kv_t, win_t, featc, fslc, fwin, fcmp, ovt, band)


def _nsa_prompt_consts(t):
    slopes = jnp.exp2(-8.0 * jnp.arange(1, H_B + 1, dtype=F32) / H_B)
    featc = jnp.zeros((H_B, A_W), F32).at[:, F_ONE].set(1.0)
    for k, term in enumerate(_split3(slopes)):
        featc = featc.at[:, F_POS + 2 * k].set(term.astype(F32)).at[:, F_POS + 2 * k + 1].set(term.astype(F32))

    def pos_rows(pos):
        f = np.zeros((A_W - DH_B, pos.shape[0]), np.float32)
        hi = (pos // 256) * 256
        lo = pos % 256
        for k in range(3):
            f[F_POS - DH_B + 2 * k] = hi
            f[F_POS - DH_B + 2 * k + 1] = lo
        return f

    nsel = t // L_SEL
    spos = np.arange(t)
    fslc = pos_rows(spos)
    for j in range(nsel):
        fslc[F_SEL - DH_B + j] = np.where(spos // L_SEL == j, -BIG, 0.0)
    wcol = np.arange(WINDOW + t)
    fwin = pos_rows(wcol)
    fwin[F_ONE - DH_B] = np.where(wcol < WINDOW, -BIG, 0.0)
    fcmp = pos_rows(np.arange(Q_BLOCK) * S_CMP + (L_CMP - 1))
    u = np.arange(WINDOW + Q_BLOCK)[None, :]
    tl = np.arange(Q_BLOCK)[:, None]
    band = np.where((u >= tl) & (u <= tl + WINDOW), 0.0, -BIG).astype(np.float32)
    return (featc, jnp.asarray(fslc, dtype=BF16), jnp.asarray(fwin, dtype=BF16),
            jnp.asarray(fcmp, dtype=BF16), _overlap_t(nsel), jnp.asarray(band))


def _nsa_sample_kernel(pt_ref, *refs, past, ts):
    pages = refs[:N_PAGES_ARG]
    (slope_ref, slopel_ref, q_ref, gate_ref, cmp_ref, kvn_ref, win_ref, winn_ref, eg_ref, ovt_ref, o_ref) = refs[N_PAGES_ARG:]
    rows = H_B * ts
    nsel = -(-(past + ts) // L_SEL)
    lane_g = lax.broadcasted_iota(jnp.int32, (1, GD), 1) // DH_B
    q = q_ref[...]
    qq = jnp.concatenate([jnp.where(lane_g == g, q[:, h * GD:(h + 1) * GD], 0.0)
                          for g in range(G_B) for h in range(HPG)], axis=0).astype(BF16)
    slope = slope_ref[...]
    t_row = lax.broadcasted_iota(jnp.int32, (rows, 1), 0) % ts
    tpos = (past + t_row).astype(F32)
    kc_t = cmp_ref[0, 0]
    vc_t = cmp_ref[0, 1]
    ncb = kc_t.shape[1]
    n_f = lax.broadcasted_iota(jnp.int32, (1, ncb), 1).astype(F32)
    dist_c = tpos - (n_f * S_CMP + (L_CMP - 1.0))
    p = _masked_softmax(_dot(qq, kc_t) - slope * dist_c, dist_c >= 0.0)
    o_cmp = _dot_nt(p.astype(BF16), vc_t)
    ps = []
    for g in range(G_B):
        blk = [p[(g * HPG + h) * ts:(g * HPG + h + 1) * ts] for h in range(HPG)]
        ps.extend([(blk[0] + blk[1]) + (blk[2] + blk[3])] * HPG)
    psum = jnp.concatenate(ps, axis=0)
    imp_t = _block_importance_t(psum, ovt_ref[...])
    nrow = ((nsel + 1 + 7) // 8) * 8
    cur_t = (past + lax.broadcasted_iota(jnp.int32, (nrow, rows), 1) % ts) // L_SEL
    sel_t = _topk_mask_t(imp_t[0:nrow], cur_t, nsel, rows)

    tpos_l = (past + lax.broadcasted_iota(jnp.int32, (1, rows), 1) % ts).astype(F32)
    slope_l = slopel_ref[...]
    key_i = lax.broadcasted_iota(jnp.int32, (PAGE_SIZE, rows), 0)
    bpp = PAGE_SIZE // L_SEL

    def rows_t(new_ref, lo):
        return _pad_rows(new_ref[:, lo:lo + GD], PAGE_SIZE).T.astype(BF16)

    def attend(kt_all, vt_all, tiles):
        s_all = lax.dot_general(kt_all, qq, (((0,), (1,)), ((), ())), preferred_element_type=F32)
        sms = []
        for i, (k0, n_real, blk) in enumerate(tiles):
            s = s_all[i * PAGE_SIZE:(i + 1) * PAGE_SIZE]
            dist = tpos_l - (k0 + key_i).astype(F32)
            dist_m = dist if n_real == PAGE_SIZE else jnp.where(key_i < n_real, dist, -1.0)
            if blk is not None:
                mask = jnp.concatenate([jnp.broadcast_to(sel_t[blk + a:blk + a + 1], (L_SEL, rows)) for a in range(bpp)],
                                       axis=0)
                ok = jnp.where(dist_m >= 0.0, mask, 0.0) > 0.5
            else:
                ok = jnp.abs(dist_m - 0.5 * WINDOW) <= 0.5 * WINDOW
            sms.append(jnp.where(ok, s - slope_l * dist, NEG))
        m = functools.reduce(jnp.maximum, [jnp.max(x, axis=0, keepdims=True) for x in sms])
        es = [jnp.exp(x - m) for x in sms]
        l = functools.reduce(lambda a, b: a + b, [jnp.sum(e, axis=0, keepdims=True) for e in es])
        o_t = _dot(vt_all, jnp.concatenate(es, axis=0).astype(BF16))
        return (o_t / jnp.maximum(l, 1e-30)).T

    kt_slc = jnp.concatenate([pg[0, 0:GD, :].astype(BF16) for pg in pages] + [rows_t(kvn_ref, 0)], axis=1)
    vt_slc = jnp.concatenate([pg[0, GD:2 * GD, :].astype(BF16) for pg in pages] + [rows_t(kvn_ref, GD)], axis=1)
    o_slc = attend(kt_slc, vt_slc, [(pi * PAGE_SIZE, PAGE_SIZE, pi * bpp) for pi in range(len(pages))]
                   + [(past, ts, past // L_SEL)])

    wbuf = win_ref.shape[2]
    kt_win = jnp.concatenate([win_ref[0, 0:GD, :].astype(BF16), rows_t(winn_ref, 0)], axis=1)
    vt_win = jnp.concatenate([win_ref[0, GD:2 * GD, :].astype(BF16), rows_t(winn_ref, GD)], axis=1)
    o_win = attend(kt_win, vt_win, [(past - wbuf + wi * PAGE_SIZE, PAGE_SIZE, None) for wi in range(wbuf // PAGE_SIZE)]
                   + [(past, ts, None)])

    hi, mid, lo = _split3(gate_ref[...])
    eg = eg_ref[...]
    gexp = _dot(hi, eg) + _dot(mid, eg) + _dot(lo, eg)

    def gate_rows(br):
        return jnp.concatenate([gexp[:, br * D_MODEL + h * GD:br * D_MODEL + (h + 1) * GD]
                                for _ in range(G_B) for h in range(HPG)], axis=0)

    comb = gate_rows(0) * o_cmp + gate_rows(1) * o_slc + gate_rows(2) * o_win
    outs = []
    for h in range(HPG):
        acc = None
        for g in range(G_B):
            piece = jnp.where(lane_g == g, comb[(g * HPG + h) * ts:(g * HPG + h + 1) * ts], 0.0)
            acc = piece if acc is None else acc + piece
        outs.append(acc)
    o_ref[...] = jnp.concatenate(outs, axis=1).astype(o_ref.dtype)


def _nsa_sample_call(page_tab, cache_t, slope_rows, q, gates, cmp_t, kvn, win_t, winn, eg, ovt, b, ts):
    npg = page_tab.shape[1]
    assert npg == N_PAGES_ARG
    past = npg * PAGE_SIZE
    page_specs = [pl.BlockSpec((1, 2 * GD, PAGE_SIZE), functools.partial(lambda bi, pt, p: (pt[bi * N_PAGES_ARG + p], 1, 0), p=p))
                  for p in range(npg)]
    wbuf = win_t.shape[2]
    grid_spec = pltpu.PrefetchScalarGridSpec(
        num_scalar_prefetch=1,
        grid=(b,),
        in_specs=page_specs + [
            pl.BlockSpec(slope_rows.shape, lambda bi, pt: (0, 0), pipeline_mode=pl.Buffered(1)),
            pl.BlockSpec((1, slope_rows.shape[0]), lambda bi, pt: (0, 0), pipeline_mode=pl.Buffered(1)),
            pl.BlockSpec((ts, D_MODEL), lambda bi, pt: (bi, 0)),
            pl.BlockSpec((ts, LANES), lambda bi, pt: (bi, 0)),
            pl.BlockSpec((1, 2, GD, cmp_t.shape[3]), lambda bi, pt: (bi, 0, 0, 0)),
            pl.BlockSpec((ts, 2 * GD), lambda bi, pt: (bi, 1)),
            pl.BlockSpec((1, 2 * GD, wbuf), lambda bi, pt: (bi, 0, 0)),
            pl.BlockSpec((ts, 2 * GD), lambda bi, pt: (bi, 0)),
            pl.BlockSpec(eg.shape, lambda bi, pt: (0, 0), pipeline_mode=pl.Buffered(1)),
            pl.BlockSpec(ovt.shape, lambda bi, pt: (0, 0), pipeline_mode=pl.Buffered(1))],
        out_specs=pl.BlockSpec((ts, D_MODEL), lambda bi, pt: (bi, 0)),
    )
    return pl.pallas_call(
        functools.partial(_nsa_sample_kernel, past=past, ts=ts),
        grid_spec=grid_spec,
        out_shape=jax.ShapeDtypeStruct((b * ts, D_MODEL), F32),
        compiler_params=_params("parallel"),
        name="nsa_sample",
    )(page_tab.reshape(-1), *([cache_t] * npg), slope_rows, slope_rows.reshape(1, -1), q, gates, cmp_t, kvn, win_t, winn, eg, ovt)


def _overlap_t(nsel):
    n = np.arange(LANES)[None, :] * S_CMP
    j = np.arange(LANES)[:, None] * L_SEL
    ov = (n <= j + L_SEL - 1) & (n + L_CMP - 1 >= j) & (np.arange(LANES)[None, :] < LANES - 1) & (np.arange(LANES)[:, None] < nsel)
    return jnp.asarray(ov.astype(np.float32), dtype=BF16)


def _gate_expand():
    e = np.zeros((LANES, 3 * D_MODEL), np.float32)
    for br in range(3):
        for hh in range(H_B):
            e[br * H_B + hh, br * D_MODEL + hh * DH_B:br * D_MODEL + (hh + 1) * DH_B] = 1.0
    return jnp.asarray(e, dtype=BF16)


def _cmp_weights(pe, w1, w2):
    r_n = L_CMP // S_CMP
    pe_t = jnp.tile(pe.reshape(r_n, S_CMP, 1, DH_B), (1, 1, G_B, 1)).reshape(r_n, S_CMP, GD)
    eye = jnp.eye(G_B, dtype=F32)
    w1r = w1.reshape(r_n, S_CMP, DH_B, CMP_HID)
    w1b = jnp.einsum("rsdh,ab->rsadbh", w1r, jnp.eye(2, dtype=F32)).reshape(r_n, S_CMP * 2 * DH_B, 2 * CMP_HID)
    w2bt = jnp.einsum("hd,gk->kdgh", w2, eye).reshape(GD, G_B * CMP_HID)
    return pe_t, w1b.astype(BF16), w2bt.astype(BF16)


def _rows_minor(x):
    b, r = x.shape[:2]
    return jnp.moveaxis(x, 1, -1).reshape(b, -1, r)


def _rows_major(x_t, feat_shape):
    b, _, r = x_t.shape
    return jnp.moveaxis(x_t.reshape((b,) + feat_shape + (r,)), -1, 1)


def kernel(x_prompt, x_sample, c_prompt, c_sample, state_ret, state_conv, cache_kv, state_win, page_table, w_ada, b_ada, g_mix, g_ffn, w_ffn_in, conv_w, conv_b, w_ffn_out, w_ret_in, w_ret_out, g_kv, w_ada_kv, b_ada_kv, w_kv, pe_ck, pe_cv, w_ck1, w_ck2, w_cv1, w_cv2, w_nsa_in, w_nsa_out, g_final):
    bp, tp, d = x_prompt.shape
    bs, ts, _ = x_sample.shape
    depth = w_ada.shape[0]
    assert depth == 2 and d == D_MODEL and tp % SLC_TILE == 0 and ts == 8
    npg = page_table.shape[1]
    past = npg * PAGE_SIZE
    f2 = 2 * D_FF
    hd = H_B * DH_B

    w_ada_all = jnp.concatenate([w_ada[0], w_ada[1], w_ada_kv], axis=1).astype(BF16)
    b_ada_all = jnp.concatenate([b_ada[0], b_ada[1], b_ada_kv], axis=0).reshape(1, -1)
    w_ret_in_b = w_ret_in[0].astype(BF16)
    w_ret_out_b = w_ret_out[0].astype(BF16)
    w_ffn_in_b = w_ffn_in.astype(BF16)
    w_ffn_out_b = w_ffn_out.astype(BF16)
    w_kv_b = w_kv.astype(BF16)
    w_kv_t = w_kv_b.T
    w_kv_rows, w_kv_win = w_kv_b[:, :4 * GD], w_kv_b[:, 4 * GD:]
    wn = w_nsa_in[0]
    wq_s = wn[:, :hd] * (DH_B ** -0.5)
    w_q_pad = jnp.pad(wq_s.reshape(d, H_B, DH_B), ((0, 0), (0, 0), (0, A_W - DH_B))).reshape(d, H_B * A_W).astype(BF16)
    w_g_grp = jnp.pad(wn[:, hd:].reshape(d, G_B, HPG, 3).transpose(0, 1, 3, 2).reshape(d, G_B, 3 * HPG),
                      ((0, 0), (0, 0), (0, LANES - 3 * HPG))).reshape(d, G_B * LANES).astype(BF16)
    w_q_s = wq_s.reshape(d, G_B, HPG, DH_B).transpose(0, 2, 1, 3).reshape(d, hd).astype(BF16)
    w_g_s = jnp.pad(wn[:, hd:].reshape(d, G_B, HPG, 3).transpose(0, 3, 2, 1).reshape(d, 3 * H_B),
                    ((0, 0), (0, LANES - 3 * H_B))).astype(BF16)
    w_o_b = w_nsa_out[0].astype(BF16)
    w_o_s = w_nsa_out[0].reshape(G_B, HPG, DH_B, d).transpose(1, 0, 2, 3).reshape(hd, d).astype(BF16)
    pe_k, w1_k, w2_k = _cmp_weights(pe_ck, w_ck1, w_ck2)
    pe_v, w1_v, w2_v = _cmp_weights(pe_cv, w_cv1, w_cv2)
    pe_c = jnp.stack([pe_k, pe_v])
    w1_c = jnp.stack([w1_k, w1_v])
    w2t_c = jnp.stack([w2_k, w2_v])
    slopes = jnp.exp2(-8.0 * jnp.arange(1, H_B + 1, dtype=F32) / H_B)
    log_g = jnp.log1p(-jnp.exp2(-5.0 - jnp.arange(H_A, dtype=F32)))
    lg_tab = jnp.broadcast_to(log_g[:, None, None], (H_A, 1, LANES))

    c_all = jnp.concatenate([c_prompt, c_sample], axis=0)
    mod = _ada_call(c_all, w_ada_all, b_ada_all)

    def mods(lo, hi):
        def cols(k):
            return mod[lo:hi, k * d:(k + 1) * d]
        return [[cols(l * 6 + k) for k in range(6)] for l in range(depth)] + [[cols(12), cols(13)]]

    def layer0(x, mod_l, s_ret, conv_buf, b, t, tm, act_dt):
        sh1, sc1, ga1, sh2, sc2, ga2 = mod_l
        pr = _nmm_call(x, g_mix[0], sh1, sc1, w_ret_in_b, t, tm, 1536, F32, name="ret_in")
        og, s_new = _ret_call(pr, s_ret, lg_tab, b, t, act_dt)
        x = _mm_res_call(og, w_ret_out_b, x, ga1, t, tm, name="ret_out")
        x, conv0 = _ffn_call(x, g_ffn[0], sh2, sc2, ga2, w_ffn_in_b[0], conv_w[0], conv_b[0], conv_buf,
                             w_ffn_out_b[0], b, t, tm, name="ffn0")
        return x, s_new, conv0

    def layer1_tail(x, o, w_o, mod_l, conv_buf, b, t, tm):
        _, _, ga1, sh2, sc2, ga2 = mod_l
        x = _mm_res_call(o, w_o, x, ga1, t, tm, name="nsa_out")
        return _ffn_call(x, g_ffn[1], sh2, sc2, ga2, w_ffn_in_b[1], conv_w[1], conv_b[1], conv_buf,
                         w_ffn_out_b[1], b, t, tm, g_final=g_final, name="ffn1")

    tm = 512
    mp = mods(0, bp)
    x = x_prompt.reshape(bp * tp, d)
    x, ret_p, conv0_p = layer0(x, mp[0], None, jnp.zeros((bp, CONV_W - 1, f2), F32), bp, tp, tm, BF16)
    sh1, sc1 = mp[1][0], mp[1][1]
    kv_t, win_t = _kvt_call(x, g_kv, mp[2][0], mp[2][1], w_kv_t, bp, tp, tm)
    q = _nmm_call(x, g_mix[1], sh1, sc1, w_q_pad, tp, tm, hd, BF16, name="q_proj")
    gates = _nmm_call(x, g_mix[1], sh1, sc1, w_g_grp, tp, tm, G_B * LANES, F32, act="sigmoid", name="gate_proj")
    assert tp == N_PAGES_ARG * PAGE_SIZE
    ident = jnp.zeros((bp, N_PAGES_ARG), jnp.int32)
    cmp_t = _cmp_call(kv_t, lambda bi, p, pt: (bi, 0, p), ident, pe_c, w1_c, w2t_c)
    o = _nsa_prompt_call(q, gates, cmp_t, kv_t, win_t, _nsa_prompt_consts(tp), bp, tp)
    y_p, conv1_p = layer1_tail(x, o, w_o_b, mp[1], jnp.zeros((bp, CONV_W - 1, f2), F32), bp, tp, tm)
    kv_p = _rows_major(kv_t, (4, G_B, DH_B))
    win_p = _rows_major(win_t[:, :, tp - min(WINDOW, tp):], (2, G_B, DH_B))

    tm = 256
    ms = mods(bp, bp + bs)
    x = x_sample.reshape(bs * ts, d)
    x, ret_s, conv0_s = layer0(x, ms[0], state_ret[0].astype(F32), state_conv[0], bs, ts, tm, F32)
    sh1, sc1 = ms[1][0], ms[1][1]
    kv_rows = _nmm_call(x, g_kv, ms[2][0], ms[2][1], w_kv_rows, ts, tm, 4 * GD, F32, name="kv_rows")
    win_rows = _nmm_call(x, g_kv, ms[2][0], ms[2][1], w_kv_win, ts, tm, 2 * GD, F32, name="win_rows")
    q = _nmm_call(x, g_mix[1], sh1, sc1, w_q_s, ts, tm, hd, F32, name="q_proj")
    gates = _nmm_call(x, g_mix[1], sh1, sc1, w_g_s, ts, tm, LANES, F32, act="sigmoid", name="gate_proj")
    cache_t = _rows_minor(cache_kv)
    state_win_t = _rows_minor(state_win)
    cmp_t = _cmp_call(cache_t, lambda bi, p, pt: (pt[bi * N_PAGES_ARG + p], 0, 0), page_table, pe_c, w1_c, w2t_c)
    slope_rows = jnp.repeat(slopes, ts).reshape(H_B * ts, 1)
    o = _nsa_sample_call(page_table, cache_t, slope_rows, q, gates, cmp_t, kv_rows, state_win_t, win_rows,
                         _gate_expand(), _overlap_t(-(-(past + ts) // L_SEL)), bs, ts)
    y_s, conv1_s = layer1_tail(x, o, w_o_s, ms[1], state_conv[1], bs, ts, tm)
    kv_s = kv_rows.reshape(bs, ts, 4, G_B, DH_B)
    win_new_t = jnp.concatenate([state_win_t, jnp.swapaxes(win_rows.reshape(bs, ts, 2 * GD), 1, 2)], axis=2)
    win_s = _rows_major(win_new_t[:, :, win_new_t.shape[2] - min(WINDOW, win_new_t.shape[2]):], (2, G_B, DH_B))

    return (y_p.reshape(bp, tp, d), y_s.reshape(bs, ts, d), ret_p[None], ret_s[None],
            jnp.stack([conv0_p, conv1_p]), jnp.stack([conv0_s, conv1_s]), kv_p, kv_s, win_p, win_s)
```

```python
import functools

import numpy as np
import jax
import jax.numpy as jnp
from jax import lax
from jax.experimental import pallas as pl
from jax.experimental.pallas import tpu as pltpu

F32 = jnp.float32
BF16 = jnp.bfloat16

D_MODEL = 1024
H_A = 4
DK_A = D_MODEL // H_A
DV_A = 2 * DK_A
RET_CHUNK = 128
H_B = 16
DH_B = D_MODEL // H_B
G_B = 4
HPG = H_B // G_B
GD = G_B * DH_B
L_CMP = 32
S_CMP = 16
L_SEL = 64
N_SEL = 16
WINDOW = 512
CMP_HID = 2 * DH_B
Q_BLOCK = 128
FORCE_BONUS = 1e4
D_FF = 2816
CONV_W = 3
EPS = 1e-6
PAGE_SIZE = 128
NEG = -1e30
BIG = 2.0 ** 100
M_INIT = -(2.0 ** 101)

LANES = 128
SUB = 8
VMEM_LIMIT = 56 * 1024 * 1024
FF_CHUNK = 256
SLC_TILE = 512
SM_ROWS = 32
N_PAGES_ARG = 16

F_POS = DH_B
F_ONE = DH_B + 6
F_SEL = DH_B + 8
A_W = 2 * DH_B

_NT = (((1,), (1,)), ((), ()))


def _sigmoid(x):
    return 1.0 / (1.0 + jnp.exp(-x))


def _silu(x):
    return x * _sigmoid(x)


def _dot(a, b):
    return jnp.dot(a, b, preferred_element_type=F32)


def _dot_nt(a, b):
    return lax.dot_general(a, b, _NT, preferred_element_type=F32)


def _split3(x):
    hi = x.astype(BF16)
    r = x - hi.astype(F32)
    mid = r.astype(BF16)
    lo = (r - mid.astype(F32)).astype(BF16)
    return hi, mid, lo


def _params(*sem):
    return pltpu.CompilerParams(dimension_semantics=sem, vmem_limit_bytes=VMEM_LIMIT)


def _const_spec(shape):
    nd = len(shape)
    return pl.BlockSpec(shape, lambda *a: (0,) * nd, pipeline_mode=pl.Buffered(1))


def _mod_arg(v, m, tm, rows_per_batch):
    b, d = v.shape
    if rows_per_batch % tm == 0:
        tpb = rows_per_batch // tm
        return v.reshape(b, 1, d), (1, 1, d), (lambda i: (i // tpb, 0, 0))
    assert tm % rows_per_batch == 0
    arr = jnp.repeat(v, rows_per_batch, axis=0).reshape(m // tm, tm, d)
    return arr, (1, tm, d), (lambda i: (i, 0, 0))


def _ada_kernel(c_ref, w_ref, b_ref, o_ref):
    c = c_ref[...]
    o_ref[...] = _dot(_silu(c).astype(BF16), w_ref[...]) + b_ref[...]


def _ada_call(c, w, b, tn=2048):
    m, d = c.shape
    n = w.shape[1]
    assert n % tn == 0
    return pl.pallas_call(
        _ada_kernel,
        grid=(n // tn,),
        in_specs=[pl.BlockSpec((m, d), lambda j: (0, 0)),
                  pl.BlockSpec((d, tn), lambda j: (0, j)),
                  pl.BlockSpec((1, tn), lambda j: (0, j))],
        out_specs=pl.BlockSpec((m, tn), lambda j: (0, j)),
        out_shape=jax.ShapeDtypeStruct((m, n), F32),
        compiler_params=_params("parallel"),
        name="ada",
    )(c, w, b)


def _norm_mod(x, g, sh, sc):
    y = x * lax.rsqrt(jnp.mean(x * x, axis=-1, keepdims=True) + EPS) * g
    return y * (1.0 + sc) + sh


def _nmm_kernel(x_ref, g_ref, sh_ref, sc_ref, w_ref, o_ref, h_scr, *, act):
    @pl.when(pl.program_id(1) == 0)
    def _():
        h_scr[...] = _norm_mod(x_ref[...], g_ref[...], sh_ref[0], sc_ref[0]).astype(BF16)

    acc = _dot(h_scr[...], w_ref[...])
    if act == "sigmoid":
        acc = _sigmoid(acc)
    o_ref[...] = acc.astype(o_ref.dtype)


def _nmm_call(x, g, sh, sc, w, rows_per_batch, tm, tn, out_dtype, act=None, name="nmm"):
    m, d = x.shape
    n = w.shape[1]
    assert m % tm == 0 and n % tn == 0
    sh_a, mshape, mmap = _mod_arg(sh, m, tm, rows_per_batch)
    sc_a, _, _ = _mod_arg(sc, m, tm, rows_per_batch)
    return pl.pallas_call(
        functools.partial(_nmm_kernel, act=act),
        grid=(m // tm, n // tn),
        in_specs=[pl.BlockSpec((tm, d), lambda i, j: (i, 0)),
                  pl.BlockSpec((1, d), lambda i, j: (0, 0)),
                  pl.BlockSpec(mshape, lambda i, j: mmap(i)),
                  pl.BlockSpec(mshape, lambda i, j: mmap(i)),
                  pl.BlockSpec((d, tn), lambda i, j: (0, j))],
        out_specs=pl.BlockSpec((tm, tn), lambda i, j: (i, j)),
        out_shape=jax.ShapeDtypeStruct((m, n), out_dtype),
        scratch_shapes=[pltpu.VMEM((tm, d), BF16)],
        compiler_params=_params("parallel", "arbitrary"),
        name=name,
    )(x, g.reshape(1, d), sh_a, sc_a, w)


def _kvt_kernel(x_ref, g_ref, sh_ref, sc_ref, wt_ref, kv_ref, win_ref):
    h = _norm_mod(x_ref[...], g_ref[...], sh_ref[0], sc_ref[0]).astype(BF16)
    n_kv = kv_ref.shape[1]
    kv_ref[0] = _dot_nt(wt_ref[0:n_kv, :], h)
    win_ref[0] = _dot_nt(wt_ref[n_kv:, :], h)


def _kvt_call(x, g, sh, sc, w_t, b, t, tm):
    m, d = x.shape
    n = w_t.shape[0]
    nt = t // tm
    sh_a, mshape, mmap = _mod_arg(sh, m, tm, t)
    sc_a, _, _ = _mod_arg(sc, m, tm, t)
    return pl.pallas_call(
        _kvt_kernel,
        grid=(b, nt),
        in_specs=[pl.BlockSpec((tm, d), lambda bi, ti: (bi * nt + ti, 0)),
                  pl.BlockSpec((1, d), lambda bi, ti: (0, 0)),
                  pl.BlockSpec(mshape, lambda bi, ti: mmap(bi * nt + ti)),
                  pl.BlockSpec(mshape, lambda bi, ti: mmap(bi * nt + ti)),
                  _const_spec((n, d))],
        out_specs=[pl.BlockSpec((1, 4 * GD, tm), lambda bi, ti: (bi, 0, ti)),
                   pl.BlockSpec((1, 2 * GD, tm), lambda bi, ti: (bi, 0, ti))],
        out_shape=[jax.ShapeDtypeStruct((b, 4 * GD, t), F32),
                   jax.ShapeDtypeStruct((b, 2 * GD, t), F32)],
        compiler_params=_params("parallel", "parallel"),
        name="kv_proj_t",
    )(x, g.reshape(1, d), sh_a, sc_a, w_t)


def _mm_res_kernel(a_ref, w_ref, res_ref, ga_ref, o_ref):
    y = _dot(a_ref[...].astype(BF16), w_ref[...])
    o_ref[...] = res_ref[...] + ga_ref[0] * y


def _mm_res_call(a, w, res, gate, rows_per_batch, tm, name="mm_res"):
    m, k = a.shape
    d = w.shape[1]
    ga_a, mshape, mmap = _mod_arg(gate, m, tm, rows_per_batch)
    return pl.pallas_call(
        _mm_res_kernel,
        grid=(m // tm,),
        in_specs=[pl.BlockSpec((tm, k), lambda i: (i, 0)),
                  _const_spec((k, d)),
                  pl.BlockSpec((tm, d), lambda i: (i, 0)),
                  pl.BlockSpec(mshape, lambda i: mmap(i))],
        out_specs=pl.BlockSpec((tm, d), lambda i: (i, 0)),
        out_shape=jax.ShapeDtypeStruct((m, d), F32),
        compiler_params=_params("parallel"),
        name=name,
    )(a, w, res, ga_a)


def _ret_kernel(*refs, c, nc, has_s0):
    if has_s0:
        lg_ref, q_ref, k_ref, v_ref, g_ref, s0_ref, o_ref, so_ref, s_scr = refs
    else:
        lg_ref, q_ref, k_ref, v_ref, g_ref, o_ref, so_ref, s_scr = refs
        s0_ref = None
    n = pl.program_id(1)

    @pl.when(n == 0)
    def _():
        if has_s0:
            s_scr[...] = s0_ref[0]
        else:
            s_scr[...] = jnp.zeros_like(s_scr)

    cp = max(c, RET_CHUNK)

    def padded(a):
        a = a.astype(F32)
        if cp > c:
            a = jnp.concatenate([a, jnp.zeros((cp - c, a.shape[1]), F32)], axis=0)
        return a

    i = lax.broadcasted_iota(jnp.int32, (cp, 1), 0).astype(F32)
    j = lax.broadcasted_iota(jnp.int32, (1, cp), 1).astype(F32)
    diff = i - j
    for h in range(H_A):
        lg = lg_ref[h][:, 0:1]
        q = padded(q_ref[:, h * DK_A:(h + 1) * DK_A]).astype(BF16)
        kf = padded(k_ref[:, h * DK_A:(h + 1) * DK_A]) * (DK_A ** -0.5)
        v = padded(v_ref[:, h * DV_A:(h + 1) * DV_A]).astype(BF16)
        dmask = jnp.where(diff >= 0, jnp.exp(jnp.maximum(diff, 0.0) * lg), 0.0)
        scores = _dot_nt(q, kf.astype(BF16)) * dmask
        s_old = s_scr[h]
        o = _dot(scores.astype(BF16), v) + _dot(q, s_old.astype(BF16)) * jnp.exp((i + 1.0) * lg)
        w = jnp.exp((c - 1.0 - i) * lg)
        kw_t = (kf * w).T.astype(BF16)
        s_new = jnp.exp(c * lg) * s_old + _dot(kw_t, v)
        s_scr[h] = s_new
        of = o * lax.rsqrt(jnp.mean(o * o, axis=-1, keepdims=True) + EPS)
        gate = g_ref[:, h * DV_A:(h + 1) * DV_A].astype(F32)
        o_ref[:, h * DV_A:(h + 1) * DV_A] = (of[0:c] * _silu(gate)).astype(o_ref.dtype)

    @pl.when(n == nc - 1)
    def _():
        so_ref[0] = s_scr[...]


def _ret_call(pr, s0, lg_tab, b, t, out_dtype):
    c = RET_CHUNK if t % RET_CHUNK == 0 else t
    nc = t // c
    has_s0 = s0 is not None
    nq, nv = H_A * DK_A, H_A * DV_A
    assert nv == 2 * nq
    in_specs = [pl.BlockSpec((H_A, 1, LANES), lambda bi, n: (0, 0, 0)),
                pl.BlockSpec((c, nq), lambda bi, n: (bi * nc + n, 0)),
                pl.BlockSpec((c, nq), lambda bi, n: (bi * nc + n, 1)),
                pl.BlockSpec((c, nv), lambda bi, n: (bi * nc + n, 1)),
                pl.BlockSpec((c, nv), lambda bi, n: (bi * nc + n, 2))]
    args = [lg_tab, pr, pr, pr, pr]
    if has_s0:
        in_specs.append(pl.BlockSpec((1, H_A, DK_A, DV_A), lambda bi, n: (bi, 0, 0, 0)))
        args.append(s0)
    return pl.pallas_call(
        functools.partial(_ret_kernel, c=c, nc=nc, has_s0=has_s0),
        grid=(b, nc),
        in_specs=in_specs,
        out_specs=[pl.BlockSpec((c, nv), lambda bi, n: (bi * nc + n, 0)),
                   pl.BlockSpec((1, H_A, DK_A, DV_A), lambda bi, n: (bi, 0, 0, 0))],
        out_shape=[jax.ShapeDtypeStruct((b * t, nv), out_dtype),
                   jax.ShapeDtypeStruct((b, H_A, DK_A, DV_A), F32)],
        scratch_shapes=[pltpu.VMEM((H_A, DK_A, DV_A), F32)],
        compiler_params=_params("parallel", "arbitrary"),
        name="retention",
    )(*args)


def _ffn_kernel(*refs, t_in, nbt, final):
    if final:
        (x_ref, g_ref, sh_ref, sc_ref, ga_ref, win_ref, cw_ref, cb_ref, cbuf_ref, wout_ref, gf_ref,
         o_ref, nc_ref, h_scr, carry_scr, act_scr, shift_scr) = refs
    else:
        (x_ref, g_ref, sh_ref, sc_ref, ga_ref, win_ref, cw_ref, cb_ref, cbuf_ref, wout_ref,
         o_ref, nc_ref, h_scr, carry_scr, act_scr, shift_scr) = refs
        gf_ref = None
    tm = x_ref.shape[0]

    @pl.when(pl.program_id(1) == 0)
    def _():
        carry_scr[...] = cbuf_ref[...]

    x = x_ref[...]
    h_scr[...] = _norm_mod(x, g_ref[...], sh_ref[0], sc_ref[0]).astype(BF16)
    cwd = FF_CHUNK
    tpos = lax.broadcasted_iota(jnp.int32, (nbt, t_in, cwd), 1)
    is0 = tpos == 0
    is1 = tpos == 1

    def conv_rolled(off):
        u = _dot(h_scr[...], win_ref[:, off:off + cwd]).reshape(nbt, t_in, cwd)
        prev = carry_scr[:, :, off:off + cwd]
        p0 = prev[:, 0:1, :]
        p1 = prev[:, 1:2, :]
        r1 = pltpu.roll(u, 1, axis=1)
        r2 = pltpu.roll(u, 2, axis=1)
        um1 = jnp.where(is0, p1, r1)
        um2 = jnp.where(is0, p0, jnp.where(is1, p1, r2))
        newc = r2[:, 0:2, :]
        carry_scr[:, :, off:off + cwd] = newc
        nc_ref[:, :, off:off + cwd] = newc
        cw = cw_ref[:, off:off + cwd]
        z = cb_ref[:, off:off + cwd] + cw[0:1] * um2
        z = z + cw[1:2] * um1
        z = z + cw[2:3] * u
        return z.reshape(tm, cwd)

    def conv_shifted(off, slot):
        buf = shift_scr.at[slot]
        buf[SUB:SUB + tm, :] = _dot(h_scr[...], win_ref[:, off:off + cwd])
        buf[SUB - 2:SUB, :] = carry_scr[0, :, off:off + cwd]
        newc = buf[SUB + tm - 2:SUB + tm, :]
        carry_scr[0, :, off:off + cwd] = newc
        nc_ref[0, :, off:off + cwd] = newc
        cw = cw_ref[:, off:off + cwd]
        z = cb_ref[:, off:off + cwd] + cw[0:1] * buf[SUB - 2:SUB - 2 + tm, :]
        z = z + cw[1:2] * buf[SUB - 1:SUB - 1 + tm, :]
        z = z + cw[2:3] * buf[SUB:SUB + tm, :]
        return z

    n_slots = shift_scr.shape[0]
    for ch in range(D_FF // cwd):
        if nbt == 1:
            za = conv_shifted(ch * cwd, (2 * ch) % n_slots)
            zg = conv_shifted(D_FF + ch * cwd, (2 * ch + 1) % n_slots)
        else:
            za = conv_rolled(ch * cwd)
            zg = conv_rolled(D_FF + ch * cwd)
        act_scr[:, ch * cwd:(ch + 1) * cwd] = (_silu(zg) * za).astype(BF16)
    xn = x + ga_ref[0] * _dot(act_scr[...], wout_ref[...])
    if final:
        xn = xn * lax.rsqrt(jnp.mean(xn * xn, axis=-1, keepdims=True) + EPS) * gf_ref[...]
    o_ref[...] = xn


def _ffn_call(x, g, sh, sc, ga, w_in, cw, cb, cbuf, w_out, b, t, tm, g_final=None, name="ffn"):
    m, d = x.shape
    f2 = w_in.shape[1]
    if t % tm == 0:
        t_in, nbt, nb_tiles, nt = tm, 1, b, t // tm
    else:
        assert tm % t == 0 and m % tm == 0
        t_in, nbt, nb_tiles, nt = t, tm // t, m // tm, 1
    sh_a, mshape, mmap = _mod_arg(sh, m, tm, t)
    sc_a, _, _ = _mod_arg(sc, m, tm, t)
    ga_a, _, _ = _mod_arg(ga, m, tm, t)
    final = g_final is not None
    in_specs = [pl.BlockSpec((tm, d), lambda bi, ti: (bi * nt + ti, 0)),
                pl.BlockSpec((1, d), lambda bi, ti: (0, 0)),
                pl.BlockSpec(mshape, lambda bi, ti: mmap(bi * nt + ti)),
                pl.BlockSpec(mshape, lambda bi, ti: mmap(bi * nt + ti)),
                pl.BlockSpec(mshape, lambda bi, ti: mmap(bi * nt + ti)),
                _const_spec((d, f2)),
                _const_spec((CONV_W, f2)),
                _const_spec((1, f2)),
                pl.BlockSpec((nbt, CONV_W - 1, f2), lambda bi, ti: (bi, 0, 0)),
                _const_spec((D_FF, d))]
    args = [x, g.reshape(1, d), sh_a, sc_a, ga_a, w_in, cw, cb.reshape(1, f2), cbuf, w_out]
    if final:
        in_specs.append(pl.BlockSpec((1, d), lambda bi, ti: (0, 0)))
        args.append(g_final.reshape(1, d))
    return pl.pallas_call(
        functools.partial(_ffn_kernel, t_in=t_in, nbt=nbt, final=final),
        grid=(nb_tiles, nt),
        in_specs=in_specs,
        out_specs=[pl.BlockSpec((tm, d), lambda bi, ti: (bi * nt + ti, 0)),
                   pl.BlockSpec((nbt, CONV_W - 1, f2), lambda bi, ti: (bi, 0, 0))],
        out_shape=[jax.ShapeDtypeStruct((m, d), F32),
                   jax.ShapeDtypeStruct((b, CONV_W - 1, f2), F32)],
        scratch_shapes=[pltpu.VMEM((tm, d), BF16),
                        pltpu.VMEM((nbt, CONV_W - 1, f2), F32),
                        pltpu.VMEM((tm, D_FF), BF16),
                        pltpu.VMEM((4, SUB + tm, FF_CHUNK), F32)],
        compiler_params=_params("parallel", "arbitrary"),
        name=name,
    )(*args)


def _cmp_kernel(pt_ref, *refs):
    pages = refs[:N_PAGES_ARG]
    pe_ref, w1_ref, w2t_ref, o_ref = refs[N_PAGES_ARG:]
    npc = PAGE_SIZE // S_CMP
    r_n = L_CMP // S_CMP
    for kind in range(2):
        rows = jnp.concatenate([pg[0, kind * GD:(kind + 1) * GD, :].T for pg in pages], axis=0)
        rows = jnp.swapaxes(rows.reshape(len(pages) * npc, S_CMP, GD), 0, 1)
        pieces = [rows[s] for s in range(S_CMP)]
        z = None
        for r in range(r_n):
            ys = []
            for gp in range(G_B // 2):
                ls = slice(gp * LANES, (gp + 1) * LANES)
                a = jnp.concatenate([(pieces[s][:, ls] + pe_ref[kind, r, s:s + 1, ls]).astype(BF16)
                                     for s in range(S_CMP)], axis=1)
                ys.append(_dot(a, w1_ref[kind, r]))
            y = jnp.concatenate(ys, axis=1)
            if r > 0:
                y = pltpu.roll(y, y.shape[0] - r, axis=0)
            z = y if z is None else z + y
        o_ref[0, kind] = _dot_nt(w2t_ref[kind], _silu(z).astype(BF16)).astype(o_ref.dtype)


def _cmp_call(rows_t, page_map, page_tab, pe, w1, w2t):
    b, npg = page_tab.shape
    assert npg == N_PAGES_ARG
    n_pieces = npg * PAGE_SIZE // S_CMP
    page_specs = [pl.BlockSpec((1, 2 * GD, PAGE_SIZE), functools.partial(lambda bi, pt, p: page_map(bi, p, pt), p=p))
                  for p in range(npg)]
    grid_spec = pltpu.PrefetchScalarGridSpec(
        num_scalar_prefetch=1,
        grid=(b,),
        in_specs=page_specs + [
            pl.BlockSpec(pe.shape, lambda bi, pt: (0, 0, 0, 0), pipeline_mode=pl.Buffered(1)),
            pl.BlockSpec(w1.shape, lambda bi, pt: (0, 0, 0, 0), pipeline_mode=pl.Buffered(1)),
            pl.BlockSpec(w2t.shape, lambda bi, pt: (0, 0, 0), pipeline_mode=pl.Buffered(1))],
        out_specs=pl.BlockSpec((1, 2, GD, n_pieces), lambda bi, pt: (bi, 0, 0, 0)),
    )
    return pl.pallas_call(
        _cmp_kernel,
        grid_spec=grid_spec,
        out_shape=jax.ShapeDtypeStruct((b, 2, GD, n_pieces), BF16),
        compiler_params=_params("parallel"),
        name="compress",
    )(page_tab.reshape(-1), *([rows_t] * npg), pe, w1, w2t)


def _masked_softmax(s, ok):
    sm = jnp.where(ok, s, NEG)
    m = jnp.max(sm, axis=-1, keepdims=True)
    e = jnp.where(ok, jnp.exp(sm - m), 0.0)
    return e / jnp.maximum(jnp.sum(e, axis=-1, keepdims=True), 1e-30)


def _topk_mask_t(imp_t, cur, nsel, ncol):
    rows = imp_t.shape[0]
    jj = lax.broadcasted_iota(jnp.int32, (rows, ncol), 0)
    forced = (jj == 0) | (jj == cur) | (jj == cur - 1)
    score = jnp.where(jj <= cur, imp_t + FORCE_BONUS * forced.astype(F32), NEG)
    rank = jnp.zeros((rows, ncol), F32)
    for i in range(nsel):
        row = score[i:i + 1, :]
        lower = jnp.where(jj > i, 1.0, 0.0)
        rank = rank + jnp.where(row > score, 1.0, jnp.where(row == score, lower, 0.0))
    return jnp.where((rank < float(min(N_SEL, nsel))) & (jj < nsel), 1.0, 0.0)


def _eye_bf16(n):
    return jnp.where(lax.broadcasted_iota(jnp.int32, (n, n), 0) == lax.broadcasted_iota(jnp.int32, (n, n), 1),
                     1.0, 0.0).astype(BF16)


def _pad_rows(a, rows):
    if a.shape[0] == rows:
        return a
    return jnp.concatenate([a, jnp.zeros((rows - a.shape[0],) + a.shape[1:], a.dtype)], axis=0)


def _online_step(sm, v, m_old, l_old, acc_old, v_transposed):
    m_new = jnp.maximum(m_old, jnp.max(sm, axis=-1, keepdims=True))
    alpha = jnp.exp(m_old - m_new)
    e = jnp.where(sm > 0.5 * NEG, jnp.exp(sm - m_new), 0.0)
    l_new = alpha * l_old + jnp.sum(e, axis=-1, keepdims=True)
    pv = _dot_nt(e.astype(BF16), v) if v_transposed else _dot(e.astype(BF16), v)
    return m_new, l_new, alpha * acc_old + pv


def _block_importance_t(ps_sum, ovt):
    hi, mid, lo = _split3(ps_sum)
    return _dot_nt(ovt, hi) + _dot_nt(ovt, mid) + _dot_nt(ovt, lo)


def _nsa_prompt_kernel(q_ref, gate_ref, cmp_ref, kv_ref, win_ref, featc_ref, fslc_ref, fwin_ref, fcmp_ref,
                       ovt_ref, band_ref, o_ref,
                       kb_slc, vt_slc, kb_win, vt_win, kb_cmp, a_scr, s_scr, e_scr, m_scr, acc_scr, gb_scr, *, t_len):
    qb = pl.program_id(1)
    s0 = qb * Q_BLOCK
    nsel = t_len // L_SEL
    n_st = t_len // SLC_TILE
    n_wt = WINDOW // Q_BLOCK
    n_wtiles = kb_win.shape[1]
    rows_g = HPG * Q_BLOCK

    def ones_rows(width):
        r = lax.broadcasted_iota(jnp.int32, (A_W - DH_B, width), 0)
        return jnp.where(r == 0, 1.0, 0.0).astype(BF16)

    @pl.when(qb == 0)
    def _():
        for g in range(G_B):
            for kt in range(n_st):
                cs = slice(kt * SLC_TILE, (kt + 1) * SLC_TILE)
                kb_slc[g, kt, 0:DH_B, :] = kv_ref[0, g * DH_B:(g + 1) * DH_B, cs].astype(BF16)
                kb_slc[g, kt, DH_B:A_W, :] = fslc_ref[:, cs]
            for wt in range(n_wtiles):
                cs = slice((wt - n_wt) * Q_BLOCK, (wt - n_wt + 1) * Q_BLOCK)
                if wt < n_wt:
                    kb_win[g, wt, 0:DH_B, :] = jnp.zeros((DH_B, Q_BLOCK), BF16)
                else:
                    kb_win[g, wt, 0:DH_B, :] = win_ref[0, g * DH_B:(g + 1) * DH_B, cs].astype(BF16)
                kb_win[g, wt, DH_B:A_W, :] = fwin_ref[:, wt * Q_BLOCK:(wt + 1) * Q_BLOCK]
            kb_cmp[g, 0:DH_B, :] = cmp_ref[0, 0, g * DH_B:(g + 1) * DH_B, :]
            kb_cmp[g, DH_B:A_W, :] = fcmp_ref[...]
            for kt in range(n_st):
                cs = slice(kt * SLC_TILE, (kt + 1) * SLC_TILE)
                vt_slc[g, kt, 0:DH_B, :] = kv_ref[0, GD + g * DH_B:GD + (g + 1) * DH_B, cs].astype(BF16)
                vt_slc[g, kt, DH_B:A_W, :] = ones_rows(SLC_TILE)
            for wt in range(n_wtiles):
                cs = slice((wt - n_wt) * Q_BLOCK, (wt - n_wt + 1) * Q_BLOCK)
                if wt < n_wt:
                    vt_win[g, wt, 0:DH_B, :] = jnp.zeros((DH_B, Q_BLOCK), BF16)
                else:
                    vt_win[g, wt, 0:DH_B, :] = win_ref[0, GD + g * DH_B:GD + (g + 1) * DH_B, cs].astype(BF16)
                vt_win[g, wt, DH_B:A_W, :] = ones_rows(Q_BLOCK)

    lane = lax.broadcasted_iota(jnp.int32, (1, A_W), 1)
    ags = [jnp.concatenate([jnp.where(lane < DH_B, q_ref[:, (g * HPG + hl) * A_W:(g * HPG + hl + 1) * A_W],
                                      featc_ref[g * HPG + hl:g * HPG + hl + 1, :].astype(BF16))
                            for hl in range(HPG)], axis=0) for g in range(G_B)]
    for g in range(G_B):
        gt = gate_ref[:, g * LANES:(g + 1) * LANES]
        for c in range(3 * HPG):
            gb_scr[g * 3 * HPG + c] = jnp.broadcast_to(gt[:, c:c + 1], (Q_BLOCK, LANES))

    ovt = ovt_ref[...]
    eye = _eye_bf16(Q_BLOCK)
    t_loc = lax.broadcasted_iota(jnp.int32, (Q_BLOCK, 1), 0)
    tpos = (s0 + t_loc).astype(F32)
    n_f = lax.broadcasted_iota(jnp.int32, (1, Q_BLOCK), 1).astype(F32)
    ok_c = (tpos - (n_f * S_CMP + (L_CMP - 1.0))) >= 0.0
    nrow = ((nsel + 7) // 8) * 8
    groups = range(G_B)

    cur_t = (s0 + lax.broadcasted_iota(jnp.int32, (nrow, Q_BLOCK), 1)) // L_SEL
    o_cmp, ags_sel = [], []
    for g in groups:
        s_c = _dot(ags[g], kb_cmp[g]).reshape(HPG, Q_BLOCK, Q_BLOCK)
        p3 = _masked_softmax(s_c, ok_c[None])
        o_cmp.append(_dot_nt(p3.reshape(rows_g, Q_BLOCK).astype(BF16), cmp_ref[0, 1, g * DH_B:(g + 1) * DH_B, :]))
        imp_t = _block_importance_t((p3[0] + p3[1]) + (p3[2] + p3[3]), ovt)
        sel_t = _topk_mask_t(imp_t[0:nrow], cur_t, nsel, Q_BLOCK)
        nsel_pad = jnp.concatenate([jnp.zeros((F_SEL, Q_BLOCK), F32), 1.0 - sel_t,
                                    jnp.zeros((A_W - F_SEL - nrow, Q_BLOCK), F32)], axis=0).astype(BF16)
        nsl = _dot_nt(eye, nsel_pad).astype(BF16)
        ags_sel.append(ags[g] + jnp.concatenate([nsl] * HPG, axis=0))
    for g in groups:
        a_scr[g] = ags_sel[g]

    def softmax_blocks(g, width, bias_fn):
        m_old_all = m_scr[g]
        m_news, alphas = [], []
        for rb in range(rows_g // SM_ROWS):
            rows = slice(rb * SM_ROWS, (rb + 1) * SM_ROWS)
            s = s_scr[g, rows, 0:width]
            if bias_fn is not None:
                s = bias_fn(s, (rb * SM_ROWS) % Q_BLOCK)
            m_old = m_old_all[rows]
            m_new = jnp.maximum(m_old, jnp.max(s, axis=-1, keepdims=True))
            e = jnp.exp(s - jnp.concatenate([m_new] * (width // LANES), axis=1))
            e_scr[g, rows, 0:width] = e.astype(BF16)
            m_news.append(m_new)
            alphas.append(jnp.exp(m_old - m_new))
        m_scr[g] = jnp.concatenate(m_news, axis=0)
        return jnp.concatenate(alphas, axis=0)

    def reset_stats():
        m_scr[...] = jnp.full_like(m_scr, M_INIT)
        acc_scr[...] = jnp.zeros_like(acc_scr)

    def attend(width, key_fn, val_fn, bias_fn):
        s_scr[0, :, 0:width] = _dot(a_scr[0], key_fn(0))
        for g in groups:
            if g + 1 < G_B:
                s_scr[g + 1, :, 0:width] = _dot(a_scr[g + 1], key_fn(g + 1))
            alpha = softmax_blocks(g, width, bias_fn)
            acc_scr[g] = alpha * acc_scr[g] + _dot_nt(e_scr[g, :, 0:width], val_fn(g))

    def branch_out(g):
        acc = acc_scr[g]
        return acc[:, 0:DH_B] / jnp.maximum(acc[:, DH_B:DH_B + 1], 1e-30)

    def causal_bias(kti):
        def fn(s, t0):
            u = lax.broadcasted_iota(jnp.int32, s.shape, 1)
            t = t0 + lax.broadcasted_iota(jnp.int32, s.shape, 0)
            return jnp.where(u <= (s0 - kti * SLC_TILE) + t, s, -BIG)
        return fn

    def slc_tile(kti, diag):
        attend(SLC_TILE, lambda g: kb_slc[g, kti], lambda g: vt_slc[g, kti], causal_bias(kti) if diag else None)

    reset_stats()
    kd = s0 // SLC_TILE

    def slc_full(kti, c):
        slc_tile(kti, False)
        return c

    lax.fori_loop(0, kd, slc_full, 0)
    slc_tile(kd, True)
    o_slc = [branch_out(g) for g in groups]

    reset_stats()
    w_win = WINDOW + Q_BLOCK
    attend(w_win,
           lambda g: jnp.concatenate([kb_win[g, qb + w] for w in range(n_wt + 1)], axis=1),
           lambda g: jnp.concatenate([vt_win[g, qb + w] for w in range(n_wt + 1)], axis=1),
           lambda s, t0: s + band_ref[t0:t0 + SM_ROWS, :])
    o_win = [branch_out(g) for g in groups]

    pieces = []
    for g in groups:
        for hl in range(HPG):
            r = slice(hl * Q_BLOCK, (hl + 1) * Q_BLOCK)
            gb = [gb_scr[g * 3 * HPG + br * HPG + hl, :, 0:DH_B] for br in range(3)]
            comb = gb[0] * o_cmp[g][r]
            comb = comb + gb[1] * o_slc[g][r]
            comb = comb + gb[2] * o_win[g][r]
            pieces.append(comb)
    o_ref[...] = jnp.concatenate(pieces, axis=1).astype(o_ref.dtype)


def _nsa_prompt_call(q, gates, cmp_t, kv_t, win_t, consts, b, t):
    featc, fslc, fwin, fcmp, ovt, band = consts
    nqb = t // Q_BLOCK
    n_wtiles = (WINDOW + t) // Q_BLOCK
    return pl.pallas_call(
        functools.partial(_nsa_prompt_kernel, t_len=t),
        grid=(b, nqb),
        in_specs=[pl.BlockSpec((Q_BLOCK, H_B * A_W), lambda bi, qi: (bi * nqb + qi, 0)),
                  pl.BlockSpec((Q_BLOCK, G_B * LANES), lambda bi, qi: (bi * nqb + qi, 0)),
                  pl.BlockSpec((1, 2, GD, cmp_t.shape[3]), lambda bi, qi: (bi, 0, 0, 0)),
                  pl.BlockSpec((1, 2 * GD, t), lambda bi, qi: (bi, 1, 0)),
                  pl.BlockSpec((1, 2 * GD, t), lambda bi, qi: (bi, 0, 0)),
                  _const_spec(featc.shape), _const_spec(fslc.shape), _const_spec(fwin.shape),
                  _const_spec(fcmp.shape), _const_spec(ovt.shape), _const_spec(band.shape)],
        out_specs=pl.BlockSpec((Q_BLOCK, D_MODEL), lambda bi, qi: (bi * nqb + qi, 0)),
        out_shape=jax.ShapeDtypeStruct((b * t, D_MODEL), BF16),
        scratch_shapes=[pltpu.VMEM((G_B, t // SLC_TILE, A_W, SLC_TILE), BF16),
                        pltpu.VMEM((G_B, t // SLC_TILE, A_W, SLC_TILE), BF16),
                        pltpu.VMEM((G_B, n_wtiles, A_W, Q_BLOCK), BF16),
                        pltpu.VMEM((G_B, n_wtiles, A_W, Q_BLOCK), BF16),
                        pltpu.VMEM((G_B, A_W, Q_BLOCK), BF16),
                        pltpu.VMEM((G_B, HPG * Q_BLOCK, A_W), BF16),
                        pltpu.VMEM((G_B, HPG * Q_BLOCK, WINDOW + Q_BLOCK), F32),
                        pltpu.VMEM((G_B, HPG * Q_BLOCK, WINDOW + Q_BLOCK), BF16),
                        pltpu.VMEM((G_B, HPG * Q_BLOCK, LANES), F32),
                        pltpu.VMEM((G_B, HPG * Q_BLOCK, A_W), F32),
                        pltpu.VMEM((3 * H_B, Q_BLOCK, LANES), F32)],
        compiler_params=_params("parallel", "arbitrary"),
        name="nsa_prompt",
    )(q, gates, cmp_t, kv_t, win_t, featc, fslc, fwin, fcmp, ovt, band)


def _nsa_prompt_consts(t):
    slopes = jnp.exp2(-8.0 * jnp.arange(1, H_B + 1, dtype=F32) / H_B)
    featc = jnp.zeros((H_B, A_W), F32).at[:, F_ONE].set(1.0)
    for k, term in enumerate(_split3(slopes)):
        featc = featc.at[:, F_POS + 2 * k].set(term.astype(F32)).at[:, F_POS + 2 * k + 1].set(term.astype(F32))

    def pos_rows(pos):
        f = np.zeros((A_W - DH_B, pos.shape[0]), np.float32)
        hi = (pos // 256) * 256
        lo = pos % 256
        for k in range(3):
            f[F_POS - DH_B + 2 * k] = hi
            f[F_POS - DH_B + 2 * k + 1] = lo
        return f

    nsel = t // L_SEL
    spos = np.arange(t)
    fslc = pos_rows(spos)
    for j in range(nsel):
        fslc[F_SEL - DH_B + j] = np.where(spos // L_SEL == j, -BIG, 0.0)
    wcol = np.arange(WINDOW + t)
    fwin = pos_rows(wcol)
    fwin[F_ONE - DH_B] = np.where(wcol < WINDOW, -BIG, 0.0)
    fcmp = pos_rows(np.arange(Q_BLOCK) * S_CMP + (L_CMP - 1))
    u = np.arange(WINDOW + Q_BLOCK)[None, :]
    tl = np.arange(Q_BLOCK)[:, None]
    band = np.where((u >= tl) & (u <= tl + WINDOW), 0.0, -BIG).astype(np.float32)
    return (featc, jnp.asarray(fslc, dtype=BF16), jnp.asarray(fwin, dtype=BF16),
            jnp.asarray(fcmp, dtype=BF16), _overlap_t(nsel), jnp.asarray(band))


def _nsa_sample_kernel(pt_ref, *refs, past, ts):
    pages = refs[:N_PAGES_ARG]
    (slope_ref, slopel_ref, q_ref, gate_ref, cmp_ref, kvn_ref, win_ref, winn_ref, eg_ref, ovt_ref, o_ref) = refs[N_PAGES_ARG:]
    rows = H_B * ts
    nsel = -(-(past + ts) // L_SEL)
    lane_g = lax.broadcasted_iota(jnp.int32, (1, GD), 1) // DH_B
    q = q_ref[...]
    qq = jnp.concatenate([jnp.where(lane_g == g, q[:, h * GD:(h + 1) * GD], 0.0)
                          for g in range(G_B) for h in range(HPG)], axis=0).astype(BF16)
    slope = slope_ref[...]
    t_row = lax.broadcasted_iota(jnp.int32, (rows, 1), 0) % ts
    tpos = (past + t_row).astype(F32)
    kc_t = cmp_ref[0, 0]
    vc_t = cmp_ref[0, 1]
    ncb = kc_t.shape[1]
    n_f = lax.broadcasted_iota(jnp.int32, (1, ncb), 1).astype(F32)
    dist_c = tpos - (n_f * S_CMP + (L_CMP - 1.0))
    p = _masked_softmax(_dot(qq, kc_t) - slope * dist_c, dist_c >= 0.0)
    o_cmp = _dot_nt(p.astype(BF16), vc_t)
    ps = []
    for g in range(G_B):
        blk = [p[(g * HPG + h) * ts:(g * HPG + h + 1) * ts] for h in range(HPG)]
        ps.extend([(blk[0] + blk[1]) + (blk[2] + blk[3])] * HPG)
    psum = jnp.concatenate(ps, axis=0)
    imp_t = _block_importance_t(psum, ovt_ref[...])
    nrow = ((nsel + 1 + 7) // 8) * 8
    cur_t = (past + lax.broadcasted_iota(jnp.int32, (nrow, rows), 1) % ts) // L_SEL
    sel_t = _topk_mask_t(imp_t[0:nrow], cur_t, nsel, rows)

    tpos_l = (past + lax.broadcasted_iota(jnp.int32, (1, rows), 1) % ts).astype(F32)
    slope_l = slopel_ref[...]
    key_i = lax.broadcasted_iota(jnp.int32, (PAGE_SIZE, rows), 0)
    bpp = PAGE_SIZE // L_SEL

    def rows_t(new_ref, lo):
        return _pad_rows(new_ref[:, lo:lo + GD], PAGE_SIZE).T.astype(BF16)

    def attend(kt_all, vt_all, tiles):
        s_all = lax.dot_general(kt_all, qq, (((0,), (1,)), ((), ())), preferred_element_type=F32)
        sms = []
        for i, (k0, n_real, blk) in enumerate(tiles):
            s = s_all[i * PAGE_SIZE:(i + 1) * PAGE_SIZE]
            dist = tpos_l - (k0 + key_i).astype(F32)
            dist_m = dist if n_real == PAGE_SIZE else jnp.where(key_i < n_real, dist, -1.0)
            if blk is not None:
                mask = jnp.concatenate([jnp.broadcast_to(sel_t[blk + a:blk + a + 1], (L_SEL, rows)) for a in range(bpp)],
                                       axis=0)
                ok = jnp.where(dist_m >= 0.0, mask, 0.0) > 0.5
            else:
                ok = jnp.abs(dist_m - 0.5 * WINDOW) <= 0.5 * WINDOW
            sms.append(jnp.where(ok, s - slope_l * dist, NEG))
        m = functools.reduce(jnp.maximum, [jnp.max(x, axis=0, keepdims=True) for x in sms])
        es = [jnp.exp(x - m) for x in sms]
        l = functools.reduce(lambda a, b: a + b, [jnp.sum(e, axis=0, keepdims=True) for e in es])
        o_t = _dot(vt_all, jnp.concatenate(es, axis=0).astype(BF16))
        return (o_t / jnp.maximum(l, 1e-30)).T

    kt_slc = jnp.concatenate([pg[0, 0:GD, :].astype(BF16) for pg in pages] + [rows_t(kvn_ref, 0)], axis=1)
    vt_slc = jnp.concatenate([pg[0, GD:2 * GD, :].astype(BF16) for pg in pages] + [rows_t(kvn_ref, GD)], axis=1)
    o_slc = attend(kt_slc, vt_slc, [(pi * PAGE_SIZE, PAGE_SIZE, pi * bpp) for pi in range(len(pages))]
                   + [(past, ts, past // L_SEL)])

    wbuf = win_ref.shape[2]
    kt_win = jnp.concatenate([win_ref[0, 0:GD, :].astype(BF16), rows_t(winn_ref, 0)], axis=1)
    vt_win = jnp.concatenate([win_ref[0, GD:2 * GD, :].astype(BF16), rows_t(winn_ref, GD)], axis=1)
    o_win = attend(kt_win, vt_win, [(past - wbuf + wi * PAGE_SIZE, PAGE_SIZE, None) for wi in range(wbuf // PAGE_SIZE)]
                   + [(past, ts, None)])

    hi, mid, lo = _split3(gate_ref[...])
    eg = eg_ref[...]
    gexp = _dot(hi, eg) + _dot(mid, eg) + _dot(lo, eg)

    def gate_rows(br):
        return jnp.concatenate([gexp[:, br * D_MODEL + h * GD:br * D_MODEL + (h + 1) * GD]
                                for _ in range(G_B) for h in range(HPG)], axis=0)

    comb = gate_rows(0) * o_cmp + gate_rows(1) * o_slc + gate_rows(2) * o_win
    outs = []
    for h in range(HPG):
        acc = None
        for g in range(G_B):
            piece = jnp.where(lane_g == g, comb[(g * HPG + h) * ts:(g * HPG + h + 1) * ts], 0.0)
            acc = piece if acc is None else acc + piece
        outs.append(acc)
    o_ref[...] = jnp.concatenate(outs, axis=1).astype(o_ref.dtype)


def _nsa_sample_call(page_tab, cache_t, slope_rows, q, gates, cmp_t, kvn, win_t, winn, eg, ovt, b, ts):
    npg = page_tab.shape[1]
    assert npg == N_PAGES_ARG
    past = npg * PAGE_SIZE
    page_specs = [pl.BlockSpec((1, 2 * GD, PAGE_SIZE), functools.partial(lambda bi, pt, p: (pt[bi * N_PAGES_ARG + p], 1, 0), p=p))
                  for p in range(npg)]
    wbuf = win_t.shape[2]
    grid_spec = pltpu.PrefetchScalarGridSpec(
        num_scalar_prefetch=1,
        grid=(b,),
        in_specs=page_specs + [
            pl.BlockSpec(slope_rows.shape, lambda bi, pt: (0, 0), pipeline_mode=pl.Buffered(1)),
            pl.BlockSpec((1, slope_rows.shape[0]), lambda bi, pt: (0, 0), pipeline_mode=pl.Buffered(1)),
            pl.BlockSpec((ts, D_MODEL), lambda bi, pt: (bi, 0)),
            pl.BlockSpec((ts, LANES), lambda bi, pt: (bi, 0)),
            pl.BlockSpec((1, 2, GD, cmp_t.shape[3]), lambda bi, pt: (bi, 0, 0, 0)),
            pl.BlockSpec((ts, 2 * GD), lambda bi, pt: (bi, 1)),
            pl.BlockSpec((1, 2 * GD, wbuf), lambda bi, pt: (bi, 0, 0)),
            pl.BlockSpec((ts, 2 * GD), lambda bi, pt: (bi, 0)),
            pl.BlockSpec(eg.shape, lambda bi, pt: (0, 0), pipeline_mode=pl.Buffered(1)),
            pl.BlockSpec(ovt.shape, lambda bi, pt: (0, 0), pipeline_mode=pl.Buffered(1))],
        out_specs=pl.BlockSpec((ts, D_MODEL), lambda bi, pt: (bi, 0)),
    )
    return pl.pallas_call(
        functools.partial(_nsa_sample_kernel, past=past, ts=ts),
        grid_spec=grid_spec,
        out_shape=jax.ShapeDtypeStruct((b * ts, D_MODEL), F32),
        compiler_params=_params("parallel"),
        name="nsa_sample",
    )(page_tab.reshape(-1), *([cache_t] * npg), slope_rows, slope_rows.reshape(1, -1), q, gates, cmp_t, kvn, win_t, winn, eg, ovt)


def _overlap_t(nsel):
    n = np.arange(LANES)[None, :] * S_CMP
    j = np.arange(LANES)[:, None] * L_SEL
    ov = (n <= j + L_SEL - 1) & (n + L_CMP - 1 >= j) & (np.arange(LANES)[None, :] < LANES - 1) & (np.arange(LANES)[:, None] < nsel)
    return jnp.asarray(ov.astype(np.float32), dtype=BF16)


def _gate_expand():
    e = np.zeros((LANES, 3 * D_MODEL), np.float32)
    for br in range(3):
        for hh in range(H_B):
            e[br * H_B + hh, br * D_MODEL + hh * DH_B:br * D_MODEL + (hh + 1) * DH_B] = 1.0
    return jnp.asarray(e, dtype=BF16)


def _cmp_weights(pe, w1, w2):
    r_n = L_CMP // S_CMP
    pe_t = jnp.tile(pe.reshape(r_n, S_CMP, 1, DH_B), (1, 1, G_B, 1)).reshape(r_n, S_CMP, GD)
    eye = jnp.eye(G_B, dtype=F32)
    w1r = w1.reshape(r_n, S_CMP, DH_B, CMP_HID)
    w1b = jnp.einsum("rsdh,ab->rsadbh", w1r, jnp.eye(2, dtype=F32)).reshape(r_n, S_CMP * 2 * DH_B, 2 * CMP_HID)
    w2bt = jnp.einsum("hd,gk->kdgh", w2, eye).reshape(GD, G_B * CMP_HID)
    return pe_t, w1b.astype(BF16), w2bt.astype(BF16)


def _rows_minor(x):
    b, r = x.shape[:2]
    return jnp.moveaxis(x, 1, -1).reshape(b, -1, r)


def _rows_major(x_t, feat_shape):
    b, _, r = x_t.shape
    return jnp.moveaxis(x_t.reshape((b,) + feat_shape + (r,)), -1, 1)


def kernel(x_prompt, x_sample, c_prompt, c_sample, state_ret, state_conv, cache_kv, state_win, page_table, w_ada, b_ada, g_mix, g_ffn, w_ffn_in, conv_w, conv_b, w_ffn_out, w_ret_in, w_ret_out, g_kv, w_ada_kv, b_ada_kv, w_kv, pe_ck, pe_cv, w_ck1, w_ck2, w_cv1, w_cv2, w_nsa_in, w_nsa_out, g_final):
    bp, tp, d = x_prompt.shape
    bs, ts, _ = x_sample.shape
    depth = w_ada.shape[0]
    assert depth == 2 and d == D_MODEL and tp % SLC_TILE == 0 and ts == 8
    npg = page_table.shape[1]
    past = npg * PAGE_SIZE
    f2 = 2 * D_FF
    hd = H_B * DH_B

    w_ada_all = jnp.concatenate([w_ada[0], w_ada[1], w_ada_kv], axis=1).astype(BF16)
    b_ada_all = jnp.concatenate([b_ada[0], b_ada[1], b_ada_kv], axis=0).reshape(1, -1)
    w_ret_in_b = w_ret_in[0].astype(BF16)
    w_ret_out_b = w_ret_out[0].astype(BF16)
    w_ffn_in_b = w_ffn_in.astype(BF16)
    w_ffn_out_b = w_ffn_out.astype(BF16)
    w_kv_b = w_kv.astype(BF16)
    w_kv_t = w_kv_b.T
    w_kv_rows, w_kv_win = w_kv_b[:, :4 * GD], w_kv_b[:, 4 * GD:]
    wn = w_nsa_in[0]
    wq_s = wn[:, :hd] * (DH_B ** -0.5)
    w_q_pad = jnp.pad(wq_s.reshape(d, H_B, DH_B), ((0, 0), (0, 0), (0, A_W - DH_B))).reshape(d, H_B * A_W).astype(BF16)
    w_g_grp = jnp.pad(wn[:, hd:].reshape(d, G_B, HPG, 3).transpose(0, 1, 3, 2).reshape(d, G_B, 3 * HPG),
                      ((0, 0), (0, 0), (0, LANES - 3 * HPG))).reshape(d, G_B * LANES).astype(BF16)
    w_q_s = wq_s.reshape(d, G_B, HPG, DH_B).transpose(0, 2, 1, 3).reshape(d, hd).astype(BF16)
    w_g_s = jnp.pad(wn[:, hd:].reshape(d, G_B, HPG, 3).transpose(0, 3, 2, 1).reshape(d, 3 * H_B),
                    ((0, 0), (0, LANES - 3 * H_B))).astype(BF16)
    w_o_b = w_nsa_out[0].astype(BF16)
    w_o_s = w_nsa_out[0].reshape(G_B, HPG, DH_B, d).transpose(1, 0, 2, 3).reshape(hd, d).astype(BF16)
    pe_k, w1_k, w2_k = _cmp_weights(pe_ck, w_ck1, w_ck2)
    pe_v, w1_v, w2_v = _cmp_weights(pe_cv, w_cv1, w_cv2)
    pe_c = jnp.stack([pe_k, pe_v])
    w1_c = jnp.stack([w1_k, w1_v])
    w2t_c = jnp.stack([w2_k, w2_v])
    slopes = jnp.exp2(-8.0 * jnp.arange(1, H_B + 1, dtype=F32) / H_B)
    log_g = jnp.log1p(-jnp.exp2(-5.0 - jnp.arange(H_A, dtype=F32)))
    lg_tab = jnp.broadcast_to(log_g[:, None, None], (H_A, 1, LANES))

    c_all = jnp.concatenate([c_prompt, c_sample], axis=0)
    mod = _ada_call(c_all, w_ada_all, b_ada_all)

    def mods(lo, hi):
        def cols(k):
            return mod[lo:hi, k * d:(k + 1) * d]
        return [[cols(l * 6 + k) for k in range(6)] for l in range(depth)] + [[cols(12), cols(13)]]

    def layer0(x, mod_l, s_ret, conv_buf, b, t, tm, act_dt):
        sh1, sc1, ga1, sh2, sc2, ga2 = mod_l
        tm_in = 2 * tm if t % (2 * tm) == 0 else tm
        pr = _nmm_call(x, g_mix[0], sh1, sc1, w_ret_in_b, t, tm_in, 1536, act_dt, name="ret_in")
        og, s_new = _ret_call(pr, s_ret, lg_tab, b, t, act_dt)
        x = _mm_res_call(og, w_ret_out_b, x, ga1, t, tm, name="ret_out")
        x, conv0 = _ffn_call(x, g_ffn[0], sh2, sc2, ga2, w_ffn_in_b[0], conv_w[0], conv_b[0], conv_buf,
                             w_ffn_out_b[0], b, t, tm, name="ffn0")
        return x, s_new, conv0

    def layer1_tail(x, o, w_o, mod_l, conv_buf, b, t, tm):
        _, _, ga1, sh2, sc2, ga2 = mod_l
        x = _mm_res_call(o, w_o, x, ga1, t, tm, name="nsa_out")
        return _ffn_call(x, g_ffn[1], sh2, sc2, ga2, w_ffn_in_b[1], conv_w[1], conv_b[1], conv_buf,
                         w_ffn_out_b[1], b, t, tm, g_final=g_final, name="ffn1")

    tm = 512
    mp = mods(0, bp)
    x = x_prompt.reshape(bp * tp, d)
    x, ret_p, conv0_p = layer0(x, mp[0], None, jnp.zeros((bp, CONV_W - 1, f2), F32), bp, tp, tm, BF16)
    sh1, sc1 = mp[1][0], mp[1][1]
    kv_t, win_t = _kvt_call(x, g_kv, mp[2][0], mp[2][1], w_kv_t, bp, tp, tm)
    q = _nmm_call(x, g_mix[1], sh1, sc1, w_q_pad, tp, tm, hd, BF16, name="q_proj")
    gates = _nmm_call(x, g_mix[1], sh1, sc1, w_g_grp, tp, tm, G_B * LANES, F32, act="sigmoid", name="gate_proj")
    assert tp == N_PAGES_ARG * PAGE_SIZE
    ident = jnp.zeros((bp, N_PAGES_ARG), jnp.int32)
    cmp_t = _cmp_call(kv_t, lambda bi, p, pt: (bi, 0, p), ident, pe_c, w1_c, w2t_c)
    o = _nsa_prompt_call(q, gates, cmp_t, kv_t, win_t, _nsa_prompt_consts(tp), bp, tp)
    y_p, conv1_p = layer1_tail(x, o, w_o_b, mp[1], jnp.zeros((bp, CONV_W - 1, f2), F32), bp, tp, tm)
    kv_p = _rows_major(kv_t, (4, G_B, DH_B))
    win_p = _rows_major(win_t[:, :, tp - min(WINDOW, tp):], (2, G_B, DH_B))

    tm = 256
    ms = mods(bp, bp + bs)
    x = x_sample.reshape(bs * ts, d)
    x, ret_s, conv0_s = layer0(x, ms[0], state_ret[0].astype(F32), state_conv[0], bs, ts, tm, F32)
    sh1, sc1 = ms[1][0], ms[1][1]
    kv_rows = _nmm_call(x, g_kv, ms[2][0], ms[2][1], w_kv_rows, ts, tm, 4 * GD, F32, name="kv_rows")
    win_rows = _nmm_call(x, g_kv, ms[2][0], ms[2][1], w_kv_win, ts, tm, 2 * GD, F32, name="win_rows")
    q = _nmm_call(x, g_mix[1], sh1, sc1, w_q_s, ts, tm, hd, F32, name="q_proj")
    gates = _nmm_call(x, g_mix[1], sh1, sc1, w_g_s, ts, tm, LANES, F32, act="sigmoid", name="gate_proj")
    cache_t = _rows_minor(cache_kv)
    state_win_t = _rows_minor(state_win)
    cmp_t = _cmp_call(cache_t, lambda bi, p, pt: (pt[bi * N_PAGES_ARG + p], 0, 0), page_table, pe_c, w1_c, w2t_c)
    slope_rows = jnp.repeat(slopes, ts).reshape(H_B * ts, 1)
    o = _nsa_sample_call(page_table, cache_t, slope_rows, q, gates, cmp_t, kv_rows, state_win_t, win_rows,
                         _gate_expand(), _overlap_t(-(-(past + ts) // L_SEL)), bs, ts)
    y_s, conv1_s = layer1_tail(x, o, w_o_s, ms[1], state_conv[1], bs, ts, tm)
    kv_s = kv_rows.reshape(bs, ts, 4, G_B, DH_B)
    win_new_t = jnp.concatenate([state_win_t, jnp.swapaxes(win_rows.reshape(bs, ts, 2 * GD), 1, 2)], axis=2)
    win_s = _rows_major(win_new_t[:, :, win_new_t.shape[2] - min(WINDOW, win_new_t.shape[2]):], (2, G_B, DH_B))

    return (y_p.reshape(bp, tp, d), y_s.reshape(bs, ts, d), ret_p[None], ret_s[None],
            jnp.stack([conv0_p, conv1_p]), jnp.stack([conv0_s, conv1_s]), kv_p, kv_s, win_p, win_s)
```

```python
import functools

import numpy as np
import jax
import jax.numpy as jnp
from jax import lax
from jax.experimental import pallas as pl
from jax.experimental.pallas import tpu as pltpu

F32 = jnp.float32
BF16 = jnp.bfloat16

D_MODEL = 1024
H_A = 4
DK_A = D_MODEL // H_A
DV_A = 2 * DK_A
RET_CHUNK = 128
H_B = 16
DH_B = D_MODEL // H_B
G_B = 4
HPG = H_B // G_B
GD = G_B * DH_B
L_CMP = 32
S_CMP = 16
L_SEL = 64
N_SEL = 16
WINDOW = 512
CMP_HID = 2 * DH_B
Q_BLOCK = 128
FORCE_BONUS = 1e4
D_FF = 2816
CONV_W = 3
EPS = 1e-6
PAGE_SIZE = 128
NEG = -1e30
BIG = 2.0 ** 100
M_INIT = -(2.0 ** 101)

LANES = 128
SUB = 8
VMEM_LIMIT = 56 * 1024 * 1024
FF_CHUNK = 256
SLC_TILE = 512
SM_ROWS = 32
N_PAGES_ARG = 16

F_POS = DH_B
F_ONE = DH_B + 6
F_SEL = DH_B + 8
A_W = 2 * DH_B

_NT = (((1,), (1,)), ((), ()))


def _sigmoid(x):
    return 1.0 / (1.0 + jnp.exp(-x))


def _silu(x):
    return x * _sigmoid(x)


def _dot(a, b):
    return jnp.dot(a, b, preferred_element_type=F32)


def _dot_nt(a, b):
    return lax.dot_general(a, b, _NT, preferred_element_type=F32)


def _split3(x):
    hi = x.astype(BF16)
    r = x - hi.astype(F32)
    mid = r.astype(BF16)
    lo = (r - mid.astype(F32)).astype(BF16)
    return hi, mid, lo


def _params(*sem):
    return pltpu.CompilerParams(dimension_semantics=sem, vmem_limit_bytes=VMEM_LIMIT)


def _const_spec(shape):
    nd = len(shape)
    return pl.BlockSpec(shape, lambda *a: (0,) * nd, pipeline_mode=pl.Buffered(1))


def _mod_arg(v, m, tm, rows_per_batch):
    table, col, per_row = v
    d = D_MODEL
    if per_row:
        assert table.shape[0] == m and m % tm == 0
        return table.reshape(m // tm, tm, table.shape[1]), (1, tm, d), (lambda i: (i, 0, col))
    assert rows_per_batch % tm == 0
    tpb = rows_per_batch // tm
    return table.reshape(table.shape[0], 1, table.shape[1]), (1, 1, d), (lambda i: (i // tpb, 0, col))


def _ada_kernel(c_ref, w_ref, b_ref, o_ref):
    c = c_ref[...]
    o_ref[...] = _dot(_silu(c).astype(BF16), w_ref[...]) + b_ref[...]


def _ada_call(c, w, b, tn=2048):
    m, d = c.shape
    n = w.shape[1]
    assert n % tn == 0
    return pl.pallas_call(
        _ada_kernel,
        grid=(n // tn,),
        in_specs=[pl.BlockSpec((m, d), lambda j: (0, 0)),
                  pl.BlockSpec((d, tn), lambda j: (0, j)),
                  pl.BlockSpec((1, tn), lambda j: (0, j))],
        out_specs=pl.BlockSpec((m, tn), lambda j: (0, j)),
        out_shape=jax.ShapeDtypeStruct((m, n), F32),
        compiler_params=_params("parallel"),
        name="ada",
    )(c, w, b)


def _norm_mod(x, g, sh, sc):
    y = x * lax.rsqrt(jnp.mean(x * x, axis=-1, keepdims=True) + EPS) * g
    return y * (1.0 + sc) + sh


def _nmm_kernel(x_ref, g_ref, sh_ref, sc_ref, w_ref, o_ref, h_scr, *, act):
    @pl.when(pl.program_id(1) == 0)
    def _():
        h_scr[...] = _norm_mod(x_ref[...], g_ref[...], sh_ref[0], sc_ref[0]).astype(BF16)

    acc = _dot(h_scr[...], w_ref[...])
    if act == "sigmoid":
        acc = _sigmoid(acc)
    o_ref[...] = acc.astype(o_ref.dtype)


def _nmm_call(x, g, sh, sc, w, rows_per_batch, tm, tn, out_dtype, act=None, name="nmm"):
    m, d = x.shape
    n = w.shape[1]
    assert m % tm == 0 and n % tn == 0
    sh_a, mshape, sh_map = _mod_arg(sh, m, tm, rows_per_batch)
    sc_a, _, sc_map = _mod_arg(sc, m, tm, rows_per_batch)
    return pl.pallas_call(
        functools.partial(_nmm_kernel, act=act),
        grid=(m // tm, n // tn),
        in_specs=[pl.BlockSpec((tm, d), lambda i, j: (i, 0)),
                  pl.BlockSpec((1, d), lambda i, j: (0, 0)),
                  pl.BlockSpec(mshape, lambda i, j: sh_map(i)),
                  pl.BlockSpec(mshape, lambda i, j: sc_map(i)),
                  pl.BlockSpec((d, tn), lambda i, j: (0, j))],
        out_specs=pl.BlockSpec((tm, tn), lambda i, j: (i, j)),
        out_shape=jax.ShapeDtypeStruct((m, n), out_dtype),
        scratch_shapes=[pltpu.VMEM((tm, d), BF16)],
        compiler_params=_params("parallel", "arbitrary"),
        name=name,
    )(x, g.reshape(1, d), sh_a, sc_a, w)


def _kvt_kernel(x_ref, g_ref, sh_ref, sc_ref, wt_ref, kv_ref, win_ref):
    h = _norm_mod(x_ref[...], g_ref[...], sh_ref[0], sc_ref[0]).astype(BF16)
    n_kv = kv_ref.shape[1]
    kv_ref[0] = _dot_nt(wt_ref[0:n_kv, :], h)
    win_ref[0] = _dot_nt(wt_ref[n_kv:, :], h)


def _kvt_call(x, g, sh, sc, w_t, b, t, tm):
    m, d = x.shape
    n = w_t.shape[0]
    nt = t // tm
    sh_a, mshape, sh_map = _mod_arg(sh, m, tm, t)
    sc_a, _, sc_map = _mod_arg(sc, m, tm, t)
    return pl.pallas_call(
        _kvt_kernel,
        grid=(b, nt),
        in_specs=[pl.BlockSpec((tm, d), lambda bi, ti: (bi * nt + ti, 0)),
                  pl.BlockSpec((1, d), lambda bi, ti: (0, 0)),
                  pl.BlockSpec(mshape, lambda bi, ti: sh_map(bi * nt + ti)),
                  pl.BlockSpec(mshape, lambda bi, ti: sc_map(bi * nt + ti)),
                  _const_spec((n, d))],
        out_specs=[pl.BlockSpec((1, 4 * GD, tm), lambda bi, ti: (bi, 0, ti)),
                   pl.BlockSpec((1, 2 * GD, tm), lambda bi, ti: (bi, 0, ti))],
        out_shape=[jax.ShapeDtypeStruct((b, 4 * GD, t), F32),
                   jax.ShapeDtypeStruct((b, 2 * GD, t), F32)],
        compiler_params=_params("parallel", "parallel"),
        name="kv_proj_t",
    )(x, g.reshape(1, d), sh_a, sc_a, w_t)


def _mm_res_kernel(a_ref, w_ref, res_ref, ga_ref, o_ref):
    y = _dot(a_ref[...].astype(BF16), w_ref[...])
    o_ref[...] = res_ref[...] + ga_ref[0] * y


def _mm_res_call(a, w, res, gate, rows_per_batch, tm, name="mm_res"):
    m, k = a.shape
    d = w.shape[1]
    ga_a, mshape, mmap = _mod_arg(gate, m, tm, rows_per_batch)
    return pl.pallas_call(
        _mm_res_kernel,
        grid=(m // tm,),
        in_specs=[pl.BlockSpec((tm, k), lambda i: (i, 0)),
                  _const_spec((k, d)),
                  pl.BlockSpec((tm, d), lambda i: (i, 0)),
                  pl.BlockSpec(mshape, lambda i: mmap(i))],
        out_specs=pl.BlockSpec((tm, d), lambda i: (i, 0)),
        out_shape=jax.ShapeDtypeStruct((m, d), F32),
        compiler_params=_params("parallel"),
        name=name,
    )(a, w, res, ga_a)


def _ret_kernel(*refs, c, nc, has_s0):
    if has_s0:
        lg_ref, q_ref, k_ref, v_ref, g_ref, s0_ref, o_ref, so_ref, s_scr = refs
    else:
        lg_ref, q_ref, k_ref, v_ref, g_ref, o_ref, so_ref, s_scr = refs
        s0_ref = None
    n = pl.program_id(1)

    @pl.when(n == 0)
    def _():
        if has_s0:
            s_scr[...] = s0_ref[0]
        else:
            s_scr[...] = jnp.zeros_like(s_scr)

    cp = max(c, RET_CHUNK)

    def padded(a):
        a = a.astype(F32)
        if cp > c:
            a = jnp.concatenate([a, jnp.zeros((cp - c, a.shape[1]), F32)], axis=0)
        return a

    i = lax.broadcasted_iota(jnp.int32, (cp, 1), 0).astype(F32)
    j = lax.broadcasted_iota(jnp.int32, (1, cp), 1).astype(F32)
    diff = i - j
    for h in range(H_A):
        lg = lg_ref[h][:, 0:1]
        q = padded(q_ref[:, h * DK_A:(h + 1) * DK_A]).astype(BF16)
        kf = padded(k_ref[:, h * DK_A:(h + 1) * DK_A]) * (DK_A ** -0.5)
        v = padded(v_ref[:, h * DV_A:(h + 1) * DV_A]).astype(BF16)
        dmask = jnp.where(diff >= 0, jnp.exp(jnp.maximum(diff, 0.0) * lg), 0.0)
        scores = _dot_nt(q, kf.astype(BF16)) * dmask
        s_old = s_scr[h]
        o = _dot(scores.astype(BF16), v) + _dot(q, s_old.astype(BF16)) * jnp.exp((i + 1.0) * lg)
        w = jnp.exp((c - 1.0 - i) * lg)
        kw_t = (kf * w).T.astype(BF16)
        s_new = jnp.exp(c * lg) * s_old + _dot(kw_t, v)
        s_scr[h] = s_new
        of = o * lax.rsqrt(jnp.mean(o * o, axis=-1, keepdims=True) + EPS)
        gate = g_ref[:, h * DV_A:(h + 1) * DV_A].astype(F32)
        o_ref[:, h * DV_A:(h + 1) * DV_A] = (of[0:c] * _silu(gate)).astype(o_ref.dtype)

    @pl.when(n == nc - 1)
    def _():
        so_ref[0] = s_scr[...]


def _ret_call(pr, s0, lg_tab, b, t, out_dtype):
    c = RET_CHUNK if t % RET_CHUNK == 0 else t
    nc = t // c
    has_s0 = s0 is not None
    nq, nv = H_A * DK_A, H_A * DV_A
    assert nv == 2 * nq
    in_specs = [pl.BlockSpec((H_A, 1, LANES), lambda bi, n: (0, 0, 0)),
                pl.BlockSpec((c, nq), lambda bi, n: (bi * nc + n, 0)),
                pl.BlockSpec((c, nq), lambda bi, n: (bi * nc + n, 1)),
                pl.BlockSpec((c, nv), lambda bi, n: (bi * nc + n, 1)),
                pl.BlockSpec((c, nv), lambda bi, n: (bi * nc + n, 2))]
    args = [lg_tab, pr, pr, pr, pr]
    if has_s0:
        in_specs.append(pl.BlockSpec((1, H_A, DK_A, DV_A), lambda bi, n: (bi, 0, 0, 0)))
        args.append(s0)
    return pl.pallas_call(
        functools.partial(_ret_kernel, c=c, nc=nc, has_s0=has_s0),
        grid=(b, nc),
        in_specs=in_specs,
        out_specs=[pl.BlockSpec((c, nv), lambda bi, n: (bi * nc + n, 0)),
                   pl.BlockSpec((1, H_A, DK_A, DV_A), lambda bi, n: (bi, 0, 0, 0))],
        out_shape=[jax.ShapeDtypeStruct((b * t, nv), out_dtype),
                   jax.ShapeDtypeStruct((b, H_A, DK_A, DV_A), F32)],
        scratch_shapes=[pltpu.VMEM((H_A, DK_A, DV_A), F32)],
        compiler_params=_params("parallel", "arbitrary"),
        name="retention",
    )(*args)


def _ffn_kernel(*refs, t_in, nbt, final):
    if final:
        (x_ref, g_ref, sh_ref, sc_ref, ga_ref, win_ref, cw_ref, cb_ref, cbuf_ref, wout_ref, gf_ref,
         o_ref, nc_ref, h_scr, carry_scr, act_scr, shift_scr) = refs
    else:
        (x_ref, g_ref, sh_ref, sc_ref, ga_ref, win_ref, cw_ref, cb_ref, cbuf_ref, wout_ref,
         o_ref, nc_ref, h_scr, carry_scr, act_scr, shift_scr) = refs
        gf_ref = None
    tm = x_ref.shape[0]

    @pl.when(pl.program_id(1) == 0)
    def _():
        carry_scr[...] = cbuf_ref[...]

    x = x_ref[...]
    h_scr[...] = _norm_mod(x, g_ref[...], sh_ref[0], sc_ref[0]).astype(BF16)
    cwd = FF_CHUNK
    tpos = lax.broadcasted_iota(jnp.int32, (nbt, t_in, cwd), 1)
    is0 = tpos == 0
    is1 = tpos == 1

    def conv_rolled(off):
        u = _dot(h_scr[...], win_ref[:, off:off + cwd]).reshape(nbt, t_in, cwd)
        prev = carry_scr[:, :, off:off + cwd]
        p0 = prev[:, 0:1, :]
        p1 = prev[:, 1:2, :]
        r1 = pltpu.roll(u, 1, axis=1)
        r2 = pltpu.roll(u, 2, axis=1)
        um1 = jnp.where(is0, p1, r1)
        um2 = jnp.where(is0, p0, jnp.where(is1, p1, r2))
        newc = r2[:, 0:2, :]
        carry_scr[:, :, off:off + cwd] = newc
        nc_ref[:, :, off:off + cwd] = newc
        cw = cw_ref[:, off:off + cwd]
        z = cb_ref[:, off:off + cwd] + cw[0:1] * um2
        z = z + cw[1:2] * um1
        z = z + cw[2:3] * u
        return z.reshape(tm, cwd)

    def conv_shifted(off, slot):
        buf = shift_scr.at[slot]
        buf[SUB:SUB + tm, :] = _dot(h_scr[...], win_ref[:, off:off + cwd])
        buf[SUB - 2:SUB, :] = carry_scr[0, :, off:off + cwd]
        newc = buf[SUB + tm - 2:SUB + tm, :]
        carry_scr[0, :, off:off + cwd] = newc
        nc_ref[0, :, off:off + cwd] = newc
        cw = cw_ref[:, off:off + cwd]
        z = cb_ref[:, off:off + cwd] + cw[0:1] * buf[SUB - 2:SUB - 2 + tm, :]
        z = z + cw[1:2] * buf[SUB - 1:SUB - 1 + tm, :]
        z = z + cw[2:3] * buf[SUB:SUB + tm, :]
        return z

    n_slots = shift_scr.shape[0]
    for ch in range(D_FF // cwd):
        if nbt == 1:
            za = conv_shifted(ch * cwd, (2 * ch) % n_slots)
            zg = conv_shifted(D_FF + ch * cwd, (2 * ch + 1) % n_slots)
        else:
            za = conv_rolled(ch * cwd)
            zg = conv_rolled(D_FF + ch * cwd)
        act_scr[:, ch * cwd:(ch + 1) * cwd] = (_silu(zg) * za).astype(BF16)
    xn = x + ga_ref[0] * _dot(act_scr[...], wout_ref[...])
    if final:
        xn = xn * lax.rsqrt(jnp.mean(xn * xn, axis=-1, keepdims=True) + EPS) * gf_ref[...]
    o_ref[...] = xn


def _ffn_call(x, g, sh, sc, ga, w_in, cw, cb, cbuf, w_out, b, t, tm, g_final=None, name="ffn"):
    m, d = x.shape
    f2 = w_in.shape[1]
    if t % tm == 0:
        t_in, nbt, nb_tiles, nt = tm, 1, b, t // tm
    else:
        assert tm % t == 0 and m % tm == 0
        t_in, nbt, nb_tiles, nt = t, tm // t, m // tm, 1
    sh_a, mshape, sh_map = _mod_arg(sh, m, tm, t)
    sc_a, _, sc_map = _mod_arg(sc, m, tm, t)
    ga_a, _, ga_map = _mod_arg(ga, m, tm, t)
    final = g_final is not None
    in_specs = [pl.BlockSpec((tm, d), lambda bi, ti: (bi * nt + ti, 0)),
                pl.BlockSpec((1, d), lambda bi, ti: (0, 0)),
                pl.BlockSpec(mshape, lambda bi, ti: sh_map(bi * nt + ti)),
                pl.BlockSpec(mshape, lambda bi, ti: sc_map(bi * nt + ti)),
                pl.BlockSpec(mshape, lambda bi, ti: ga_map(bi * nt + ti)),
                _const_spec((d, f2)),
                _const_spec((CONV_W, f2)),
                _const_spec((1, f2)),
                pl.BlockSpec((nbt, CONV_W - 1, f2), lambda bi, ti: (bi, 0, 0)),
                _const_spec((D_FF, d))]
    args = [x, g.reshape(1, d), sh_a, sc_a, ga_a, w_in, cw, cb.reshape(1, f2), cbuf, w_out]
    if final:
        in_specs.append(pl.BlockSpec((1, d), lambda bi, ti: (0, 0)))
        args.append(g_final.reshape(1, d))
    return pl.pallas_call(
        functools.partial(_ffn_kernel, t_in=t_in, nbt=nbt, final=final),
        grid=(nb_tiles, nt),
        in_specs=in_specs,
        out_specs=[pl.BlockSpec((tm, d), lambda bi, ti: (bi * nt + ti, 0)),
                   pl.BlockSpec((nbt, CONV_W - 1, f2), lambda bi, ti: (bi, 0, 0))],
        out_shape=[jax.ShapeDtypeStruct((m, d), F32),
                   jax.ShapeDtypeStruct((b, CONV_W - 1, f2), F32)],
        scratch_shapes=[pltpu.VMEM((tm, d), BF16),
                        pltpu.VMEM((nbt, CONV_W - 1, f2), F32),
                        pltpu.VMEM((tm, D_FF), BF16),
                        pltpu.VMEM((4, SUB + tm, FF_CHUNK), F32)],
        compiler_params=_params("parallel", "arbitrary"),
        name=name,
    )(*args)


def _cmp_kernel(pt_ref, *refs):
    pages = refs[:N_PAGES_ARG]
    pe_ref, w1_ref, w2t_ref, o_ref = refs[N_PAGES_ARG:]
    npc = PAGE_SIZE // S_CMP
    r_n = L_CMP // S_CMP
    for kind in range(2):
        rows = jnp.concatenate([pg[0, kind * GD:(kind + 1) * GD, :].T for pg in pages], axis=0)
        rows = jnp.swapaxes(rows.reshape(len(pages) * npc, S_CMP, GD), 0, 1)
        pieces = [rows[s] for s in range(S_CMP)]
        z = None
        for r in range(r_n):
            ys = []
            for gp in range(G_B // 2):
                ls = slice(gp * LANES, (gp + 1) * LANES)
                a = jnp.concatenate([(pieces[s][:, ls] + pe_ref[kind, r, s:s + 1, ls]).astype(BF16)
                                     for s in range(S_CMP)], axis=1)
                ys.append(_dot(a, w1_ref[kind, r]))
            y = jnp.concatenate(ys, axis=1)
            if r > 0:
                y = pltpu.roll(y, y.shape[0] - r, axis=0)
            z = y if z is None else z + y
        o_ref[0, kind] = _dot_nt(w2t_ref[kind], _silu(z).astype(BF16)).astype(o_ref.dtype)


def _cmp_call(rows_t, page_map, page_tab, pe, w1, w2t):
    b, npg = page_tab.shape
    assert npg == N_PAGES_ARG
    n_pieces = npg * PAGE_SIZE // S_CMP
    page_specs = [pl.BlockSpec((1, 2 * GD, PAGE_SIZE), functools.partial(lambda bi, pt, p: page_map(bi, p, pt), p=p))
                  for p in range(npg)]
    grid_spec = pltpu.PrefetchScalarGridSpec(
        num_scalar_prefetch=1,
        grid=(b,),
        in_specs=page_specs + [
            pl.BlockSpec(pe.shape, lambda bi, pt: (0, 0, 0, 0), pipeline_mode=pl.Buffered(1)),
            pl.BlockSpec(w1.shape, lambda bi, pt: (0, 0, 0, 0), pipeline_mode=pl.Buffered(1)),
            pl.BlockSpec(w2t.shape, lambda bi, pt: (0, 0, 0), pipeline_mode=pl.Buffered(1))],
        out_specs=pl.BlockSpec((1, 2, GD, n_pieces), lambda bi, pt: (bi, 0, 0, 0)),
    )
    return pl.pallas_call(
        _cmp_kernel,
        grid_spec=grid_spec,
        out_shape=jax.ShapeDtypeStruct((b, 2, GD, n_pieces), BF16),
        compiler_params=_params("parallel"),
        name="compress",
    )(page_tab.reshape(-1), *([rows_t] * npg), pe, w1, w2t)


def _masked_softmax(s, ok):
    sm = jnp.where(ok, s, NEG)
    m = jnp.max(sm, axis=-1, keepdims=True)
    e = jnp.where(ok, jnp.exp(sm - m), 0.0)
    return e / jnp.maximum(jnp.sum(e, axis=-1, keepdims=True), 1e-30)


def _topk_mask_steps(imp_t, cur, nsel, ncol):
    rows = imp_t.shape[0]
    jj = lax.broadcasted_iota(jnp.int32, (rows, ncol), 0)
    forced = (jj == 0) | (jj == cur) | (jj == cur - 1)
    score = jnp.where(jj <= cur, imp_t + FORCE_BONUS * forced.astype(F32), NEG)
    rank = jnp.zeros((rows, ncol), F32)
    for i in range(nsel):
        row = score[i:i + 1, :]
        lower = jnp.where(jj > i, 1.0, 0.0)
        rank = rank + jnp.where(row > score, 1.0, jnp.where(row == score, lower, 0.0))
        if i % 8 == 7 and i + 1 < nsel:
            yield None
    yield jnp.where((rank < float(min(N_SEL, nsel))) & (jj < nsel), 1.0, 0.0)


def _topk_mask_t(imp_t, cur, nsel, ncol):
    out = None
    for out in _topk_mask_steps(imp_t, cur, nsel, ncol):
        pass
    return out


def _round_robin(*streams):
    live = list(streams)
    while live:
        for s in list(live):
            try:
                next(s)
            except StopIteration:
                live.remove(s)


def _eye_bf16(n):
    return jnp.where(lax.broadcasted_iota(jnp.int32, (n, n), 0) == lax.broadcasted_iota(jnp.int32, (n, n), 1),
                     1.0, 0.0).astype(BF16)


def _pad_rows(a, rows):
    if a.shape[0] == rows:
        return a
    return jnp.concatenate([a, jnp.zeros((rows - a.shape[0],) + a.shape[1:], a.dtype)], axis=0)


def _online_step(sm, v, m_old, l_old, acc_old, v_transposed):
    m_new = jnp.maximum(m_old, jnp.max(sm, axis=-1, keepdims=True))
    alpha = jnp.exp(m_old - m_new)
    e = jnp.where(sm > 0.5 * NEG, jnp.exp(sm - m_new), 0.0)
    l_new = alpha * l_old + jnp.sum(e, axis=-1, keepdims=True)
    pv = _dot_nt(e.astype(BF16), v) if v_transposed else _dot(e.astype(BF16), v)
    return m_new, l_new, alpha * acc_old + pv


def _block_importance_t(ps_sum, ovt):
    hi, mid, lo = _split3(ps_sum)
    return _dot_nt(ovt, hi) + _dot_nt(ovt, mid) + _dot_nt(ovt, lo)


def _nsa_prompt_kernel(q_ref, gate_ref, cmp_ref, kv_ref, win_ref, featc_ref, fslc_ref, fwin_ref, fcmp_ref,
                       ovt_ref, band_ref, o_ref,
                       kb_slc, vt_slc, kb_win, vt_win, kb_cmp, a_scr, s_scr, e_scr, m_scr, acc_scr, gb_scr, *, t_len):
    qb = pl.program_id(1)
    s0 = qb * Q_BLOCK
    nsel = t_len // L_SEL
    n_st = t_len // SLC_TILE
    n_wt = WINDOW // Q_BLOCK
    n_wtiles = kb_win.shape[1]
    rows_g = HPG * Q_BLOCK

    def ones_rows(width):
        r = lax.broadcasted_iota(jnp.int32, (A_W - DH_B, width), 0)
        return jnp.where(r == 0, 1.0, 0.0).astype(BF16)

    @pl.when(qb == 0)
    def _():
        for g in range(G_B):
            for kt in range(n_st):
                cs = slice(kt * SLC_TILE, (kt + 1) * SLC_TILE)
                kb_slc[g, kt, 0:DH_B, :] = kv_ref[0, g * DH_B:(g + 1) * DH_B, cs].astype(BF16)
                kb_slc[g, kt, DH_B:A_W, :] = fslc_ref[:, cs]
            for wt in range(n_wtiles):
                cs = slice((wt - n_wt) * Q_BLOCK, (wt - n_wt + 1) * Q_BLOCK)
                if wt < n_wt:
                    kb_win[g, wt, 0:DH_B, :] = jnp.zeros((DH_B, Q_BLOCK), BF16)
                else:
                    kb_win[g, wt, 0:DH_B, :] = win_ref[0, g * DH_B:(g + 1) * DH_B, cs].astype(BF16)
                kb_win[g, wt, DH_B:A_W, :] = fwin_ref[:, wt * Q_BLOCK:(wt + 1) * Q_BLOCK]
            kb_cmp[g, 0:DH_B, :] = cmp_ref[0, 0, g * DH_B:(g + 1) * DH_B, :]
            kb_cmp[g, DH_B:A_W, :] = fcmp_ref[...]
            for kt in range(n_st):
                cs = slice(kt * SLC_TILE, (kt + 1) * SLC_TILE)
                vt_slc[g, kt, 0:DH_B, :] = kv_ref[0, GD + g * DH_B:GD + (g + 1) * DH_B, cs].astype(BF16)
                vt_slc[g, kt, DH_B:A_W, :] = ones_rows(SLC_TILE)
            for wt in range(n_wtiles):
                cs = slice((wt - n_wt) * Q_BLOCK, (wt - n_wt + 1) * Q_BLOCK)
                if wt < n_wt:
                    vt_win[g, wt, 0:DH_B, :] = jnp.zeros((DH_B, Q_BLOCK), BF16)
                else:
                    vt_win[g, wt, 0:DH_B, :] = win_ref[0, GD + g * DH_B:GD + (g + 1) * DH_B, cs].astype(BF16)
                vt_win[g, wt, DH_B:A_W, :] = ones_rows(Q_BLOCK)

    lane = lax.broadcasted_iota(jnp.int32, (1, A_W), 1)
    ags = [jnp.concatenate([jnp.where(lane < DH_B, q_ref[:, (g * HPG + hl) * A_W:(g * HPG + hl + 1) * A_W],
                                      featc_ref[g * HPG + hl:g * HPG + hl + 1, :].astype(BF16))
                            for hl in range(HPG)], axis=0) for g in range(G_B)]

    ovt = ovt_ref[...]
    eye = _eye_bf16(Q_BLOCK)
    t_loc = lax.broadcasted_iota(jnp.int32, (Q_BLOCK, 1), 0)
    tpos = (s0 + t_loc).astype(F32)
    n_f = lax.broadcasted_iota(jnp.int32, (1, Q_BLOCK), 1).astype(F32)
    ok_c = (tpos - (n_f * S_CMP + (L_CMP - 1.0))) >= 0.0
    nrow = ((nsel + 7) // 8) * 8
    groups = range(G_B)

    def softmax_steps(g, width, bias_fn, out):
        m_old_all = m_scr[g]
        m_news, alphas = [], []
        for rb in range(rows_g // SM_ROWS):
            rows = slice(rb * SM_ROWS, (rb + 1) * SM_ROWS)
            s = s_scr[g, rows, 0:width]
            if bias_fn is not None:
                s = bias_fn(s, (rb * SM_ROWS) % Q_BLOCK)
            m_old = m_old_all[rows]
            m_new = jnp.maximum(m_old, jnp.max(s, axis=-1, keepdims=True))
            e = jnp.exp(s - jnp.concatenate([m_new] * (width // LANES), axis=1))
            e_scr[g, rows, 0:width] = e.astype(BF16)
            m_news.append(m_new)
            alphas.append(jnp.exp(m_old - m_new))
            if rb % 2 == 1:
                yield
        m_scr[g] = jnp.concatenate(m_news, axis=0)
        out.append(jnp.concatenate(alphas, axis=0))

    def reset_stats():
        m_scr[...] = jnp.full_like(m_scr, M_INIT)
        acc_scr[...] = jnp.zeros_like(acc_scr)

    def attend_steps(width, a_fn, key_fn, val_fn, bias_fn):
        s_scr[0, :, 0:width] = _dot(a_fn(0), key_fn(0))
        yield
        for g in groups:
            if g + 1 < G_B:
                s_scr[g + 1, :, 0:width] = _dot(a_fn(g + 1), key_fn(g + 1))
            alpha = []
            yield from softmax_steps(g, width, bias_fn, alpha)
            acc_scr[g] = alpha[0] * acc_scr[g] + _dot_nt(e_scr[g, :, 0:width], val_fn(g))
            yield

    def attend(*args):
        for _ in attend_steps(*args):
            pass

    def branch_out(g):
        acc = acc_scr[g]
        return acc[:, 0:DH_B] / jnp.maximum(acc[:, DH_B:DH_B + 1], 1e-30)

    reset_stats()
    win_stream = attend_steps(WINDOW + Q_BLOCK,
                              lambda g: ags[g],
                              lambda g: jnp.concatenate([kb_win[g, qb + w] for w in range(n_wt + 1)], axis=1),
                              lambda g: jnp.concatenate([vt_win[g, qb + w] for w in range(n_wt + 1)], axis=1),
                              lambda s, t0: s + band_ref[t0:t0 + SM_ROWS, :])

    cur_t = (s0 + lax.broadcasted_iota(jnp.int32, (nrow, Q_BLOCK), 1)) // L_SEL
    o_cmp, ags_sel = [], []

    def sel_stream():
        for g in groups:
            s_c = _dot(ags[g], kb_cmp[g]).reshape(HPG, Q_BLOCK, Q_BLOCK)
            yield
            p3 = _masked_softmax(s_c, ok_c[None])
            yield
            o_cmp.append(_dot_nt(p3.reshape(rows_g, Q_BLOCK).astype(BF16), cmp_ref[0, 1, g * DH_B:(g + 1) * DH_B, :]))
            imp_t = _block_importance_t((p3[0] + p3[1]) + (p3[2] + p3[3]), ovt)
            yield
            sel_t = None
            for sel_t in _topk_mask_steps(imp_t[0:nrow], cur_t, nsel, Q_BLOCK):
                yield
            nsel_pad = jnp.concatenate([jnp.zeros((F_SEL, Q_BLOCK), F32), 1.0 - sel_t,
                                        jnp.zeros((A_W - F_SEL - nrow, Q_BLOCK), F32)], axis=0).astype(BF16)
            nsl = _dot_nt(eye, nsel_pad).astype(BF16)
            ags_sel.append(ags[g] + jnp.concatenate([nsl] * HPG, axis=0))
            yield

    def gate_stream():
        for g in groups:
            gt = gate_ref[:, g * LANES:(g + 1) * LANES]
            for c in range(3 * HPG):
                gb_scr[g * 3 * HPG + c] = jnp.broadcast_to(gt[:, c:c + 1], (Q_BLOCK, LANES))
                if c % 2 == 1:
                    yield

    _round_robin(win_stream, sel_stream(), gate_stream())
    o_win = [branch_out(g) for g in groups]
    for g in groups:
        a_scr[g] = ags_sel[g]

    def causal_bias(kti):
        def fn(s, t0):
            u = lax.broadcasted_iota(jnp.int32, s.shape, 1)
            t = t0 + lax.broadcasted_iota(jnp.int32, s.shape, 0)
            return jnp.where(u <= (s0 - kti * SLC_TILE) + t, s, -BIG)
        return fn

    def slc_tile(kti, diag):
        attend(SLC_TILE, lambda g: a_scr[g], lambda g: kb_slc[g, kti], lambda g: vt_slc[g, kti],
               causal_bias(kti) if diag else None)

    reset_stats()
    kd = s0 // SLC_TILE

    def slc_full(kti, c):
        slc_tile(kti, False)
        return c

    lax.fori_loop(0, kd, slc_full, 0)
    slc_tile(kd, True)
    o_slc = [branch_out(g) for g in groups]

    pieces = []
    for g in groups:
        for hl in range(HPG):
            r = slice(hl * Q_BLOCK, (hl + 1) * Q_BLOCK)
            gb = [gb_scr[g * 3 * HPG + br * HPG + hl, :, 0:DH_B] for br in range(3)]
            comb = gb[0] * o_cmp[g][r]
            comb = comb + gb[1] * o_slc[g][r]
            comb = comb + gb[2] * o_win[g][r]
            pieces.append(comb)
    o_ref[...] = jnp.concatenate(pieces, axis=1).astype(o_ref.dtype)


def _nsa_prompt_call(q, gates, cmp_t, kv_t, win_t, consts, b, t):
    featc, fslc, fwin, fcmp, ovt, band = consts
    nqb = t // Q_BLOCK
    n_wtiles = (WINDOW + t) // Q_BLOCK
    return pl.pallas_call(
        functools.partial(_nsa_prompt_kernel, t_len=t),
        grid=(b, nqb),
        in_specs=[pl.BlockSpec((Q_BLOCK, H_B * A_W), lambda bi, qi: (bi * nqb + qi, 0)),
                  pl.BlockSpec((Q_BLOCK, G_B * LANES), lambda bi, qi: (bi * nqb + qi, 0)),
                  pl.BlockSpec((1, 2, GD, cmp_t.shape[3]), lambda bi, qi: (bi, 0, 0, 0)),
                  pl.BlockSpec((1, 2 * GD, t), lambda bi, qi: (bi, 1, 0)),
                  pl.BlockSpec((1, 2 * GD, t), lambda bi, qi: (bi, 0, 0)),
                  _const_spec(featc.shape), _const_spec(fslc.shape), _const_spec(fwin.shape),
                  _const_spec(fcmp.shape), _const_spec(ovt.shape), _const_spec(band.shape)],
        out_specs=pl.BlockSpec((Q_BLOCK, D_MODEL), lambda bi, qi: (bi * nqb + qi, 0)),
        out_shape=jax.ShapeDtypeStruct((b * t, D_MODEL), BF16),
        scratch_shapes=[pltpu.VMEM((G_B, t // SLC_TILE, A_W, SLC_TILE), BF16),
                        pltpu.VMEM((G_B, t // SLC_TILE, A_W, SLC_TILE), BF16),
                        pltpu.VMEM((G_B, n_wtiles, A_W, Q_BLOCK), BF16),
                        pltpu.VMEM((G_B, n_wtiles, A_W, Q_BLOCK), BF16),
                        pltpu.VMEM((G_B, A_W, Q_BLOCK), BF16),
                        pltpu.VMEM((G_B, HPG * Q_BLOCK, A_W), BF16),
                        pltpu.VMEM((G_B, HPG * Q_BLOCK, WINDOW + Q_BLOCK), F32),
                        pltpu.VMEM((G_B, HPG * Q_BLOCK, WINDOW + Q_BLOCK), BF16),
                        pltpu.VMEM((G_B, HPG * Q_BLOCK, LANES), F32),
                        pltpu.VMEM((G_B, HPG * Q_BLOCK, A_W), F32),
                        pltpu.VMEM((3 * H_B, Q_BLOCK, LANES), F32)],
        compiler_params=_params("parallel", "arbitrary"),
        name="nsa_prompt",
    )(q, gates, cmp_t, kv_t, win_t, featc, fslc, fwin, fcmp, ovt, band)


def _nsa_prompt_consts(t):
    slopes = jnp.exp2(-8.0 * jnp.arange(1, H_B + 1, dtype=F32) / H_B)
    featc = jnp.zeros((H_B, A_W), F32).at[:, F_ONE].set(1.0)
    for k, term in enumerate(_split3(slopes)):
        featc = featc.at[:, F_POS + 2 * k].set(term.astype(F32)).at[:, F_POS + 2 * k + 1].set(term.astype(F32))

    def pos_rows(pos):
        f = np.zeros((A_W - DH_B, pos.shape[0]), np.float32)
        hi = (pos // 256) * 256
        lo = pos % 256
        for k in range(3):
            f[F_POS - DH_B + 2 * k] = hi
            f[F_POS - DH_B + 2 * k + 1] = lo
        return f

    nsel = t // L_SEL
    spos = np.arange(t)
    fslc = pos_rows(spos)
    for j in range(nsel):
        fslc[F_SEL - DH_B + j] = np.where(spos // L_SEL == j, -BIG, 0.0)
    wcol = np.arange(WINDOW + t)
    fwin = pos_rows(wcol)
    fwin[F_ONE - DH_B] = np.where(wcol < WINDOW, -BIG, 0.0)
    fcmp = pos_rows(np.arange(Q_BLOCK) * S_CMP + (L_CMP - 1))
    u = np.arange(WINDOW + Q_BLOCK)[None, :]
    tl = np.arange(Q_BLOCK)[:, None]
    band = np.where((u >= tl) & (u <= tl + WINDOW), 0.0, -BIG).astype(np.float32)
    return (featc, jnp.asarray(fslc, dtype=BF16), jnp.asarray(fwin, dtype=BF16),
            jnp.asarray(fcmp, dtype=BF16), _overlap_t(nsel), jnp.asarray(band))


def _nsa_sample_kernel(pt_ref, *refs, past, ts):
    pages = refs[:N_PAGES_ARG]
    (slope_ref, slopel_ref, q_ref, gate_ref, cmp_ref, kvn_ref, win_ref, winn_ref, eg_ref, ovt_ref, o_ref) = refs[N_PAGES_ARG:]
    rows = H_B * ts
    nsel = -(-(past + ts) // L_SEL)
    lane_g = lax.broadcasted_iota(jnp.int32, (1, GD), 1) // DH_B
    q = q_ref[...]
    qq = jnp.concatenate([jnp.where(lane_g == g, q[:, h * GD:(h + 1) * GD], 0.0)
                          for g in range(G_B) for h in range(HPG)], axis=0).astype(BF16)
    slope = slope_ref[...]
    t_row = lax.broadcasted_iota(jnp.int32, (rows, 1), 0) % ts
    tpos = (past + t_row).astype(F32)
    kc_t = cmp_ref[0, 0]
    vc_t = cmp_ref[0, 1]
    ncb = kc_t.shape[1]
    n_f = lax.broadcasted_iota(jnp.int32, (1, ncb), 1).astype(F32)
    dist_c = tpos - (n_f * S_CMP + (L_CMP - 1.0))
    p = _masked_softmax(_dot(qq, kc_t) - slope * dist_c, dist_c >= 0.0)
    o_cmp = _dot_nt(p.astype(BF16), vc_t)
    ps = []
    for g in range(G_B):
        blk = [p[(g * HPG + h) * ts:(g * HPG + h + 1) * ts] for h in range(HPG)]
        ps.extend([(blk[0] + blk[1]) + (blk[2] + blk[3])] * HPG)
    psum = jnp.concatenate(ps, axis=0)
    imp_t = _block_importance_t(psum, ovt_ref[...])
    nrow = ((nsel + 1 + 7) // 8) * 8
    cur_t = (past + lax.broadcasted_iota(jnp.int32, (nrow, rows), 1) % ts) // L_SEL
    sel_t = _topk_mask_t(imp_t[0:nrow], cur_t, nsel, rows)

    tpos_l = (past + lax.broadcasted_iota(jnp.int32, (1, rows), 1) % ts).astype(F32)
    slope_l = slopel_ref[...]
    key_i = lax.broadcasted_iota(jnp.int32, (PAGE_SIZE, rows), 0)
    bpp = PAGE_SIZE // L_SEL

    def rows_t(new_ref, lo):
        return _pad_rows(new_ref[:, lo:lo + GD], PAGE_SIZE).T.astype(BF16)

    def attend(kt_all, vt_all, tiles):
        s_all = lax.dot_general(kt_all, qq, (((0,), (1,)), ((), ())), preferred_element_type=F32)
        sms = []
        for i, (k0, n_real, blk) in enumerate(tiles):
            s = s_all[i * PAGE_SIZE:(i + 1) * PAGE_SIZE]
            dist = tpos_l - (k0 + key_i).astype(F32)
            dist_m = dist if n_real == PAGE_SIZE else jnp.where(key_i < n_real, dist, -1.0)
            if blk is not None:
                mask = jnp.concatenate([jnp.broadcast_to(sel_t[blk + a:blk + a + 1], (L_SEL, rows)) for a in range(bpp)],
                                       axis=0)
                ok = jnp.where(dist_m >= 0.0, mask, 0.0) > 0.5
            else:
                ok = jnp.abs(dist_m - 0.5 * WINDOW) <= 0.5 * WINDOW
            sms.append(jnp.where(ok, s - slope_l * dist, NEG))
        m = functools.reduce(jnp.maximum, [jnp.max(x, axis=0, keepdims=True) for x in sms])
        es = [jnp.exp(x - m) for x in sms]
        l = functools.reduce(lambda a, b: a + b, [jnp.sum(e, axis=0, keepdims=True) for e in es])
        o_t = _dot(vt_all, jnp.concatenate(es, axis=0).astype(BF16))
        return (o_t / jnp.maximum(l, 1e-30)).T

    kt_slc = jnp.concatenate([pg[0, 0:GD, :].astype(BF16) for pg in pages] + [rows_t(kvn_ref, 0)], axis=1)
    vt_slc = jnp.concatenate([pg[0, GD:2 * GD, :].astype(BF16) for pg in pages] + [rows_t(kvn_ref, GD)], axis=1)
    o_slc = attend(kt_slc, vt_slc, [(pi * PAGE_SIZE, PAGE_SIZE, pi * bpp) for pi in range(len(pages))]
                   + [(past, ts, past // L_SEL)])

    wbuf = win_ref.shape[2]
    kt_win = jnp.concatenate([win_ref[0, 0:GD, :].astype(BF16), rows_t(winn_ref, 0)], axis=1)
    vt_win = jnp.concatenate([win_ref[0, GD:2 * GD, :].astype(BF16), rows_t(winn_ref, GD)], axis=1)
    o_win = attend(kt_win, vt_win, [(past - wbuf + wi * PAGE_SIZE, PAGE_SIZE, None) for wi in range(wbuf // PAGE_SIZE)]
                   + [(past, ts, None)])

    hi, mid, lo = _split3(gate_ref[...])
    eg = eg_ref[...]
    gexp = _dot(hi, eg) + _dot(mid, eg) + _dot(lo, eg)

    def gate_rows(br):
        return jnp.concatenate([gexp[:, br * D_MODEL + h * GD:br * D_MODEL + (h + 1) * GD]
                                for _ in range(G_B) for h in range(HPG)], axis=0)

    comb = gate_rows(0) * o_cmp + gate_rows(1) * o_slc + gate_rows(2) * o_win
    outs = []
    for h in range(HPG):
        acc = None
        for g in range(G_B):
            piece = jnp.where(lane_g == g, comb[(g * HPG + h) * ts:(g * HPG + h + 1) * ts], 0.0)
            acc = piece if acc is None else acc + piece
        outs.append(acc)
    o_ref[...] = jnp.concatenate(outs, axis=1).astype(o_ref.dtype)


def _nsa_sample_call(page_tab, cache_t, slope_rows, q, gates, cmp_t, kvn, win_t, winn, eg, ovt, b, ts):
    npg = page_tab.shape[1]
    assert npg == N_PAGES_ARG
    past = npg * PAGE_SIZE
    page_specs = [pl.BlockSpec((1, 2 * GD, PAGE_SIZE), functools.partial(lambda bi, pt, p: (pt[bi * N_PAGES_ARG + p], 1, 0), p=p))
                  for p in range(npg)]
    wbuf = win_t.shape[2]
    grid_spec = pltpu.PrefetchScalarGridSpec(
        num_scalar_prefetch=1,
        grid=(b,),
        in_specs=page_specs + [
            pl.BlockSpec(slope_rows.shape, lambda bi, pt: (0, 0), pipeline_mode=pl.Buffered(1)),
            pl.BlockSpec((1, slope_rows.shape[0]), lambda bi, pt: (0, 0), pipeline_mode=pl.Buffered(1)),
            pl.BlockSpec((ts, D_MODEL), lambda bi, pt: (bi, 0)),
            pl.BlockSpec((ts, LANES), lambda bi, pt: (bi, 0)),
            pl.BlockSpec((1, 2, GD, cmp_t.shape[3]), lambda bi, pt: (bi, 0, 0, 0)),
            pl.BlockSpec((ts, 2 * GD), lambda bi, pt: (bi, 1)),
            pl.BlockSpec((1, 2 * GD, wbuf), lambda bi, pt: (bi, 0, 0)),
            pl.BlockSpec((ts, 2 * GD), lambda bi, pt: (bi, 0)),
            pl.BlockSpec(eg.shape, lambda bi, pt: (0, 0), pipeline_mode=pl.Buffered(1)),
            pl.BlockSpec(ovt.shape, lambda bi, pt: (0, 0), pipeline_mode=pl.Buffered(1))],
        out_specs=pl.BlockSpec((ts, D_MODEL), lambda bi, pt: (bi, 0)),
    )
    return pl.pallas_call(
        functools.partial(_nsa_sample_kernel, past=past, ts=ts),
        grid_spec=grid_spec,
        out_shape=jax.ShapeDtypeStruct((b * ts, D_MODEL), F32),
        compiler_params=_params("parallel"),
        name="nsa_sample",
    )(page_tab.reshape(-1), *([cache_t] * npg), slope_rows, slope_rows.reshape(1, -1), q, gates, cmp_t, kvn, win_t, winn, eg, ovt)


def _overlap_t(nsel):
    n = np.arange(LANES)[None, :] * S_CMP
    j = np.arange(LANES)[:, None] * L_SEL
    ov = (n <= j + L_SEL - 1) & (n + L_CMP - 1 >= j) & (np.arange(LANES)[None, :] < LANES - 1) & (np.arange(LANES)[:, None] < nsel)
    return jnp.asarray(ov.astype(np.float32), dtype=BF16)


def _gate_expand():
    e = np.zeros((LANES, 3 * D_MODEL), np.float32)
    for br in range(3):
        for hh in range(H_B):
            e[br * H_B + hh, br * D_MODEL + hh * DH_B:br * D_MODEL + (hh + 1) * DH_B] = 1.0
    return jnp.asarray(e, dtype=BF16)


def _cmp_weights(pe, w1, w2):
    r_n = L_CMP // S_CMP
    pe_t = jnp.tile(pe.reshape(r_n, S_CMP, 1, DH_B), (1, 1, G_B, 1)).reshape(r_n, S_CMP, GD)
    eye = jnp.eye(G_B, dtype=F32)
    w1r = w1.reshape(r_n, S_CMP, DH_B, CMP_HID)
    w1b = jnp.einsum("rsdh,ab->rsadbh", w1r, jnp.eye(2, dtype=F32)).reshape(r_n, S_CMP * 2 * DH_B, 2 * CMP_HID)
    w2bt = jnp.einsum("hd,gk->kdgh", w2, eye).reshape(GD, G_B * CMP_HID)
    return pe_t, w1b.astype(BF16), w2bt.astype(BF16)


def _rows_minor(x):
    b, r = x.shape[:2]
    return jnp.moveaxis(x, 1, -1).reshape(b, -1, r)


def _rows_major(x_t, feat_shape):
    b, _, r = x_t.shape
    return jnp.moveaxis(x_t.reshape((b,) + feat_shape + (r,)), -1, 1)


def kernel(x_prompt, x_sample, c_prompt, c_sample, state_ret, state_conv, cache_kv, state_win, page_table, w_ada, b_ada, g_mix, g_ffn, w_ffn_in, conv_w, conv_b, w_ffn_out, w_ret_in, w_ret_out, g_kv, w_ada_kv, b_ada_kv, w_kv, pe_ck, pe_cv, w_ck1, w_ck2, w_cv1, w_cv2, w_nsa_in, w_nsa_out, g_final):
    bp, tp, d = x_prompt.shape
    bs, ts, _ = x_sample.shape
    depth = w_ada.shape[0]
    assert depth == 2 and d == D_MODEL and tp % SLC_TILE == 0 and ts == 8
    npg = page_table.shape[1]
    past = npg * PAGE_SIZE
    f2 = 2 * D_FF
    hd = H_B * DH_B

    w_ada_all = jnp.concatenate([w_ada[0], w_ada[1], w_ada_kv], axis=1).astype(BF16)
    b_ada_all = jnp.concatenate([b_ada[0], b_ada[1], b_ada_kv], axis=0).reshape(1, -1)
    w_ret_in_b = w_ret_in[0].astype(BF16)
    w_ret_out_b = w_ret_out[0].astype(BF16)
    w_ffn_in_b = w_ffn_in.astype(BF16)
    w_ffn_out_b = w_ffn_out.astype(BF16)
    w_kv_b = w_kv.astype(BF16)
    w_kv_t = w_kv_b.T
    w_kv_rows, w_kv_win = w_kv_b[:, :4 * GD], w_kv_b[:, 4 * GD:]
    wn = w_nsa_in[0]
    wq_s = wn[:, :hd] * (DH_B ** -0.5)
    w_q_pad = jnp.pad(wq_s.reshape(d, H_B, DH_B), ((0, 0), (0, 0), (0, A_W - DH_B))).reshape(d, H_B * A_W).astype(BF16)
    w_g_grp = jnp.pad(wn[:, hd:].reshape(d, G_B, HPG, 3).transpose(0, 1, 3, 2).reshape(d, G_B, 3 * HPG),
                      ((0, 0), (0, 0), (0, LANES - 3 * HPG))).reshape(d, G_B * LANES).astype(BF16)
    w_q_s = wq_s.reshape(d, G_B, HPG, DH_B).transpose(0, 2, 1, 3).reshape(d, hd).astype(BF16)
    w_g_s = jnp.pad(wn[:, hd:].reshape(d, G_B, HPG, 3).transpose(0, 3, 2, 1).reshape(d, 3 * H_B),
                    ((0, 0), (0, LANES - 3 * H_B))).astype(BF16)
    w_o_b = w_nsa_out[0].astype(BF16)
    w_o_s = w_nsa_out[0].reshape(G_B, HPG, DH_B, d).transpose(1, 0, 2, 3).reshape(hd, d).astype(BF16)
    pe_k, w1_k, w2_k = _cmp_weights(pe_ck, w_ck1, w_ck2)
    pe_v, w1_v, w2_v = _cmp_weights(pe_cv, w_cv1, w_cv2)
    pe_c = jnp.stack([pe_k, pe_v])
    w1_c = jnp.stack([w1_k, w1_v])
    w2t_c = jnp.stack([w2_k, w2_v])
    slopes = jnp.exp2(-8.0 * jnp.arange(1, H_B + 1, dtype=F32) / H_B)
    log_g = jnp.log1p(-jnp.exp2(-5.0 - jnp.arange(H_A, dtype=F32)))
    lg_tab = jnp.broadcast_to(log_g[:, None, None], (H_A, 1, LANES))

    c_all = jnp.concatenate([c_prompt, c_sample], axis=0)
    mod = _ada_call(c_all, w_ada_all, b_ada_all)

    def mods(table, per_row):
        def cols(k):
            return (table, k, per_row)
        return [[cols(l * 6 + k) for k in range(6)] for l in range(depth)] + [[cols(12), cols(13)]]

    def layer0(x, mod_l, s_ret, conv_buf, b, t, tm, act_dt):
        sh1, sc1, ga1, sh2, sc2, ga2 = mod_l
        tm_in = 2 * tm if t % (2 * tm) == 0 else tm
        pr = _nmm_call(x, g_mix[0], sh1, sc1, w_ret_in_b, t, tm_in, 1536, act_dt, name="ret_in")
        og, s_new = _ret_call(pr, s_ret, lg_tab, b, t, act_dt)
        x = _mm_res_call(og, w_ret_out_b, x, ga1, t, tm, name="ret_out")
        x, conv0 = _ffn_call(x, g_ffn[0], sh2, sc2, ga2, w_ffn_in_b[0], conv_w[0], conv_b[0], conv_buf,
                             w_ffn_out_b[0], b, t, tm, name="ffn0")
        return x, s_new, conv0

    def layer1_tail(x, o, w_o, mod_l, conv_buf, b, t, tm):
        _, _, ga1, sh2, sc2, ga2 = mod_l
        x = _mm_res_call(o, w_o, x, ga1, t, tm, name="nsa_out")
        return _ffn_call(x, g_ffn[1], sh2, sc2, ga2, w_ffn_in_b[1], conv_w[1], conv_b[1], conv_buf,
                         w_ffn_out_b[1], b, t, tm, g_final=g_final, name="ffn1")

    tm = 512
    mp = mods(mod[0:bp], False)
    x = x_prompt.reshape(bp * tp, d)
    x, ret_p, conv0_p = layer0(x, mp[0], None, jnp.zeros((bp, CONV_W - 1, f2), F32), bp, tp, tm, BF16)
    sh1, sc1 = mp[1][0], mp[1][1]
    kv_t, win_t = _kvt_call(x, g_kv, mp[2][0], mp[2][1], w_kv_t, bp, tp, tm)
    q = _nmm_call(x, g_mix[1], sh1, sc1, w_q_pad, tp, tm, hd, BF16, name="q_proj")
    gates = _nmm_call(x, g_mix[1], sh1, sc1, w_g_grp, tp, tm, G_B * LANES, F32, act="sigmoid", name="gate_proj")
    assert tp == N_PAGES_ARG * PAGE_SIZE
    ident = jnp.zeros((bp, N_PAGES_ARG), jnp.int32)
    cmp_t = _cmp_call(kv_t, lambda bi, p, pt: (bi, 0, p), ident, pe_c, w1_c, w2t_c)
    o = _nsa_prompt_call(q, gates, cmp_t, kv_t, win_t, _nsa_prompt_consts(tp), bp, tp)
    y_p, conv1_p = layer1_tail(x, o, w_o_b, mp[1], jnp.zeros((bp, CONV_W - 1, f2), F32), bp, tp, tm)
    kv_p = _rows_major(kv_t, (4, G_B, DH_B))
    win_p = _rows_major(win_t[:, :, tp - min(WINDOW, tp):], (2, G_B, DH_B))

    tm = 256
    ms = mods(jnp.repeat(mod[bp:bp + bs], ts, axis=0), True)
    x = x_sample.reshape(bs * ts, d)
    x, ret_s, conv0_s = layer0(x, ms[0], state_ret[0].astype(F32), state_conv[0], bs, ts, tm, F32)
    sh1, sc1 = ms[1][0], ms[1][1]
    kv_rows = _nmm_call(x, g_kv, ms[2][0], ms[2][1], w_kv_rows, ts, tm, 4 * GD, F32, name="kv_rows")
    win_rows = _nmm_call(x, g_kv, ms[2][0], ms[2][1], w_kv_win, ts, tm, 2 * GD, F32, name="win_rows")
    q = _nmm_call(x, g_mix[1], sh1, sc1, w_q_s, ts, tm, hd, F32, name="q_proj")
    gates = _nmm_call(x, g_mix[1], sh1, sc1, w_g_s, ts, tm, LANES, F32, act="sigmoid", name="gate_proj")
    cache_t = _rows_minor(cache_kv)
    state_win_t = _rows_minor(state_win)
    cmp_t = _cmp_call(cache_t, lambda bi, p, pt: (pt[bi * N_PAGES_ARG + p], 0, 0), page_table, pe_c, w1_c, w2t_c)
    slope_rows = jnp.repeat(slopes, ts).reshape(H_B * ts, 1)
    o = _nsa_sample_call(page_table, cache_t, slope_rows, q, gates, cmp_t, kv_rows, state_win_t, win_rows,
                         _gate_expand(), _overlap_t(-(-(past + ts) // L_SEL)), bs, ts)
    y_s, conv1_s = layer1_tail(x, o, w_o_s, ms[1], state_conv[1], bs, ts, tm)
    kv_s = kv_rows.reshape(bs, ts, 4, G_B, DH_B)
    win_new_t = jnp.concatenate([state_win_t, jnp.swapaxes(win_rows.reshape(bs, ts, 2 * GD), 1, 2)], axis=2)
    win_s = _rows_major(win_new_t[:, :, win_new_t.shape[2] - min(WINDOW, win_new_t.shape[2]):], (2, G_B, DH_B))

    return (y_p.reshape(bp, tp, d), y_s.reshape(bs, ts, d), ret_p[None], ret_s[None],
            jnp.stack([conv0_p, conv1_p]), jnp.stack([conv0_s, conv1_s]), kv_p, kv_s, win_p, win_s)
```

```python
import functools

import numpy as np
import jax
import jax.numpy as jnp
from jax import lax
from jax.experimental import pallas as pl
from jax.experimental.pallas import tpu as pltpu

F32 = jnp.float32
BF16 = jnp.bfloat16

D_MODEL = 1024
H_A = 4
DK_A = D_MODEL // H_A
DV_A = 2 * DK_A
RET_CHUNK = 128
H_B = 16
DH_B = D_MODEL // H_B
G_B = 4
HPG = H_B // G_B
GD = G_B * DH_B
L_CMP = 32
S_CMP = 16
L_SEL = 64
N_SEL = 16
WINDOW = 512
CMP_HID = 2 * DH_B
Q_BLOCK = 128
FORCE_BONUS = 1e4
D_FF = 2816
CONV_W = 3
EPS = 1e-6
PAGE_SIZE = 128
NEG = -1e30
BIG = 2.0 ** 100
M_INIT = -(2.0 ** 101)

LANES = 128
SUB = 8
VMEM_LIMIT = 56 * 1024 * 1024
FF_CHUNK = 256
SLC_TILE = 512
SM_ROWS = 32
N_PAGES_ARG = 16

F_POS = DH_B
F_ONE = DH_B + 6
F_SEL = DH_B + 8
A_W = 2 * DH_B

_NT = (((1,), (1,)), ((), ()))


def _sigmoid(x):
    return 1.0 / (1.0 + jnp.exp(-x))


def _silu(x):
    return x * _sigmoid(x)


def _dot(a, b):
    return jnp.dot(a, b, preferred_element_type=F32)


def _dot_nt(a, b):
    return lax.dot_general(a, b, _NT, preferred_element_type=F32)


def _split3(x):
    hi = x.astype(BF16)
    r = x - hi.astype(F32)
    mid = r.astype(BF16)
    lo = (r - mid.astype(F32)).astype(BF16)
    return hi, mid, lo


def _params(*sem):
    return pltpu.CompilerParams(dimension_semantics=sem, vmem_limit_bytes=VMEM_LIMIT)


def _const_spec(shape):
    nd = len(shape)
    return pl.BlockSpec(shape, lambda *a: (0,) * nd, pipeline_mode=pl.Buffered(1))


def _mod_arg(v, m, tm, rows_per_batch):
    table, col, per_row = v
    d = D_MODEL
    if per_row:
        assert table.shape[0] == m and m % tm == 0
        return table.reshape(m // tm, tm, table.shape[1]), (1, tm, d), (lambda i: (i, 0, col))
    assert rows_per_batch % tm == 0
    tpb = rows_per_batch // tm
    return table.reshape(table.shape[0], 1, table.shape[1]), (1, 1, d), (lambda i: (i // tpb, 0, col))


def _ada_kernel(c_ref, w_ref, b_ref, o_ref):
    c = c_ref[...]
    o_ref[...] = _dot(_silu(c).astype(BF16), w_ref[...]) + b_ref[...]


def _ada_call(c, w, b, tn=2048):
    m, d = c.shape
    n = w.shape[1]
    assert n % tn == 0
    return pl.pallas_call(
        _ada_kernel,
        grid=(n // tn,),
        in_specs=[pl.BlockSpec((m, d), lambda j: (0, 0)),
                  pl.BlockSpec((d, tn), lambda j: (0, j)),
                  pl.BlockSpec((1, tn), lambda j: (0, j))],
        out_specs=pl.BlockSpec((m, tn), lambda j: (0, j)),
        out_shape=jax.ShapeDtypeStruct((m, n), F32),
        compiler_params=_params("parallel"),
        name="ada",
    )(c, w, b)


def _norm_mod(x, g, sh, sc):
    y = x * lax.rsqrt(jnp.mean(x * x, axis=-1, keepdims=True) + EPS) * g
    return y * (1.0 + sc) + sh


def _nmm_kernel(x_ref, g_ref, sh_ref, sc_ref, w_ref, o_ref, h_scr, *, act):
    @pl.when(pl.program_id(1) == 0)
    def _():
        h_scr[...] = _norm_mod(x_ref[...], g_ref[...], sh_ref[0], sc_ref[0]).astype(BF16)

    acc = _dot(h_scr[...], w_ref[...])
    if act == "sigmoid":
        acc = _sigmoid(acc)
    o_ref[...] = acc.astype(o_ref.dtype)


def _nmm_call(x, g, sh, sc, w, rows_per_batch, tm, tn, out_dtype, act=None, name="nmm"):
    m, d = x.shape
    n = w.shape[1]
    assert m % tm == 0 and n % tn == 0
    sh_a, mshape, sh_map = _mod_arg(sh, m, tm, rows_per_batch)
    sc_a, _, sc_map = _mod_arg(sc, m, tm, rows_per_batch)
    return pl.pallas_call(
        functools.partial(_nmm_kernel, act=act),
        grid=(m // tm, n // tn),
        in_specs=[pl.BlockSpec((tm, d), lambda i, j: (i, 0)),
                  pl.BlockSpec((1, d), lambda i, j: (0, 0)),
                  pl.BlockSpec(mshape, lambda i, j: sh_map(i)),
                  pl.BlockSpec(mshape, lambda i, j: sc_map(i)),
                  pl.BlockSpec((d, tn), lambda i, j: (0, j))],
        out_specs=pl.BlockSpec((tm, tn), lambda i, j: (i, j)),
        out_shape=jax.ShapeDtypeStruct((m, n), out_dtype),
        scratch_shapes=[pltpu.VMEM((tm, d), BF16)],
        compiler_params=_params("parallel", "arbitrary"),
        name=name,
    )(x, g.reshape(1, d), sh_a, sc_a, w)


def _kvt_kernel(x_ref, g_ref, sh_ref, sc_ref, wt_ref, kv_ref, win_ref):
    h = _norm_mod(x_ref[...], g_ref[...], sh_ref[0], sc_ref[0]).astype(BF16)
    n_kv = kv_ref.shape[1]
    kv_ref[0] = _dot_nt(wt_ref[0:n_kv, :], h)
    win_ref[0] = _dot_nt(wt_ref[n_kv:, :], h)


def _kvt_call(x, g, sh, sc, w_t, b, t, tm):
    m, d = x.shape
    n = w_t.shape[0]
    nt = t // tm
    sh_a, mshape, sh_map = _mod_arg(sh, m, tm, t)
    sc_a, _, sc_map = _mod_arg(sc, m, tm, t)
    return pl.pallas_call(
        _kvt_kernel,
        grid=(b, nt),
        in_specs=[pl.BlockSpec((tm, d), lambda bi, ti: (bi * nt + ti, 0)),
                  pl.BlockSpec((1, d), lambda bi, ti: (0, 0)),
                  pl.BlockSpec(mshape, lambda bi, ti: sh_map(bi * nt + ti)),
                  pl.BlockSpec(mshape, lambda bi, ti: sc_map(bi * nt + ti)),
                  _const_spec((n, d))],
        out_specs=[pl.BlockSpec((1, 4 * GD, tm), lambda bi, ti: (bi, 0, ti)),
                   pl.BlockSpec((1, 2 * GD, tm), lambda bi, ti: (bi, 0, ti))],
        out_shape=[jax.ShapeDtypeStruct((b, 4 * GD, t), F32),
                   jax.ShapeDtypeStruct((b, 2 * GD, t), F32)],
        compiler_params=_params("parallel", "parallel"),
        name="kv_proj_t",
    )(x, g.reshape(1, d), sh_a, sc_a, w_t)


def _mm_res_kernel(a_ref, w_ref, res_ref, ga_ref, o_ref):
    y = _dot(a_ref[...].astype(BF16), w_ref[...])
    o_ref[...] = res_ref[...] + ga_ref[0] * y


def _mm_res_call(a, w, res, gate, rows_per_batch, tm, name="mm_res"):
    m, k = a.shape
    d = w.shape[1]
    ga_a, mshape, mmap = _mod_arg(gate, m, tm, rows_per_batch)
    return pl.pallas_call(
        _mm_res_kernel,
        grid=(m // tm,),
        in_specs=[pl.BlockSpec((tm, k), lambda i: (i, 0)),
                  _const_spec((k, d)),
                  pl.BlockSpec((tm, d), lambda i: (i, 0)),
                  pl.BlockSpec(mshape, lambda i: mmap(i))],
        out_specs=pl.BlockSpec((tm, d), lambda i: (i, 0)),
        out_shape=jax.ShapeDtypeStruct((m, d), F32),
        compiler_params=_params("parallel"),
        name=name,
    )(a, w, res, ga_a)


def _ret_kernel(*refs, c, nc, has_s0):
    if has_s0:
        lg_ref, q_ref, k_ref, v_ref, g_ref, s0_ref, o_ref, so_ref, s_scr = refs
    else:
        lg_ref, q_ref, k_ref, v_ref, g_ref, o_ref, so_ref, s_scr = refs
        s0_ref = None
    n = pl.program_id(1)

    @pl.when(n == 0)
    def _():
        if has_s0:
            s_scr[...] = s0_ref[0]
        else:
            s_scr[...] = jnp.zeros_like(s_scr)

    cp = max(c, RET_CHUNK)

    def padded(a):
        a = a.astype(F32)
        if cp > c:
            a = jnp.concatenate([a, jnp.zeros((cp - c, a.shape[1]), F32)], axis=0)
        return a

    i = lax.broadcasted_iota(jnp.int32, (cp, 1), 0).astype(F32)
    j = lax.broadcasted_iota(jnp.int32, (1, cp), 1).astype(F32)
    diff = i - j
    s_olds = [s_scr[h] for h in range(H_A)]
    s_news = [None] * H_A

    def head_steps(h):
        lg = lg_ref[h][:, 0:1]
        q = padded(q_ref[:, h * DK_A:(h + 1) * DK_A]).astype(BF16)
        kf = padded(k_ref[:, h * DK_A:(h + 1) * DK_A]) * (DK_A ** -0.5)
        v = padded(v_ref[:, h * DV_A:(h + 1) * DV_A]).astype(BF16)
        dmask = jnp.where(diff >= 0, jnp.exp(jnp.maximum(diff, 0.0) * lg), 0.0)
        scores = _dot_nt(q, kf.astype(BF16)) * dmask
        yield
        s_old = s_olds[h]
        o = _dot(scores.astype(BF16), v) + _dot(q, s_old.astype(BF16)) * jnp.exp((i + 1.0) * lg)
        yield
        w = jnp.exp((c - 1.0 - i) * lg)
        kw_t = (kf * w).T.astype(BF16)
        s_news[h] = jnp.exp(c * lg) * s_old + _dot(kw_t, v)
        yield
        of = o * lax.rsqrt(jnp.mean(o * o, axis=-1, keepdims=True) + EPS)
        gate = g_ref[:, h * DV_A:(h + 1) * DV_A].astype(F32)
        o_ref[:, h * DV_A:(h + 1) * DV_A] = (of[0:c] * _silu(gate)).astype(o_ref.dtype)

    _round_robin(*[head_steps(h) for h in range(H_A)])
    for h in range(H_A):
        s_scr[h] = s_news[h]

    @pl.when(n == nc - 1)
    def _():
        so_ref[0] = s_scr[...]


def _ret_call(pr, s0, lg_tab, b, t, out_dtype):
    c = RET_CHUNK if t % RET_CHUNK == 0 else t
    nc = t // c
    has_s0 = s0 is not None
    nq, nv = H_A * DK_A, H_A * DV_A
    assert nv == 2 * nq
    in_specs = [pl.BlockSpec((H_A, 1, LANES), lambda bi, n: (0, 0, 0)),
                pl.BlockSpec((c, nq), lambda bi, n: (bi * nc + n, 0)),
                pl.BlockSpec((c, nq), lambda bi, n: (bi * nc + n, 1)),
                pl.BlockSpec((c, nv), lambda bi, n: (bi * nc + n, 1)),
                pl.BlockSpec((c, nv), lambda bi, n: (bi * nc + n, 2))]
    args = [lg_tab, pr, pr, pr, pr]
    if has_s0:
        in_specs.append(pl.BlockSpec((1, H_A, DK_A, DV_A), lambda bi, n: (bi, 0, 0, 0)))
        args.append(s0)
    return pl.pallas_call(
        functools.partial(_ret_kernel, c=c, nc=nc, has_s0=has_s0),
        grid=(b, nc),
        in_specs=in_specs,
        out_specs=[pl.BlockSpec((c, nv), lambda bi, n: (bi * nc + n, 0)),
                   pl.BlockSpec((1, H_A, DK_A, DV_A), lambda bi, n: (bi, 0, 0, 0))],
        out_shape=[jax.ShapeDtypeStruct((b * t, nv), out_dtype),
                   jax.ShapeDtypeStruct((b, H_A, DK_A, DV_A), F32)],
        scratch_shapes=[pltpu.VMEM((H_A, DK_A, DV_A), F32)],
        compiler_params=_params("parallel", "arbitrary"),
        name="retention",
    )(*args)


def _ffn_kernel(*refs, t_in, nbt, final):
    if final:
        (x_ref, g_ref, sh_ref, sc_ref, ga_ref, win_ref, cw_ref, cb_ref, cbuf_ref, wout_ref, gf_ref,
         o_ref, nc_ref, h_scr, carry_scr, act_scr, shift_scr) = refs
    else:
        (x_ref, g_ref, sh_ref, sc_ref, ga_ref, win_ref, cw_ref, cb_ref, cbuf_ref, wout_ref,
         o_ref, nc_ref, h_scr, carry_scr, act_scr, shift_scr) = refs
        gf_ref = None
    tm = x_ref.shape[0]

    @pl.when(pl.program_id(1) == 0)
    def _():
        carry_scr[...] = cbuf_ref[...]

    x = x_ref[...]
    h_scr[...] = _norm_mod(x, g_ref[...], sh_ref[0], sc_ref[0]).astype(BF16)
    cwd = FF_CHUNK
    tpos = lax.broadcasted_iota(jnp.int32, (nbt, t_in, cwd), 1)
    is0 = tpos == 0
    is1 = tpos == 1

    def conv_rolled(off):
        u = _dot(h_scr[...], win_ref[:, off:off + cwd]).reshape(nbt, t_in, cwd)
        prev = carry_scr[:, :, off:off + cwd]
        p0 = prev[:, 0:1, :]
        p1 = prev[:, 1:2, :]
        r1 = pltpu.roll(u, 1, axis=1)
        r2 = pltpu.roll(u, 2, axis=1)
        um1 = jnp.where(is0, p1, r1)
        um2 = jnp.where(is0, p0, jnp.where(is1, p1, r2))
        newc = r2[:, 0:2, :]
        carry_scr[:, :, off:off + cwd] = newc
        nc_ref[:, :, off:off + cwd] = newc
        cw = cw_ref[:, off:off + cwd]
        z = cb_ref[:, off:off + cwd] + cw[0:1] * um2
        z = z + cw[1:2] * um1
        z = z + cw[2:3] * u
        return z.reshape(tm, cwd)

    def conv_shifted(off, slot):
        buf = shift_scr.at[slot]
        buf[SUB:SUB + tm, :] = _dot(h_scr[...], win_ref[:, off:off + cwd])
        buf[SUB - 2:SUB, :] = carry_scr[0, :, off:off + cwd]
        newc = buf[SUB + tm - 2:SUB + tm, :]
        carry_scr[0, :, off:off + cwd] = newc
        nc_ref[0, :, off:off + cwd] = newc
        cw = cw_ref[:, off:off + cwd]
        z = cb_ref[:, off:off + cwd] + cw[0:1] * buf[SUB - 2:SUB - 2 + tm, :]
        z = z + cw[1:2] * buf[SUB - 1:SUB - 1 + tm, :]
        z = z + cw[2:3] * buf[SUB:SUB + tm, :]
        return z

    n_slots = shift_scr.shape[0]
    for ch in range(D_FF // cwd):
        if nbt == 1:
            za = conv_shifted(ch * cwd, (2 * ch) % n_slots)
            zg = conv_shifted(D_FF + ch * cwd, (2 * ch + 1) % n_slots)
        else:
            za = conv_rolled(ch * cwd)
            zg = conv_rolled(D_FF + ch * cwd)
        act_scr[:, ch * cwd:(ch + 1) * cwd] = (_silu(zg) * za).astype(BF16)
    xn = x + ga_ref[0] * _dot(act_scr[...], wout_ref[...])
    if final:
        xn = xn * lax.rsqrt(jnp.mean(xn * xn, axis=-1, keepdims=True) + EPS) * gf_ref[...]
    o_ref[...] = xn


def _ffn_call(x, g, sh, sc, ga, w_in, cw, cb, cbuf, w_out, b, t, tm, g_final=None, name="ffn"):
    m, d = x.shape
    f2 = w_in.shape[1]
    if t % tm == 0:
        t_in, nbt, nb_tiles, nt = tm, 1, b, t // tm
    else:
        assert tm % t == 0 and m % tm == 0
        t_in, nbt, nb_tiles, nt = t, tm // t, m // tm, 1
    sh_a, mshape, sh_map = _mod_arg(sh, m, tm, t)
    sc_a, _, sc_map = _mod_arg(sc, m, tm, t)
    ga_a, _, ga_map = _mod_arg(ga, m, tm, t)
    final = g_final is not None
    in_specs = [pl.BlockSpec((tm, d), lambda bi, ti: (bi * nt + ti, 0)),
                pl.BlockSpec((1, d), lambda bi, ti: (0, 0)),
                pl.BlockSpec(mshape, lambda bi, ti: sh_map(bi * nt + ti)),
                pl.BlockSpec(mshape, lambda bi, ti: sc_map(bi * nt + ti)),
                pl.BlockSpec(mshape, lambda bi, ti: ga_map(bi * nt + ti)),
                _const_spec((d, f2)),
                _const_spec((CONV_W, f2)),
                _const_spec((1, f2)),
                pl.BlockSpec((nbt, CONV_W - 1, f2), lambda bi, ti: (bi, 0, 0)),
                _const_spec((D_FF, d))]
    args = [x, g.reshape(1, d), sh_a, sc_a, ga_a, w_in, cw, cb.reshape(1, f2), cbuf, w_out]
    if final:
        in_specs.append(pl.BlockSpec((1, d), lambda bi, ti: (0, 0)))
        args.append(g_final.reshape(1, d))
    return pl.pallas_call(
        functools.partial(_ffn_kernel, t_in=t_in, nbt=nbt, final=final),
        grid=(nb_tiles, nt),
        in_specs=in_specs,
        out_specs=[pl.BlockSpec((tm, d), lambda bi, ti: (bi * nt + ti, 0)),
                   pl.BlockSpec((nbt, CONV_W - 1, f2), lambda bi, ti: (bi, 0, 0))],
        out_shape=[jax.ShapeDtypeStruct((m, d), F32),
                   jax.ShapeDtypeStruct((b, CONV_W - 1, f2), F32)],
        scratch_shapes=[pltpu.VMEM((tm, d), BF16),
                        pltpu.VMEM((nbt, CONV_W - 1, f2), F32),
                        pltpu.VMEM((tm, D_FF), BF16),
                        pltpu.VMEM((4, SUB + tm, FF_CHUNK), F32)],
        compiler_params=_params("parallel", "arbitrary"),
        name=name,
    )(*args)


def _cmp_kernel(pt_ref, *refs):
    pages = refs[:N_PAGES_ARG]
    pe_ref, w1_ref, w2t_ref, o_ref = refs[N_PAGES_ARG:]
    npc = PAGE_SIZE // S_CMP
    r_n = L_CMP // S_CMP
    def kind_steps(kind):
        rows = jnp.concatenate([pg[0, kind * GD:(kind + 1) * GD, :].T for pg in pages], axis=0)
        yield
        rows = jnp.swapaxes(rows.reshape(len(pages) * npc, S_CMP, GD), 0, 1)
        pieces = [rows[s] for s in range(S_CMP)]
        yield
        z = None
        for r in range(r_n):
            ys = []
            for gp in range(G_B // 2):
                ls = slice(gp * LANES, (gp + 1) * LANES)
                a = jnp.concatenate([(pieces[s][:, ls] + pe_ref[kind, r, s:s + 1, ls]).astype(BF16)
                                     for s in range(S_CMP)], axis=1)
                ys.append(_dot(a, w1_ref[kind, r]))
                yield
            y = jnp.concatenate(ys, axis=1)
            if r > 0:
                y = pltpu.roll(y, y.shape[0] - r, axis=0)
            z = y if z is None else z + y
        o_ref[0, kind] = _dot_nt(w2t_ref[kind], _silu(z).astype(BF16)).astype(o_ref.dtype)

    _round_robin(kind_steps(0), kind_steps(1))


def _cmp_call(rows_t, page_map, page_tab, pe, w1, w2t):
    b, npg = page_tab.shape
    assert npg == N_PAGES_ARG
    n_pieces = npg * PAGE_SIZE // S_CMP
    page_specs = [pl.BlockSpec((1, 2 * GD, PAGE_SIZE), functools.partial(lambda bi, pt, p: page_map(bi, p, pt), p=p))
                  for p in range(npg)]
    grid_spec = pltpu.PrefetchScalarGridSpec(
        num_scalar_prefetch=1,
        grid=(b,),
        in_specs=page_specs + [
            pl.BlockSpec(pe.shape, lambda bi, pt: (0, 0, 0, 0), pipeline_mode=pl.Buffered(1)),
            pl.BlockSpec(w1.shape, lambda bi, pt: (0, 0, 0, 0), pipeline_mode=pl.Buffered(1)),
            pl.BlockSpec(w2t.shape, lambda bi, pt: (0, 0, 0), pipeline_mode=pl.Buffered(1))],
        out_specs=pl.BlockSpec((1, 2, GD, n_pieces), lambda bi, pt: (bi, 0, 0, 0)),
    )
    return pl.pallas_call(
        _cmp_kernel,
        grid_spec=grid_spec,
        out_shape=jax.ShapeDtypeStruct((b, 2, GD, n_pieces), BF16),
        compiler_params=_params("parallel"),
        name="compress",
    )(page_tab.reshape(-1), *([rows_t] * npg), pe, w1, w2t)


def _masked_softmax(s, ok):
    sm = jnp.where(ok, s, NEG)
    m = jnp.max(sm, axis=-1, keepdims=True)
    e = jnp.where(ok, jnp.exp(sm - m), 0.0)
    return e / jnp.maximum(jnp.sum(e, axis=-1, keepdims=True), 1e-30)


def _topk_mask_steps(imp_t, cur, nsel, ncol):
    rows = imp_t.shape[0]
    jj = lax.broadcasted_iota(jnp.int32, (rows, ncol), 0)
    forced = (jj == 0) | (jj == cur) | (jj == cur - 1)
    score = jnp.where(jj <= cur, imp_t + FORCE_BONUS * forced.astype(F32), NEG)
    rank = jnp.zeros((rows, ncol), F32)
    for i in range(nsel):
        row = score[i:i + 1, :]
        lower = jnp.where(jj > i, 1.0, 0.0)
        rank = rank + jnp.where(row > score, 1.0, jnp.where(row == score, lower, 0.0))
        if i % 8 == 7 and i + 1 < nsel:
            yield None
    yield jnp.where((rank < float(min(N_SEL, nsel))) & (jj < nsel), 1.0, 0.0)


def _topk_mask_t(imp_t, cur, nsel, ncol):
    out = None
    for out in _topk_mask_steps(imp_t, cur, nsel, ncol):
        pass
    return out


def _round_robin(*streams):
    live = list(streams)
    while live:
        for s in list(live):
            try:
                next(s)
            except StopIteration:
                live.remove(s)


def _eye_bf16(n):
    return jnp.where(lax.broadcasted_iota(jnp.int32, (n, n), 0) == lax.broadcasted_iota(jnp.int32, (n, n), 1),
                     1.0, 0.0).astype(BF16)


def _pad_rows(a, rows):
    if a.shape[0] == rows:
        return a
    return jnp.concatenate([a, jnp.zeros((rows - a.shape[0],) + a.shape[1:], a.dtype)], axis=0)


def _online_step(sm, v, m_old, l_old, acc_old, v_transposed):
    m_new = jnp.maximum(m_old, jnp.max(sm, axis=-1, keepdims=True))
    alpha = jnp.exp(m_old - m_new)
    e = jnp.where(sm > 0.5 * NEG, jnp.exp(sm - m_new), 0.0)
    l_new = alpha * l_old + jnp.sum(e, axis=-1, keepdims=True)
    pv = _dot_nt(e.astype(BF16), v) if v_transposed else _dot(e.astype(BF16), v)
    return m_new, l_new, alpha * acc_old + pv


def _block_importance_t(ps_sum, ovt):
    hi, mid, lo = _split3(ps_sum)
    return _dot_nt(ovt, hi) + _dot_nt(ovt, mid) + _dot_nt(ovt, lo)


def _nsa_prompt_kernel(q_ref, gate_ref, cmp_ref, kv_ref, win_ref, featc_ref, fslc_ref, fwin_ref, fcmp_ref,
                       ovt_ref, band_ref, o_ref,
                       kb_slc, vt_slc, kb_win, vt_win, kb_cmp, a_scr, s_scr, e_scr, m_scr, acc_scr, gb_scr, *, t_len):
    qb = pl.program_id(1)
    s0 = qb * Q_BLOCK
    nsel = t_len // L_SEL
    n_st = t_len // SLC_TILE
    n_wt = WINDOW // Q_BLOCK
    n_wtiles = kb_win.shape[1]
    rows_g = HPG * Q_BLOCK

    def ones_rows(width):
        r = lax.broadcasted_iota(jnp.int32, (A_W - DH_B, width), 0)
        return jnp.where(r == 0, 1.0, 0.0).astype(BF16)

    @pl.when(qb == 0)
    def _():
        for g in range(G_B):
            for kt in range(n_st):
                cs = slice(kt * SLC_TILE, (kt + 1) * SLC_TILE)
                kb_slc[g, kt, 0:DH_B, :] = kv_ref[0, g * DH_B:(g + 1) * DH_B, cs].astype(BF16)
                kb_slc[g, kt, DH_B:A_W, :] = fslc_ref[:, cs]
            for wt in range(n_wtiles):
                cs = slice((wt - n_wt) * Q_BLOCK, (wt - n_wt + 1) * Q_BLOCK)
                if wt < n_wt:
                    kb_win[g, wt, 0:DH_B, :] = jnp.zeros((DH_B, Q_BLOCK), BF16)
                else:
                    kb_win[g, wt, 0:DH_B, :] = win_ref[0, g * DH_B:(g + 1) * DH_B, cs].astype(BF16)
                kb_win[g, wt, DH_B:A_W, :] = fwin_ref[:, wt * Q_BLOCK:(wt + 1) * Q_BLOCK]
            kb_cmp[g, 0:DH_B, :] = cmp_ref[0, 0, g * DH_B:(g + 1) * DH_B, :]
            kb_cmp[g, DH_B:A_W, :] = fcmp_ref[...]
            for kt in range(n_st):
                cs = slice(kt * SLC_TILE, (kt + 1) * SLC_TILE)
                vt_slc[g, kt, 0:DH_B, :] = kv_ref[0, GD + g * DH_B:GD + (g + 1) * DH_B, cs].astype(BF16)
                vt_slc[g, kt, DH_B:A_W, :] = ones_rows(SLC_TILE)
            for wt in range(n_wtiles):
                cs = slice((wt - n_wt) * Q_BLOCK, (wt - n_wt + 1) * Q_BLOCK)
                if wt < n_wt:
                    vt_win[g, wt, 0:DH_B, :] = jnp.zeros((DH_B, Q_BLOCK), BF16)
                else:
                    vt_win[g, wt, 0:DH_B, :] = win_ref[0, GD + g * DH_B:GD + (g + 1) * DH_B, cs].astype(BF16)
                vt_win[g, wt, DH_B:A_W, :] = ones_rows(Q_BLOCK)

    lane = lax.broadcasted_iota(jnp.int32, (1, A_W), 1)
    ags = [jnp.concatenate([jnp.where(lane < DH_B, q_ref[:, (g * HPG + hl) * A_W:(g * HPG + hl + 1) * A_W],
                                      featc_ref[g * HPG + hl:g * HPG + hl + 1, :].astype(BF16))
                            for hl in range(HPG)], axis=0) for g in range(G_B)]

    ovt = ovt_ref[...]
    eye = _eye_bf16(Q_BLOCK)
    t_loc = lax.broadcasted_iota(jnp.int32, (Q_BLOCK, 1), 0)
    tpos = (s0 + t_loc).astype(F32)
    n_f = lax.broadcasted_iota(jnp.int32, (1, Q_BLOCK), 1).astype(F32)
    ok_c = (tpos - (n_f * S_CMP + (L_CMP - 1.0))) >= 0.0
    nrow = ((nsel + 7) // 8) * 8
    groups = range(G_B)

    def softmax_steps(g, width, bias_fn, out):
        m_old_all = m_scr[g]
        m_news, alphas = [], []
        for rb in range(rows_g // SM_ROWS):
            rows = slice(rb * SM_ROWS, (rb + 1) * SM_ROWS)
            s = s_scr[g, rows, 0:width]
            if bias_fn is not None:
                s = bias_fn(s, (rb * SM_ROWS) % Q_BLOCK)
            m_old = m_old_all[rows]
            m_new = jnp.maximum(m_old, jnp.max(s, axis=-1, keepdims=True))
            e = jnp.exp(s - jnp.concatenate([m_new] * (width // LANES), axis=1))
            e_scr[g, rows, 0:width] = e.astype(BF16)
            m_news.append(m_new)
            alphas.append(jnp.exp(m_old - m_new))
            if rb % 2 == 1:
                yield
        m_scr[g] = jnp.concatenate(m_news, axis=0)
        out.append(jnp.concatenate(alphas, axis=0))

    def reset_stats():
        m_scr[...] = jnp.full_like(m_scr, M_INIT)
        acc_scr[...] = jnp.zeros_like(acc_scr)

    def attend_steps(width, a_fn, key_fn, val_fn, bias_fn):
        s_scr[0, :, 0:width] = _dot(a_fn(0), key_fn(0))
        yield
        for g in groups:
            if g + 1 < G_B:
                s_scr[g + 1, :, 0:width] = _dot(a_fn(g + 1), key_fn(g + 1))
            alpha = []
            yield from softmax_steps(g, width, bias_fn, alpha)
            acc_scr[g] = alpha[0] * acc_scr[g] + _dot_nt(e_scr[g, :, 0:width], val_fn(g))
            yield

    def attend(*args):
        for _ in attend_steps(*args):
            pass

    def branch_out(g):
        acc = acc_scr[g]
        return acc[:, 0:DH_B] / jnp.maximum(acc[:, DH_B:DH_B + 1], 1e-30)

    reset_stats()
    win_stream = attend_steps(WINDOW + Q_BLOCK,
                              lambda g: ags[g],
                              lambda g: jnp.concatenate([kb_win[g, qb + w] for w in range(n_wt + 1)], axis=1),
                              lambda g: jnp.concatenate([vt_win[g, qb + w] for w in range(n_wt + 1)], axis=1),
                              lambda s, t0: s + band_ref[t0:t0 + SM_ROWS, :])

    cur_t = (s0 + lax.broadcasted_iota(jnp.int32, (nrow, Q_BLOCK), 1)) // L_SEL
    o_cmp, ags_sel = [None] * G_B, [None] * G_B

    def sel_stream(g):
        s_c = _dot(ags[g], kb_cmp[g]).reshape(HPG, Q_BLOCK, Q_BLOCK)
        yield
        p3 = _masked_softmax(s_c, ok_c[None])
        yield
        o_cmp[g] = _dot_nt(p3.reshape(rows_g, Q_BLOCK).astype(BF16), cmp_ref[0, 1, g * DH_B:(g + 1) * DH_B, :])
        imp_t = _block_importance_t((p3[0] + p3[1]) + (p3[2] + p3[3]), ovt)
        yield
        sel_t = None
        for sel_t in _topk_mask_steps(imp_t[0:nrow], cur_t, nsel, Q_BLOCK):
            yield
        nsel_pad = jnp.concatenate([jnp.zeros((F_SEL, Q_BLOCK), F32), 1.0 - sel_t,
                                    jnp.zeros((A_W - F_SEL - nrow, Q_BLOCK), F32)], axis=0).astype(BF16)
        nsl = _dot_nt(eye, nsel_pad).astype(BF16)
        ags_sel[g] = ags[g] + jnp.concatenate([nsl] * HPG, axis=0)
        yield

    def gate_stream():
        for g in groups:
            gt = gate_ref[:, g * LANES:(g + 1) * LANES]
            for c in range(3 * HPG):
                gb_scr[g * 3 * HPG + c] = jnp.broadcast_to(gt[:, c:c + 1], (Q_BLOCK, LANES))
                if c % 2 == 1:
                    yield

    _round_robin(win_stream, *[sel_stream(g) for g in groups], gate_stream())
    o_win = [branch_out(g) for g in groups]
    for g in groups:
        a_scr[g] = ags_sel[g]

    def causal_bias(kti):
        def fn(s, t0):
            u = lax.broadcasted_iota(jnp.int32, s.shape, 1)
            t = t0 + lax.broadcasted_iota(jnp.int32, s.shape, 0)
            return jnp.where(u <= (s0 - kti * SLC_TILE) + t, s, -BIG)
        return fn

    def slc_tile(kti, diag):
        attend(SLC_TILE, lambda g: a_scr[g], lambda g: kb_slc[g, kti], lambda g: vt_slc[g, kti],
               causal_bias(kti) if diag else None)

    reset_stats()
    kd = s0 // SLC_TILE

    def slc_full(kti, c):
        slc_tile(kti, False)
        return c

    lax.fori_loop(0, kd, slc_full, 0)
    slc_tile(kd, True)
    o_slc = [branch_out(g) for g in groups]

    pieces = []
    for g in groups:
        for hl in range(HPG):
            r = slice(hl * Q_BLOCK, (hl + 1) * Q_BLOCK)
            gb = [gb_scr[g * 3 * HPG + br * HPG + hl, :, 0:DH_B] for br in range(3)]
            comb = gb[0] * o_cmp[g][r]
            comb = comb + gb[1] * o_slc[g][r]
            comb = comb + gb[2] * o_win[g][r]
            pieces.append(comb)
    o_ref[...] = jnp.concatenate(pieces, axis=1).astype(o_ref.dtype)


def _nsa_prompt_call(q, gates, cmp_t, kv_t, win_t, consts, b, t):
    featc, fslc, fwin, fcmp, ovt, band = consts
    nqb = t // Q_BLOCK
    n_wtiles = (WINDOW + t) // Q_BLOCK
    return pl.pallas_call(
        functools.partial(_nsa_prompt_kernel, t_len=t),
        grid=(b, nqb),
        in_specs=[pl.BlockSpec((Q_BLOCK, H_B * A_W), lambda bi, qi: (bi * nqb + qi, 0)),
                  pl.BlockSpec((Q_BLOCK, G_B * LANES), lambda bi, qi: (bi * nqb + qi, 0)),
                  pl.BlockSpec((1, 2, GD, cmp_t.shape[3]), lambda bi, qi: (bi, 0, 0, 0)),
                  pl.BlockSpec((1, 2 * GD, t), lambda bi, qi: (bi, 1, 0)),
                  pl.BlockSpec((1, 2 * GD, t), lambda bi, qi: (bi, 0, 0)),
                  _const_spec(featc.shape), _const_spec(fslc.shape), _const_spec(fwin.shape),
                  _const_spec(fcmp.shape), _const_spec(ovt.shape), _const_spec(band.shape)],
        out_specs=pl.BlockSpec((Q_BLOCK, D_MODEL), lambda bi, qi: (bi * nqb + qi, 0)),
        out_shape=jax.ShapeDtypeStruct((b * t, D_MODEL), BF16),
        scratch_shapes=[pltpu.VMEM((G_B, t // SLC_TILE, A_W, SLC_TILE), BF16),
                        pltpu.VMEM((G_B, t // SLC_TILE, A_W, SLC_TILE), BF16),
                        pltpu.VMEM((G_B, n_wtiles, A_W, Q_BLOCK), BF16),
                        pltpu.VMEM((G_B, n_wtiles, A_W, Q_BLOCK), BF16),
                        pltpu.VMEM((G_B, A_W, Q_BLOCK), BF16),
                        pltpu.VMEM((G_B, HPG * Q_BLOCK, A_W), BF16),
                        pltpu.VMEM((G_B, HPG * Q_BLOCK, WINDOW + Q_BLOCK), F32),
                        pltpu.VMEM((G_B, HPG * Q_BLOCK, WINDOW + Q_BLOCK), BF16),
                        pltpu.VMEM((G_B, HPG * Q_BLOCK, LANES), F32),
                        pltpu.VMEM((G_B, HPG * Q_BLOCK, A_W), F32),
                        pltpu.VMEM((3 * H_B, Q_BLOCK, LANES), F32)],
        compiler_params=_params("parallel", "arbitrary"),
        name="nsa_prompt",
    )(q, gates, cmp_t, kv_t, win_t, featc, fslc, fwin, fcmp, ovt, band)


def _nsa_prompt_consts(t):
    slopes = jnp.exp2(-8.0 * jnp.arange(1, H_B + 1, dtype=F32) / H_B)
    featc = jnp.zeros((H_B, A_W), F32).at[:, F_ONE].set(1.0)
    for k, term in enumerate(_split3(slopes)):
        featc = featc.at[:, F_POS + 2 * k].set(term.astype(F32)).at[:, F_POS + 2 * k + 1].set(term.astype(F32))

    def pos_rows(pos):
        f = np.zeros((A_W - DH_B, pos.shape[0]), np.float32)
        hi = (pos // 256) * 256
        lo = pos % 256
        for k in range(3):
            f[F_POS - DH_B + 2 * k] = hi
            f[F_POS - DH_B + 2 * k + 1] = lo
        return f

    nsel = t // L_SEL
    spos = np.arange(t)
    fslc = pos_rows(spos)
    for j in range(nsel):
        fslc[F_SEL - DH_B + j] = np.where(spos // L_SEL == j, -BIG, 0.0)
    wcol = np.arange(WINDOW + t)
    fwin = pos_rows(wcol)
    fwin[F_ONE - DH_B] = np.where(wcol < WINDOW, -BIG, 0.0)
    fcmp = pos_rows(np.arange(Q_BLOCK) * S_CMP + (L_CMP - 1))
    u = np.arange(WINDOW + Q_BLOCK)[None, :]
    tl = np.arange(Q_BLOCK)[:, None]
    band = np.where((u >= tl) & (u <= tl + WINDOW), 0.0, -BIG).astype(np.float32)
    return (featc, jnp.asarray(fslc, dtype=BF16), jnp.asarray(fwin, dtype=BF16),
            jnp.asarray(fcmp, dtype=BF16), _overlap_t(nsel), jnp.asarray(band))


def _nsa_sample_kernel(pt_ref, *refs, past, ts):
    pages = refs[:N_PAGES_ARG]
    (slope_ref, slopel_ref, q_ref, gate_ref, cmp_ref, kvn_ref, win_ref, winn_ref, eg_ref, ovt_ref, o_ref) = refs[N_PAGES_ARG:]
    rows = H_B * ts
    nsel = -(-(past + ts) // L_SEL)
    lane_g = lax.broadcasted_iota(jnp.int32, (1, GD), 1) // DH_B
    q = q_ref[...]
    qq = jnp.concatenate([jnp.where(lane_g == g, q[:, h * GD:(h + 1) * GD], 0.0)
                          for g in range(G_B) for h in range(HPG)], axis=0).astype(BF16)
    slope = slope_ref[...]
    t_row = lax.broadcasted_iota(jnp.int32, (rows, 1), 0) % ts
    tpos = (past + t_row).astype(F32)
    kc_t = cmp_ref[0, 0]
    vc_t = cmp_ref[0, 1]
    ncb = kc_t.shape[1]
    res = {}
    n_f = lax.broadcasted_iota(jnp.int32, (1, ncb), 1).astype(F32)
    dist_c = tpos - (n_f * S_CMP + (L_CMP - 1.0))
    nrow = ((nsel + 1 + 7) // 8) * 8
    cur_t = (past + lax.broadcasted_iota(jnp.int32, (nrow, rows), 1) % ts) // L_SEL

    def select_steps():
        p = _masked_softmax(_dot(qq, kc_t) - slope * dist_c, dist_c >= 0.0)
        yield
        res["o_cmp"] = _dot_nt(p.astype(BF16), vc_t)
        ps = []
        for g in range(G_B):
            blk = [p[(g * HPG + h) * ts:(g * HPG + h + 1) * ts] for h in range(HPG)]
            ps.extend([(blk[0] + blk[1]) + (blk[2] + blk[3])] * HPG)
        psum = jnp.concatenate(ps, axis=0)
        imp_t = _block_importance_t(psum, ovt_ref[...])
        yield
        for sel in _topk_mask_steps(imp_t[0:nrow], cur_t, nsel, rows):
            res["sel"] = sel
            yield

    tpos_l = (past + lax.broadcasted_iota(jnp.int32, (1, rows), 1) % ts).astype(F32)
    slope_l = slopel_ref[...]
    key_i = lax.broadcasted_iota(jnp.int32, (PAGE_SIZE, rows), 0)
    bpp = PAGE_SIZE // L_SEL

    def rows_t(new_ref, lo):
        return _pad_rows(new_ref[:, lo:lo + GD], PAGE_SIZE).T.astype(BF16)

    def attend_steps(name, kt_all, vt_all, tiles, before_masks=None):
        s_all = lax.dot_general(kt_all, qq, (((0,), (1,)), ((), ())), preferred_element_type=F32)
        yield
        if before_masks is not None:
            yield from before_masks
        sms = []
        for i, (k0, n_real, blk) in enumerate(tiles):
            s = s_all[i * PAGE_SIZE:(i + 1) * PAGE_SIZE]
            dist = tpos_l - (k0 + key_i).astype(F32)
            dist_m = dist if n_real == PAGE_SIZE else jnp.where(key_i < n_real, dist, -1.0)
            if blk is not None:
                sel_t = res["sel"]
                mask = jnp.concatenate([jnp.broadcast_to(sel_t[blk + a:blk + a + 1], (L_SEL, rows)) for a in range(bpp)],
                                       axis=0)
                ok = jnp.where(dist_m >= 0.0, mask, 0.0) > 0.5
            else:
                ok = jnp.abs(dist_m - 0.5 * WINDOW) <= 0.5 * WINDOW
            sms.append(jnp.where(ok, s - slope_l * dist, NEG))
            if i % 2 == 1:
                yield
        m = functools.reduce(jnp.maximum, [jnp.max(x, axis=0, keepdims=True) for x in sms])
        es = []
        for i, x in enumerate(sms):
            es.append(jnp.exp(x - m))
            if i % 4 == 3:
                yield
        l = functools.reduce(lambda a, b: a + b, [jnp.sum(e, axis=0, keepdims=True) for e in es])
        o_t = _dot(vt_all, jnp.concatenate(es, axis=0).astype(BF16))
        res[name] = (o_t / jnp.maximum(l, 1e-30)).T

    kt_slc = jnp.concatenate([pg[0, 0:GD, :].astype(BF16) for pg in pages] + [rows_t(kvn_ref, 0)], axis=1)
    vt_slc = jnp.concatenate([pg[0, GD:2 * GD, :].astype(BF16) for pg in pages] + [rows_t(kvn_ref, GD)], axis=1)
    wbuf = win_ref.shape[2]
    kt_win = jnp.concatenate([win_ref[0, 0:GD, :].astype(BF16), rows_t(winn_ref, 0)], axis=1)
    vt_win = jnp.concatenate([win_ref[0, GD:2 * GD, :].astype(BF16), rows_t(winn_ref, GD)], axis=1)
    _round_robin(
        attend_steps("o_slc", kt_slc, vt_slc,
                     [(pi * PAGE_SIZE, PAGE_SIZE, pi * bpp) for pi in range(len(pages))] + [(past, ts, past // L_SEL)],
                     before_masks=select_steps()),
        attend_steps("o_win", kt_win, vt_win,
                     [(past - wbuf + wi * PAGE_SIZE, PAGE_SIZE, None) for wi in range(wbuf // PAGE_SIZE)]
                     + [(past, ts, None)]))
    o_cmp, o_slc, o_win = res["o_cmp"], res["o_slc"], res["o_win"]

    hi, mid, lo = _split3(gate_ref[...])
    eg = eg_ref[...]
    gexp = _dot(hi, eg) + _dot(mid, eg) + _dot(lo, eg)

    def gate_rows(br):
        return jnp.concatenate([gexp[:, br * D_MODEL + h * GD:br * D_MODEL + (h + 1) * GD]
                                for _ in range(G_B) for h in range(HPG)], axis=0)

    comb = gate_rows(0) * o_cmp + gate_rows(1) * o_slc + gate_rows(2) * o_win
    outs = []
    for h in range(HPG):
        acc = None
        for g in range(G_B):
            piece = jnp.where(lane_g == g, comb[(g * HPG + h) * ts:(g * HPG + h + 1) * ts], 0.0)
            acc = piece if acc is None else acc + piece
        outs.append(acc)
    o_ref[...] = jnp.concatenate(outs, axis=1).astype(o_ref.dtype)


def _nsa_sample_call(page_tab, cache_t, slope_rows, q, gates, cmp_t, kvn, win_t, winn, eg, ovt, b, ts):
    npg = page_tab.shape[1]
    assert npg == N_PAGES_ARG
    past = npg * PAGE_SIZE
    page_specs = [pl.BlockSpec((1, 2 * GD, PAGE_SIZE), functools.partial(lambda bi, pt, p: (pt[bi * N_PAGES_ARG + p], 1, 0), p=p))
                  for p in range(npg)]
    wbuf = win_t.shape[2]
    grid_spec = pltpu.PrefetchScalarGridSpec(
        num_scalar_prefetch=1,
        grid=(b,),
        in_specs=page_specs + [
            pl.BlockSpec(slope_rows.shape, lambda bi, pt: (0, 0), pipeline_mode=pl.Buffered(1)),
            pl.BlockSpec((1, slope_rows.shape[0]), lambda bi, pt: (0, 0), pipeline_mode=pl.Buffered(1)),
            pl.BlockSpec((ts, D_MODEL), lambda bi, pt: (bi, 0)),
            pl.BlockSpec((ts, LANES), lambda bi, pt: (bi, 0)),
            pl.BlockSpec((1, 2, GD, cmp_t.shape[3]), lambda bi, pt: (bi, 0, 0, 0)),
            pl.BlockSpec((ts, 2 * GD), lambda bi, pt: (bi, 1)),
            pl.BlockSpec((1, 2 * GD, wbuf), lambda bi, pt: (bi, 0, 0)),
            pl.BlockSpec((ts, 2 * GD), lambda bi, pt: (bi, 0)),
            pl.BlockSpec(eg.shape, lambda bi, pt: (0, 0), pipeline_mode=pl.Buffered(1)),
            pl.BlockSpec(ovt.shape, lambda bi, pt: (0, 0), pipeline_mode=pl.Buffered(1))],
        out_specs=pl.BlockSpec((ts, D_MODEL), lambda bi, pt: (bi, 0)),
    )
    return pl.pallas_call(
        functools.partial(_nsa_sample_kernel, past=past, ts=ts),
        grid_spec=grid_spec,
        out_shape=jax.ShapeDtypeStruct((b * ts, D_MODEL), F32),
        compiler_params=_params("parallel"),
        name="nsa_sample",
    )(page_tab.reshape(-1), *([cache_t] * npg), slope_rows, slope_rows.reshape(1, -1), q, gates, cmp_t, kvn, win_t, winn, eg, ovt)


def _overlap_t(nsel):
    n = np.arange(LANES)[None, :] * S_CMP
    j = np.arange(LANES)[:, None] * L_SEL
    ov = (n <= j + L_SEL - 1) & (n + L_CMP - 1 >= j) & (np.arange(LANES)[None, :] < LANES - 1) & (np.arange(LANES)[:, None] < nsel)
    return jnp.asarray(ov.astype(np.float32), dtype=BF16)


def _gate_expand():
    e = np.zeros((LANES, 3 * D_MODEL), np.float32)
    for br in range(3):
        for hh in range(H_B):
            e[br * H_B + hh, br * D_MODEL + hh * DH_B:br * D_MODEL + (hh + 1) * DH_B] = 1.0
    return jnp.asarray(e, dtype=BF16)


def _cmp_weights(pe, w1, w2):
    r_n = L_CMP // S_CMP
    pe_t = jnp.tile(pe.reshape(r_n, S_CMP, 1, DH_B), (1, 1, G_B, 1)).reshape(r_n, S_CMP, GD)
    eye = jnp.eye(G_B, dtype=F32)
    w1r = w1.reshape(r_n, S_CMP, DH_B, CMP_HID)
    w1b = jnp.einsum("rsdh,ab->rsadbh", w1r, jnp.eye(2, dtype=F32)).reshape(r_n, S_CMP * 2 * DH_B, 2 * CMP_HID)
    w2bt = jnp.einsum("hd,gk->kdgh", w2, eye).reshape(GD, G_B * CMP_HID)
    return pe_t, w1b.astype(BF16), w2bt.astype(BF16)


def _rows_minor(x):
    b, r = x.shape[:2]
    return jnp.moveaxis(x, 1, -1).reshape(b, -1, r)


def _rows_major(x_t, feat_shape):
    b, _, r = x_t.shape
    return jnp.moveaxis(x_t.reshape((b,) + feat_shape + (r,)), -1, 1)


def kernel(x_prompt, x_sample, c_prompt, c_sample, state_ret, state_conv, cache_kv, state_win, page_table, w_ada, b_ada, g_mix, g_ffn, w_ffn_in, conv_w, conv_b, w_ffn_out, w_ret_in, w_ret_out, g_kv, w_ada_kv, b_ada_kv, w_kv, pe_ck, pe_cv, w_ck1, w_ck2, w_cv1, w_cv2, w_nsa_in, w_nsa_out, g_final):
    bp, tp, d = x_prompt.shape
    bs, ts, _ = x_sample.shape
    depth = w_ada.shape[0]
    assert depth == 2 and d == D_MODEL and tp % SLC_TILE == 0 and ts == 8
    npg = page_table.shape[1]
    past = npg * PAGE_SIZE
    f2 = 2 * D_FF
    hd = H_B * DH_B

    w_ada_all = jnp.concatenate([w_ada[0], w_ada[1], w_ada_kv], axis=1).astype(BF16)
    b_ada_all = jnp.concatenate([b_ada[0], b_ada[1], b_ada_kv], axis=0).reshape(1, -1)
    w_ret_in_b = w_ret_in[0].astype(BF16)
    w_ret_out_b = w_ret_out[0].astype(BF16)
    w_ffn_in_b = w_ffn_in.astype(BF16)
    w_ffn_out_b = w_ffn_out.astype(BF16)
    w_kv_b = w_kv.astype(BF16)
    w_kv_t = w_kv_b.T
    w_kv_rows, w_kv_win = w_kv_b[:, :4 * GD], w_kv_b[:, 4 * GD:]
    wn = w_nsa_in[0]
    wq_s = wn[:, :hd] * (DH_B ** -0.5)
    w_q_pad = jnp.pad(wq_s.reshape(d, H_B, DH_B), ((0, 0), (0, 0), (0, A_W - DH_B))).reshape(d, H_B * A_W).astype(BF16)
    w_g_grp = jnp.pad(wn[:, hd:].reshape(d, G_B, HPG, 3).transpose(0, 1, 3, 2).reshape(d, G_B, 3 * HPG),
                      ((0, 0), (0, 0), (0, LANES - 3 * HPG))).reshape(d, G_B * LANES).astype(BF16)
    w_q_s = wq_s.reshape(d, G_B, HPG, DH_B).transpose(0, 2, 1, 3).reshape(d, hd).astype(BF16)
    w_g_s = jnp.pad(wn[:, hd:].reshape(d, G_B, HPG, 3).transpose(0, 3, 2, 1).reshape(d, 3 * H_B),
                    ((0, 0), (0, LANES - 3 * H_B))).astype(BF16)
    w_o_b = w_nsa_out[0].astype(BF16)
    w_o_s = w_nsa_out[0].reshape(G_B, HPG, DH_B, d).transpose(1, 0, 2, 3).reshape(hd, d).astype(BF16)
    pe_k, w1_k, w2_k = _cmp_weights(pe_ck, w_ck1, w_ck2)
    pe_v, w1_v, w2_v = _cmp_weights(pe_cv, w_cv1, w_cv2)
    pe_c = jnp.stack([pe_k, pe_v])
    w1_c = jnp.stack([w1_k, w1_v])
    w2t_c = jnp.stack([w2_k, w2_v])
    slopes = jnp.exp2(-8.0 * jnp.arange(1, H_B + 1, dtype=F32) / H_B)
    log_g = jnp.log1p(-jnp.exp2(-5.0 - jnp.arange(H_A, dtype=F32)))
    lg_tab = jnp.broadcast_to(log_g[:, None, None], (H_A, 1, LANES))

    c_all = jnp.concatenate([c_prompt, c_sample], axis=0)
    mod = _ada_call(c_all, w_ada_all, b_ada_all)

    def mods(table, per_row):
        def cols(k):
            return (table, k, per_row)
        return [[cols(l * 6 + k) for k in range(6)] for l in range(depth)] + [[cols(12), cols(13)]]

    def layer0(x, mod_l, s_ret, conv_buf, b, t, tm, act_dt):
        sh1, sc1, ga1, sh2, sc2, ga2 = mod_l
        tm_in = 2 * tm if t % (2 * tm) == 0 else tm
        pr = _nmm_call(x, g_mix[0], sh1, sc1, w_ret_in_b, t, tm_in, 1536, act_dt, name="ret_in")
        og, s_new = _ret_call(pr, s_ret, lg_tab, b, t, act_dt)
        x = _mm_res_call(og, w_ret_out_b, x, ga1, t, tm, name="ret_out")
        x, conv0 = _ffn_call(x, g_ffn[0], sh2, sc2, ga2, w_ffn_in_b[0], conv_w[0], conv_b[0], conv_buf,
                             w_ffn_out_b[0], b, t, tm, name="ffn0")
        return x, s_new, conv0

    def layer1_tail(x, o, w_o, mod_l, conv_buf, b, t, tm):
        _, _, ga1, sh2, sc2, ga2 = mod_l
        x = _mm_res_call(o, w_o, x, ga1, t, tm, name="nsa_out")
        return _ffn_call(x, g_ffn[1], sh2, sc2, ga2, w_ffn_in_b[1], conv_w[1], conv_b[1], conv_buf,
                         w_ffn_out_b[1], b, t, tm, g_final=g_final, name="ffn1")

    tm = 512
    mp = mods(mod[0:bp], False)
    x = x_prompt.reshape(bp * tp, d)
    x, ret_p, conv0_p = layer0(x, mp[0], None, jnp.zeros((bp, CONV_W - 1, f2), F32), bp, tp, tm, BF16)
    sh1, sc1 = mp[1][0], mp[1][1]
    kv_t, win_t = _kvt_call(x, g_kv, mp[2][0], mp[2][1], w_kv_t, bp, tp, tm)
    q = _nmm_call(x, g_mix[1], sh1, sc1, w_q_pad, tp, tm, hd, BF16, name="q_proj")
    gates = _nmm_call(x, g_mix[1], sh1, sc1, w_g_grp, tp, tm, G_B * LANES, F32, act="sigmoid", name="gate_proj")
    assert tp == N_PAGES_ARG * PAGE_SIZE
    ident = jnp.zeros((bp, N_PAGES_ARG), jnp.int32)
    cmp_t = _cmp_call(kv_t, lambda bi, p, pt: (bi, 0, p), ident, pe_c, w1_c, w2t_c)
    o = _nsa_prompt_call(q, gates, cmp_t, kv_t, win_t, _nsa_prompt_consts(tp), bp, tp)
    y_p, conv1_p = layer1_tail(x, o, w_o_b, mp[1], jnp.zeros((bp, CONV_W - 1, f2), F32), bp, tp, tm)
    kv_p = _rows_major(kv_t, (4, G_B, DH_B))
    win_p = _rows_major(win_t[:, :, tp - min(WINDOW, tp):], (2, G_B, DH_B))

    tm = 256
    ms = mods(jnp.repeat(mod[bp:bp + bs], ts, axis=0), True)
    x = x_sample.reshape(bs * ts, d)
    x, ret_s, conv0_s = layer0(x, ms[0], state_ret[0].astype(F32), state_conv[0], bs, ts, tm, F32)
    sh1, sc1 = ms[1][0], ms[1][1]
    kv_rows = _nmm_call(x, g_kv, ms[2][0], ms[2][1], w_kv_rows, ts, tm, 4 * GD, F32, name="kv_rows")
    win_rows = _nmm_call(x, g_kv, ms[2][0], ms[2][1], w_kv_win, ts, tm, 2 * GD, F32, name="win_rows")
    q = _nmm_call(x, g_mix[1], sh1, sc1, w_q_s, ts, tm, hd, F32, name="q_proj")
    gates = _nmm_call(x, g_mix[1], sh1, sc1, w_g_s, ts, tm, LANES, F32, act="sigmoid", name="gate_proj")
    cache_t = _rows_minor(cache_kv)
    state_win_t = _rows_minor(state_win)
    cmp_t = _cmp_call(cache_t, lambda bi, p, pt: (pt[bi * N_PAGES_ARG + p], 0, 0), page_table, pe_c, w1_c, w2t_c)
    slope_rows = jnp.repeat(slopes, ts).reshape(H_B * ts, 1)
    o = _nsa_sample_call(page_table, cache_t, slope_rows, q, gates, cmp_t, kv_rows, state_win_t, win_rows,
                         _gate_expand(), _overlap_t(-(-(past + ts) // L_SEL)), bs, ts)
    y_s, conv1_s = layer1_tail(x, o, w_o_s, ms[1], state_conv[1], bs, ts, tm)
    kv_s = kv_rows.reshape(bs, ts, 4, G_B, DH_B)
    win_new_t = jnp.concatenate([state_win_t, jnp.swapaxes(win_rows.reshape(bs, ts, 2 * GD), 1, 2)], axis=2)
    win_s = _rows_major(win_new_t[:, :, win_new_t.shape[2] - min(WINDOW, win_new_t.shape[2]):], (2, G_B, DH_B))

    return (y_p.reshape(bp, tp, d), y_s.reshape(bs, ts, d), ret_p[None], ret_s[None],
            jnp.stack([conv0_p, conv1_p]), jnp.stack([conv0_s, conv1_s]), kv_p, kv_s, win_p, win_s)
```

```python
import functools

import numpy as np
import jax
import jax.numpy as jnp
from jax import lax
from jax.experimental import pallas as pl
from jax.experimental.pallas import tpu as pltpu

F32 = jnp.float32
BF16 = jnp.bfloat16

D_MODEL = 1024
H_A = 4
DK_A = D_MODEL // H_A
DV_A = 2 * DK_A
RET_CHUNK = 128
H_B = 16
DH_B = D_MODEL // H_B
G_B = 4
HPG = H_B // G_B
GD = G_B * DH_B
L_CMP = 32
S_CMP = 16
L_SEL = 64
N_SEL = 16
WINDOW = 512
CMP_HID = 2 * DH_B
Q_BLOCK = 128
FORCE_BONUS = 1e4
D_FF = 2816
CONV_W = 3
EPS = 1e-6
PAGE_SIZE = 128
NEG = -1e30
BIG = 2.0 ** 100
M_INIT = -(2.0 ** 101)
LOG2E = 1.4426950408889634

LANES = 128
SUB = 8
VMEM_LIMIT = 56 * 1024 * 1024
FF_CHUNK = 256
SLC_TILE = 512
SM_ROWS = 32
N_PAGES_ARG = 16

F_POS = DH_B
F_ONE = DH_B + 6
F_SEL = DH_B + 8
A_W = 2 * DH_B

_NT = (((1,), (1,)), ((), ()))


def _sigmoid(x):
    return 1.0 / (1.0 + jnp.exp(-x))


def _silu(x):
    return x * _sigmoid(x)


def _dot(a, b):
    return jnp.dot(a, b, preferred_element_type=F32)


def _dot_nt(a, b):
    return lax.dot_general(a, b, _NT, preferred_element_type=F32)


def _split3(x):
    hi = x.astype(BF16)
    r = x - hi.astype(F32)
    mid = r.astype(BF16)
    lo = (r - mid.astype(F32)).astype(BF16)
    return hi, mid, lo


def _params(*sem):
    return pltpu.CompilerParams(dimension_semantics=sem, vmem_limit_bytes=VMEM_LIMIT)


def _const_spec(shape):
    nd = len(shape)
    return pl.BlockSpec(shape, lambda *a: (0,) * nd, pipeline_mode=pl.Buffered(1))


def _mod_arg(v, m, tm, rows_per_batch):
    table, col, per_row = v
    d = D_MODEL
    if per_row:
        assert table.shape[0] == m and m % tm == 0
        return table.reshape(m // tm, tm, table.shape[1]), (1, tm, d), (lambda i: (i, 0, col))
    assert rows_per_batch % tm == 0
    tpb = rows_per_batch // tm
    return table.reshape(table.shape[0], 1, table.shape[1]), (1, 1, d), (lambda i: (i // tpb, 0, col))


def _ada_kernel(c_ref, w_ref, b_ref, o_ref):
    c = c_ref[...]
    o_ref[...] = _dot(_silu(c).astype(BF16), w_ref[...]) + b_ref[...]


def _ada_call(c, w, b, tn=2048):
    m, d = c.shape
    n = w.shape[1]
    assert n % tn == 0
    return pl.pallas_call(
        _ada_kernel,
        grid=(n // tn,),
        in_specs=[pl.BlockSpec((m, d), lambda j: (0, 0)),
                  pl.BlockSpec((d, tn), lambda j: (0, j)),
                  pl.BlockSpec((1, tn), lambda j: (0, j))],
        out_specs=pl.BlockSpec((m, tn), lambda j: (0, j)),
        out_shape=jax.ShapeDtypeStruct((m, n), F32),
        compiler_params=_params("parallel"),
        name="ada",
    )(c, w, b)


def _norm_mod(x, g, sh, sc):
    y = x * lax.rsqrt(jnp.mean(x * x, axis=-1, keepdims=True) + EPS) * g
    return y * (1.0 + sc) + sh


def _nmm_kernel(x_ref, g_ref, sh_ref, sc_ref, w_ref, o_ref, h_scr, *, act):
    @pl.when(pl.program_id(1) == 0)
    def _():
        h_scr[...] = _norm_mod(x_ref[...], g_ref[...], sh_ref[0], sc_ref[0]).astype(BF16)

    acc = _dot(h_scr[...], w_ref[...])
    if act == "sigmoid":
        acc = _sigmoid(acc)
    o_ref[...] = acc.astype(o_ref.dtype)


def _nmm_call(x, g, sh, sc, w, rows_per_batch, tm, tn, out_dtype, act=None, name="nmm"):
    m, d = x.shape
    n = w.shape[1]
    assert m % tm == 0 and n % tn == 0
    sh_a, mshape, sh_map = _mod_arg(sh, m, tm, rows_per_batch)
    sc_a, _, sc_map = _mod_arg(sc, m, tm, rows_per_batch)
    return pl.pallas_call(
        functools.partial(_nmm_kernel, act=act),
        grid=(m // tm, n // tn),
        in_specs=[pl.BlockSpec((tm, d), lambda i, j: (i, 0)),
                  pl.BlockSpec((1, d), lambda i, j: (0, 0)),
                  pl.BlockSpec(mshape, lambda i, j: sh_map(i)),
                  pl.BlockSpec(mshape, lambda i, j: sc_map(i)),
                  pl.BlockSpec((d, tn), lambda i, j: (0, j))],
        out_specs=pl.BlockSpec((tm, tn), lambda i, j: (i, j)),
        out_shape=jax.ShapeDtypeStruct((m, n), out_dtype),
        scratch_shapes=[pltpu.VMEM((tm, d), BF16)],
        compiler_params=_params("parallel", "arbitrary"),
        name=name,
    )(x, g.reshape(1, d), sh_a, sc_a, w)


def _kvt_kernel(x_ref, g_ref, sh_ref, sc_ref, wt_ref, kv_ref, win_ref):
    h = _norm_mod(x_ref[...], g_ref[...], sh_ref[0], sc_ref[0]).astype(BF16)
    n_kv = kv_ref.shape[1]
    kv_ref[0] = _dot_nt(wt_ref[0:n_kv, :], h)
    win_ref[0] = _dot_nt(wt_ref[n_kv:, :], h)


def _kvt_call(x, g, sh, sc, w_t, b, t, tm):
    m, d = x.shape
    n = w_t.shape[0]
    nt = t // tm
    sh_a, mshape, sh_map = _mod_arg(sh, m, tm, t)
    sc_a, _, sc_map = _mod_arg(sc, m, tm, t)
    return pl.pallas_call(
        _kvt_kernel,
        grid=(b, nt),
        in_specs=[pl.BlockSpec((tm, d), lambda bi, ti: (bi * nt + ti, 0)),
                  pl.BlockSpec((1, d), lambda bi, ti: (0, 0)),
                  pl.BlockSpec(mshape, lambda bi, ti: sh_map(bi * nt + ti)),
                  pl.BlockSpec(mshape, lambda bi, ti: sc_map(bi * nt + ti)),
                  _const_spec((n, d))],
        out_specs=[pl.BlockSpec((1, 4 * GD, tm), lambda bi, ti: (bi, 0, ti)),
                   pl.BlockSpec((1, 2 * GD, tm), lambda bi, ti: (bi, 0, ti))],
        out_shape=[jax.ShapeDtypeStruct((b, 4 * GD, t), F32),
                   jax.ShapeDtypeStruct((b, 2 * GD, t), F32)],
        compiler_params=_params("parallel", "parallel"),
        name="kv_proj_t",
    )(x, g.reshape(1, d), sh_a, sc_a, w_t)


def _mm_res_kernel(a_ref, w_ref, res_ref, ga_ref, o_ref):
    y = _dot(a_ref[...].astype(BF16), w_ref[...])
    o_ref[...] = res_ref[...] + ga_ref[0] * y


def _mm_res_call(a, w, res, gate, rows_per_batch, tm, name="mm_res"):
    m, k = a.shape
    d = w.shape[1]
    ga_a, mshape, mmap = _mod_arg(gate, m, tm, rows_per_batch)
    return pl.pallas_call(
        _mm_res_kernel,
        grid=(m // tm,),
        in_specs=[pl.BlockSpec((tm, k), lambda i: (i, 0)),
                  _const_spec((k, d)),
                  pl.BlockSpec((tm, d), lambda i: (i, 0)),
                  pl.BlockSpec(mshape, lambda i: mmap(i))],
        out_specs=pl.BlockSpec((tm, d), lambda i: (i, 0)),
        out_shape=jax.ShapeDtypeStruct((m, d), F32),
        compiler_params=_params("parallel"),
        name=name,
    )(a, w, res, ga_a)


def _ret_kernel(*refs, c, nc, has_s0):
    if has_s0:
        lg_ref, q_ref, k_ref, v_ref, g_ref, s0_ref, o_ref, so_ref, s_scr = refs
    else:
        lg_ref, q_ref, k_ref, v_ref, g_ref, o_ref, so_ref, s_scr = refs
        s0_ref = None
    n = pl.program_id(1)

    @pl.when(n == 0)
    def _():
        if has_s0:
            s_scr[...] = s0_ref[0]
        else:
            s_scr[...] = jnp.zeros_like(s_scr)

    cp = max(c, RET_CHUNK)

    def padded(a):
        a = a.astype(F32)
        if cp > c:
            a = jnp.concatenate([a, jnp.zeros((cp - c, a.shape[1]), F32)], axis=0)
        return a

    i = lax.broadcasted_iota(jnp.int32, (cp, 1), 0).astype(F32)
    j = lax.broadcasted_iota(jnp.int32, (1, cp), 1).astype(F32)
    diff = i - j
    s_olds = [s_scr[h] for h in range(H_A)]
    s_news = [None] * H_A

    def head_steps(h):
        lg = lg_ref[h][:, 0:1]
        q = padded(q_ref[:, h * DK_A:(h + 1) * DK_A]).astype(BF16)
        kf = padded(k_ref[:, h * DK_A:(h + 1) * DK_A]) * (DK_A ** -0.5)
        v = padded(v_ref[:, h * DV_A:(h + 1) * DV_A]).astype(BF16)
        dmask = jnp.where(diff >= 0, jnp.exp(jnp.maximum(diff, 0.0) * lg), 0.0)
        scores = _dot_nt(q, kf.astype(BF16)) * dmask
        yield
        s_old = s_olds[h]
        o = _dot(scores.astype(BF16), v) + _dot(q, s_old.astype(BF16)) * jnp.exp((i + 1.0) * lg)
        yield
        w = jnp.exp((c - 1.0 - i) * lg)
        kw_t = (kf * w).T.astype(BF16)
        s_news[h] = jnp.exp(c * lg) * s_old + _dot(kw_t, v)
        yield
        of = o * lax.rsqrt(jnp.mean(o * o, axis=-1, keepdims=True) + EPS)
        gate = g_ref[:, h * DV_A:(h + 1) * DV_A].astype(F32)
        o_ref[:, h * DV_A:(h + 1) * DV_A] = (of[0:c] * _silu(gate)).astype(o_ref.dtype)

    _round_robin(*[head_steps(h) for h in range(H_A)])
    for h in range(H_A):
        s_scr[h] = s_news[h]

    @pl.when(n == nc - 1)
    def _():
        so_ref[0] = s_scr[...]


def _ret_call(pr, s0, lg_tab, b, t, out_dtype):
    c = RET_CHUNK if t % RET_CHUNK == 0 else t
    nc = t // c
    has_s0 = s0 is not None
    nq, nv = H_A * DK_A, H_A * DV_A
    assert nv == 2 * nq
    in_specs = [pl.BlockSpec((H_A, 1, LANES), lambda bi, n: (0, 0, 0)),
                pl.BlockSpec((c, nq), lambda bi, n: (bi * nc + n, 0)),
                pl.BlockSpec((c, nq), lambda bi, n: (bi * nc + n, 1)),
                pl.BlockSpec((c, nv), lambda bi, n: (bi * nc + n, 1)),
                pl.BlockSpec((c, nv), lambda bi, n: (bi * nc + n, 2))]
    args = [lg_tab, pr, pr, pr, pr]
    if has_s0:
        in_specs.append(pl.BlockSpec((1, H_A, DK_A, DV_A), lambda bi, n: (bi, 0, 0, 0)))
        args.append(s0)
    return pl.pallas_call(
        functools.partial(_ret_kernel, c=c, nc=nc, has_s0=has_s0),
        grid=(b, nc),
        in_specs=in_specs,
        out_specs=[pl.BlockSpec((c, nv), lambda bi, n: (bi * nc + n, 0)),
                   pl.BlockSpec((1, H_A, DK_A, DV_A), lambda bi, n: (bi, 0, 0, 0))],
        out_shape=[jax.ShapeDtypeStruct((b * t, nv), out_dtype),
                   jax.ShapeDtypeStruct((b, H_A, DK_A, DV_A), F32)],
        scratch_shapes=[pltpu.VMEM((H_A, DK_A, DV_A), F32)],
        compiler_params=_params("parallel", "arbitrary"),
        name="retention",
    )(*args)


def _ffn_kernel(*refs, t_in, nbt, final):
    if final:
        (x_ref, g_ref, sh_ref, sc_ref, ga_ref, win_ref, cw_ref, cb_ref, cbuf_ref, wout_ref, gf_ref,
         o_ref, nc_ref, h_scr, carry_scr, act_scr, shift_scr) = refs
    else:
        (x_ref, g_ref, sh_ref, sc_ref, ga_ref, win_ref, cw_ref, cb_ref, cbuf_ref, wout_ref,
         o_ref, nc_ref, h_scr, carry_scr, act_scr, shift_scr) = refs
        gf_ref = None
    tm = x_ref.shape[0]

    @pl.when(pl.program_id(1) == 0)
    def _():
        carry_scr[...] = cbuf_ref[...]

    x = x_ref[...]
    h_scr[...] = _norm_mod(x, g_ref[...], sh_ref[0], sc_ref[0]).astype(BF16)
    cwd = FF_CHUNK
    tpos = lax.broadcasted_iota(jnp.int32, (nbt, t_in, cwd), 1)
    is0 = tpos == 0
    is1 = tpos == 1

    def conv_rolled(off):
        u = _dot(h_scr[...], win_ref[:, off:off + cwd]).reshape(nbt, t_in, cwd)
        prev = carry_scr[:, :, off:off + cwd]
        p0 = prev[:, 0:1, :]
        p1 = prev[:, 1:2, :]
        r1 = pltpu.roll(u, 1, axis=1)
        r2 = pltpu.roll(u, 2, axis=1)
        um1 = jnp.where(is0, p1, r1)
        um2 = jnp.where(is0, p0, jnp.where(is1, p1, r2))
        newc = r2[:, 0:2, :]
        carry_scr[:, :, off:off + cwd] = newc
        nc_ref[:, :, off:off + cwd] = newc
        cw = cw_ref[:, off:off + cwd]
        z = cb_ref[:, off:off + cwd] + cw[0:1] * um2
        z = z + cw[1:2] * um1
        z = z + cw[2:3] * u
        return z.reshape(tm, cwd)

    def up_shifted(off, slot):
        buf = shift_scr.at[slot]
        buf[SUB:SUB + tm, :] = _dot(h_scr[...], win_ref[:, off:off + cwd])
        buf[SUB - 2:SUB, :] = carry_scr[0, :, off:off + cwd]
        newc = buf[SUB + tm - 2:SUB + tm, :]
        carry_scr[0, :, off:off + cwd] = newc
        nc_ref[0, :, off:off + cwd] = newc

    def conv_shifted(off, slot):
        buf = shift_scr.at[slot]
        cw = cw_ref[:, off:off + cwd]
        z = cb_ref[:, off:off + cwd] + cw[0:1] * buf[SUB - 2:SUB - 2 + tm, :]
        z = z + cw[1:2] * buf[SUB - 1:SUB - 1 + tm, :]
        z = z + cw[2:3] * buf[SUB:SUB + tm, :]
        return z

    n_slots = shift_scr.shape[0]
    n_ch = D_FF // cwd

    def up_chunk(ch):
        up_shifted(ch * cwd, (2 * ch) % n_slots)
        up_shifted(D_FF + ch * cwd, (2 * ch + 1) % n_slots)

    if nbt == 1:
        up_chunk(0)
    for ch in range(n_ch):
        if nbt == 1:
            if ch + 1 < n_ch:
                up_chunk(ch + 1)
            za = conv_shifted(ch * cwd, (2 * ch) % n_slots)
            zg = conv_shifted(D_FF + ch * cwd, (2 * ch + 1) % n_slots)
        else:
            za = conv_rolled(ch * cwd)
            zg = conv_rolled(D_FF + ch * cwd)
        act_scr[:, ch * cwd:(ch + 1) * cwd] = (_silu(zg) * za).astype(BF16)
    xn = x + ga_ref[0] * _dot(act_scr[...], wout_ref[...])
    if final:
        xn = xn * lax.rsqrt(jnp.mean(xn * xn, axis=-1, keepdims=True) + EPS) * gf_ref[...]
    o_ref[...] = xn


def _ffn_call(x, g, sh, sc, ga, w_in, cw, cb, cbuf, w_out, b, t, tm, g_final=None, name="ffn"):
    m, d = x.shape
    f2 = w_in.shape[1]
    if t % tm == 0:
        t_in, nbt, nb_tiles, nt = tm, 1, b, t // tm
    else:
        assert tm % t == 0 and m % tm == 0
        t_in, nbt, nb_tiles, nt = t, tm // t, m // tm, 1
    sh_a, mshape, sh_map = _mod_arg(sh, m, tm, t)
    sc_a, _, sc_map = _mod_arg(sc, m, tm, t)
    ga_a, _, ga_map = _mod_arg(ga, m, tm, t)
    final = g_final is not None
    in_specs = [pl.BlockSpec((tm, d), lambda bi, ti: (bi * nt + ti, 0)),
                pl.BlockSpec((1, d), lambda bi, ti: (0, 0)),
                pl.BlockSpec(mshape, lambda bi, ti: sh_map(bi * nt + ti)),
                pl.BlockSpec(mshape, lambda bi, ti: sc_map(bi * nt + ti)),
                pl.BlockSpec(mshape, lambda bi, ti: ga_map(bi * nt + ti)),
                _const_spec((d, f2)),
                _const_spec((CONV_W, f2)),
                _const_spec((1, f2)),
                pl.BlockSpec((nbt, CONV_W - 1, f2), lambda bi, ti: (bi, 0, 0)),
                _const_spec((D_FF, d))]
    args = [x, g.reshape(1, d), sh_a, sc_a, ga_a, w_in, cw, cb.reshape(1, f2), cbuf, w_out]
    if final:
        in_specs.append(pl.BlockSpec((1, d), lambda bi, ti: (0, 0)))
        args.append(g_final.reshape(1, d))
    return pl.pallas_call(
        functools.partial(_ffn_kernel, t_in=t_in, nbt=nbt, final=final),
        grid=(nb_tiles, nt),
        in_specs=in_specs,
        out_specs=[pl.BlockSpec((tm, d), lambda bi, ti: (bi * nt + ti, 0)),
                   pl.BlockSpec((nbt, CONV_W - 1, f2), lambda bi, ti: (bi, 0, 0))],
        out_shape=[jax.ShapeDtypeStruct((m, d), F32),
                   jax.ShapeDtypeStruct((b, CONV_W - 1, f2), F32)],
        scratch_shapes=[pltpu.VMEM((tm, d), BF16),
                        pltpu.VMEM((nbt, CONV_W - 1, f2), F32),
                        pltpu.VMEM((tm, D_FF), BF16),
                        pltpu.VMEM((4, SUB + tm, FF_CHUNK), F32)],
        compiler_params=_params("parallel", "arbitrary"),
        name=name,
    )(*args)


def _cmp_kernel(pt_ref, *refs):
    pages = refs[:N_PAGES_ARG]
    pe_ref, w1_ref, w2t_ref, o_ref = refs[N_PAGES_ARG:]
    npc = PAGE_SIZE // S_CMP
    r_n = L_CMP // S_CMP
    for kind in range(2):
        rows = jnp.concatenate([pg[0, kind * GD:(kind + 1) * GD, :].T for pg in pages], axis=0)
        rows = jnp.swapaxes(rows.reshape(len(pages) * npc, S_CMP, GD), 0, 1)
        pieces = [rows[s] for s in range(S_CMP)]
        z = None
        for r in range(r_n):
            ys = []
            for gp in range(G_B // 2):
                ls = slice(gp * LANES, (gp + 1) * LANES)
                a = jnp.concatenate([(pieces[s][:, ls] + pe_ref[kind, r, s:s + 1, ls]).astype(BF16)
                                     for s in range(S_CMP)], axis=1)
                ys.append(_dot(a, w1_ref[kind, r]))
            y = jnp.concatenate(ys, axis=1)
            if r > 0:
                y = pltpu.roll(y, y.shape[0] - r, axis=0)
            z = y if z is None else z + y
        o_ref[0, kind] = _dot_nt(w2t_ref[kind], _silu(z).astype(BF16)).astype(o_ref.dtype)


def _cmp_call(rows_t, page_map, page_tab, pe, w1, w2t):
    b, npg = page_tab.shape
    assert npg == N_PAGES_ARG
    n_pieces = npg * PAGE_SIZE // S_CMP
    page_specs = [pl.BlockSpec((1, 2 * GD, PAGE_SIZE), functools.partial(lambda bi, pt, p: page_map(bi, p, pt), p=p))
                  for p in range(npg)]
    grid_spec = pltpu.PrefetchScalarGridSpec(
        num_scalar_prefetch=1,
        grid=(b,),
        in_specs=page_specs + [
            pl.BlockSpec(pe.shape, lambda bi, pt: (0, 0, 0, 0), pipeline_mode=pl.Buffered(1)),
            pl.BlockSpec(w1.shape, lambda bi, pt: (0, 0, 0, 0), pipeline_mode=pl.Buffered(1)),
            pl.BlockSpec(w2t.shape, lambda bi, pt: (0, 0, 0), pipeline_mode=pl.Buffered(1))],
        out_specs=pl.BlockSpec((1, 2, GD, n_pieces), lambda bi, pt: (bi, 0, 0, 0)),
    )
    return pl.pallas_call(
        _cmp_kernel,
        grid_spec=grid_spec,
        out_shape=jax.ShapeDtypeStruct((b, 2, GD, n_pieces), BF16),
        compiler_params=_params("parallel"),
        name="compress",
    )(page_tab.reshape(-1), *([rows_t] * npg), pe, w1, w2t)


def _masked_softmax(s, ok, base2=False):
    sm = jnp.where(ok, s, NEG)
    m = jnp.max(sm, axis=-1, keepdims=True)
    e = jnp.where(ok, (jnp.exp2 if base2 else jnp.exp)(sm - m), 0.0)
    return e / jnp.maximum(jnp.sum(e, axis=-1, keepdims=True), 1e-30)


def _topk_mask_steps(imp_t, cur, nsel, ncol):
    rows = imp_t.shape[0]
    jj = lax.broadcasted_iota(jnp.int32, (rows, ncol), 0)
    forced = (jj == 0) | (jj == cur) | (jj == cur - 1)
    score = jnp.where(jj <= cur, imp_t + FORCE_BONUS * forced.astype(F32), NEG)
    rank = jnp.zeros((rows, ncol), F32)
    for i in range(nsel):
        row = score[i:i + 1, :]
        lower = jnp.where(jj > i, 1.0, 0.0)
        rank = rank + jnp.where(row > score, 1.0, jnp.where(row == score, lower, 0.0))
        if i % 8 == 7 and i + 1 < nsel:
            yield None
    yield jnp.where((rank < float(min(N_SEL, nsel))) & (jj < nsel), 1.0, 0.0)


def _topk_mask_t(imp_t, cur, nsel, ncol):
    out = None
    for out in _topk_mask_steps(imp_t, cur, nsel, ncol):
        pass
    return out


def _round_robin(*streams):
    live = list(streams)
    while live:
        for s in list(live):
            try:
                next(s)
            except StopIteration:
                live.remove(s)


def _eye_bf16(n):
    return jnp.where(lax.broadcasted_iota(jnp.int32, (n, n), 0) == lax.broadcasted_iota(jnp.int32, (n, n), 1),
                     1.0, 0.0).astype(BF16)


def _pad_rows(a, rows):
    if a.shape[0] == rows:
        return a
    return jnp.concatenate([a, jnp.zeros((rows - a.shape[0],) + a.shape[1:], a.dtype)], axis=0)


def _online_step(sm, v, m_old, l_old, acc_old, v_transposed):
    m_new = jnp.maximum(m_old, jnp.max(sm, axis=-1, keepdims=True))
    alpha = jnp.exp(m_old - m_new)
    e = jnp.where(sm > 0.5 * NEG, jnp.exp(sm - m_new), 0.0)
    l_new = alpha * l_old + jnp.sum(e, axis=-1, keepdims=True)
    pv = _dot_nt(e.astype(BF16), v) if v_transposed else _dot(e.astype(BF16), v)
    return m_new, l_new, alpha * acc_old + pv


def _block_importance_t(ps_sum, ovt):
    hi, mid, lo = _split3(ps_sum)
    return _dot_nt(ovt, hi) + _dot_nt(ovt, mid) + _dot_nt(ovt, lo)


def _nsa_prompt_kernel(q_ref, gate_ref, cmp_ref, kv_ref, win_ref, featc_ref, fslc_ref, fwin_ref, fcmp_ref,
                       ovt_ref, band_ref, cbias_ref, o_ref,
                       kb_slc, vt_slc, kb_win, vt_win, kb_cmp, a_scr, s_scr, e_scr, m_scr, acc_scr, gb_scr, *, t_len):
    qb = pl.program_id(1)
    s0 = qb * Q_BLOCK
    nsel = t_len // L_SEL
    n_st = t_len // SLC_TILE
    n_wt = WINDOW // Q_BLOCK
    n_wtiles = kb_win.shape[1]
    rows_g = HPG * Q_BLOCK

    def ones_rows(width):
        r = lax.broadcasted_iota(jnp.int32, (A_W - DH_B, width), 0)
        return jnp.where(r == 0, 1.0, 0.0).astype(BF16)

    @pl.when(qb == 0)
    def _():
        for g in range(G_B):
            for kt in range(n_st):
                cs = slice(kt * SLC_TILE, (kt + 1) * SLC_TILE)
                kb_slc[g, kt, 0:DH_B, :] = kv_ref[0, g * DH_B:(g + 1) * DH_B, cs].astype(BF16)
                kb_slc[g, kt, DH_B:A_W, :] = fslc_ref[:, cs]
            for wt in range(n_wtiles):
                cs = slice((wt - n_wt) * Q_BLOCK, (wt - n_wt + 1) * Q_BLOCK)
                if wt < n_wt:
                    kb_win[g, wt, 0:DH_B, :] = jnp.zeros((DH_B, Q_BLOCK), BF16)
                else:
                    kb_win[g, wt, 0:DH_B, :] = win_ref[0, g * DH_B:(g + 1) * DH_B, cs].astype(BF16)
                kb_win[g, wt, DH_B:A_W, :] = fwin_ref[:, wt * Q_BLOCK:(wt + 1) * Q_BLOCK]
            kb_cmp[g, 0:DH_B, :] = cmp_ref[0, 0, g * DH_B:(g + 1) * DH_B, :]
            kb_cmp[g, DH_B:A_W, :] = fcmp_ref[...]
            for kt in range(n_st):
                cs = slice(kt * SLC_TILE, (kt + 1) * SLC_TILE)
                vt_slc[g, kt, 0:DH_B, :] = kv_ref[0, GD + g * DH_B:GD + (g + 1) * DH_B, cs].astype(BF16)
                vt_slc[g, kt, DH_B:A_W, :] = ones_rows(SLC_TILE)
            for wt in range(n_wtiles):
                cs = slice((wt - n_wt) * Q_BLOCK, (wt - n_wt + 1) * Q_BLOCK)
                if wt < n_wt:
                    vt_win[g, wt, 0:DH_B, :] = jnp.zeros((DH_B, Q_BLOCK), BF16)
                else:
                    vt_win[g, wt, 0:DH_B, :] = win_ref[0, GD + g * DH_B:GD + (g + 1) * DH_B, cs].astype(BF16)
                vt_win[g, wt, DH_B:A_W, :] = ones_rows(Q_BLOCK)

    lane = lax.broadcasted_iota(jnp.int32, (1, A_W), 1)
    ags = [jnp.concatenate([jnp.where(lane < DH_B, q_ref[:, (g * HPG + hl) * A_W:(g * HPG + hl + 1) * A_W],
                                      featc_ref[g * HPG + hl:g * HPG + hl + 1, :].astype(BF16))
                            for hl in range(HPG)], axis=0) for g in range(G_B)]

    ovt = ovt_ref[...]
    eye = _eye_bf16(Q_BLOCK)
    t_loc = lax.broadcasted_iota(jnp.int32, (Q_BLOCK, 1), 0)
    tpos = (s0 + t_loc).astype(F32)
    n_f = lax.broadcasted_iota(jnp.int32, (1, Q_BLOCK), 1).astype(F32)
    ok_c = (tpos - (n_f * S_CMP + (L_CMP - 1.0))) >= 0.0
    nrow = ((nsel + 7) // 8) * 8
    groups = range(G_B)

    def softmax_steps(g, width, bias_fn, out):
        m_old_all = m_scr[g]
        m_news, alphas = [], []
        for rb in range(rows_g // SM_ROWS):
            rows = slice(rb * SM_ROWS, (rb + 1) * SM_ROWS)
            s = s_scr[g, rows, 0:width]
            if bias_fn is not None:
                s = bias_fn(s, (rb * SM_ROWS) % Q_BLOCK)
            m_old = m_old_all[rows]
            m_new = jnp.maximum(m_old, jnp.max(s, axis=-1, keepdims=True))
            e = jnp.exp2(s - jnp.concatenate([m_new] * (width // LANES), axis=1))
            e_scr[g, rows, 0:width] = e.astype(BF16)
            m_news.append(m_new)
            alphas.append(jnp.exp2(m_old - m_new))
            if rb % 2 == 1:
                yield
        m_scr[g] = jnp.concatenate(m_news, axis=0)
        out.append(jnp.concatenate(alphas, axis=0))

    def reset_stats():
        m_scr[...] = jnp.full_like(m_scr, M_INIT)
        acc_scr[...] = jnp.zeros_like(acc_scr)

    def attend_steps(width, a_fn, key_fn, val_fn, bias_fn):
        s_scr[0, :, 0:width] = _dot(a_fn(0), key_fn(0))
        yield
        for g in groups:
            if g + 1 < G_B:
                s_scr[g + 1, :, 0:width] = _dot(a_fn(g + 1), key_fn(g + 1))
            alpha = []
            yield from softmax_steps(g, width, bias_fn, alpha)
            acc_scr[g] = alpha[0] * acc_scr[g] + _dot_nt(e_scr[g, :, 0:width], val_fn(g))
            yield

    def attend(*args):
        for _ in attend_steps(*args):
            pass

    def branch_out(g):
        acc = acc_scr[g]
        return acc[:, 0:DH_B] / jnp.maximum(acc[:, DH_B:DH_B + 1], 1e-30)

    reset_stats()
    win_stream = attend_steps(WINDOW + Q_BLOCK,
                              lambda g: ags[g],
                              lambda g: jnp.concatenate([kb_win[g, qb + w] for w in range(n_wt + 1)], axis=1),
                              lambda g: jnp.concatenate([vt_win[g, qb + w] for w in range(n_wt + 1)], axis=1),
                              lambda s, t0: s + band_ref[t0:t0 + SM_ROWS, :])

    cur_t = (s0 + lax.broadcasted_iota(jnp.int32, (nrow, Q_BLOCK), 1)) // L_SEL
    o_cmp, ags_sel = [None] * G_B, [None] * G_B

    def sel_stream(g):
        s_c = _dot(ags[g], kb_cmp[g]).reshape(HPG, Q_BLOCK, Q_BLOCK)
        yield
        p3 = _masked_softmax(s_c, ok_c[None], base2=True)
        yield
        o_cmp[g] = _dot_nt(p3.reshape(rows_g, Q_BLOCK).astype(BF16), cmp_ref[0, 1, g * DH_B:(g + 1) * DH_B, :])
        imp_t = _block_importance_t((p3[0] + p3[1]) + (p3[2] + p3[3]), ovt)
        yield
        sel_t = None
        for sel_t in _topk_mask_steps(imp_t[0:nrow], cur_t, nsel, Q_BLOCK):
            yield
        nsel_pad = jnp.concatenate([jnp.zeros((F_SEL, Q_BLOCK), F32), 1.0 - sel_t,
                                    jnp.zeros((A_W - F_SEL - nrow, Q_BLOCK), F32)], axis=0).astype(BF16)
        nsl = _dot_nt(eye, nsel_pad).astype(BF16)
        ags_sel[g] = ags[g] + jnp.concatenate([nsl] * HPG, axis=0)
        yield

    def gate_stream():
        for g in groups:
            gt = gate_ref[:, g * LANES:(g + 1) * LANES]
            for c in range(3 * HPG):
                gb_scr[g * 3 * HPG + c] = jnp.broadcast_to(gt[:, c:c + 1], (Q_BLOCK, LANES))
                if c % 2 == 1:
                    yield

    _round_robin(win_stream, *[sel_stream(g) for g in groups], gate_stream())
    o_win = [branch_out(g) for g in groups]
    for g in groups:
        a_scr[g] = ags_sel[g]

    def causal_bias(kti):
        c = qb % (SLC_TILE // Q_BLOCK)

        def fn(s, t0):
            return s + cbias_ref[c, t0:t0 + SM_ROWS, :]
        return fn

    def slc_tile(kti, diag):
        attend(SLC_TILE, lambda g: a_scr[g], lambda g: kb_slc[g, kti], lambda g: vt_slc[g, kti],
               causal_bias(kti) if diag else None)

    reset_stats()
    kd = s0 // SLC_TILE

    def slc_full(kti, c):
        slc_tile(kti, False)
        return c

    lax.fori_loop(0, kd, slc_full, 0)
    slc_tile(kd, True)
    o_slc = [branch_out(g) for g in groups]

    pieces = []
    for g in groups:
        for hl in range(HPG):
            r = slice(hl * Q_BLOCK, (hl + 1) * Q_BLOCK)
            gb = [gb_scr[g * 3 * HPG + br * HPG + hl, :, 0:DH_B] for br in range(3)]
            comb = gb[0] * o_cmp[g][r]
            comb = comb + gb[1] * o_slc[g][r]
            comb = comb + gb[2] * o_win[g][r]
            pieces.append(comb)
    o_ref[...] = jnp.concatenate(pieces, axis=1).astype(o_ref.dtype)


def _nsa_prompt_call(q, gates, cmp_t, kv_t, win_t, consts, b, t):
    featc, fslc, fwin, fcmp, ovt, band, cbias = consts
    nqb = t // Q_BLOCK
    n_wtiles = (WINDOW + t) // Q_BLOCK
    return pl.pallas_call(
        functools.partial(_nsa_prompt_kernel, t_len=t),
        grid=(b, nqb),
        in_specs=[pl.BlockSpec((Q_BLOCK, H_B * A_W), lambda bi, qi: (bi * nqb + qi, 0)),
                  pl.BlockSpec((Q_BLOCK, G_B * LANES), lambda bi, qi: (bi * nqb + qi, 0)),
                  pl.BlockSpec((1, 2, GD, cmp_t.shape[3]), lambda bi, qi: (bi, 0, 0, 0)),
                  pl.BlockSpec((1, 2 * GD, t), lambda bi, qi: (bi, 1, 0)),
                  pl.BlockSpec((1, 2 * GD, t), lambda bi, qi: (bi, 0, 0)),
                  _const_spec(featc.shape), _const_spec(fslc.shape), _const_spec(fwin.shape),
                  _const_spec(fcmp.shape), _const_spec(ovt.shape), _const_spec(band.shape), _const_spec(cbias.shape)],
        out_specs=pl.BlockSpec((Q_BLOCK, D_MODEL), lambda bi, qi: (bi * nqb + qi, 0)),
        out_shape=jax.ShapeDtypeStruct((b * t, D_MODEL), BF16),
        scratch_shapes=[pltpu.VMEM((G_B, t // SLC_TILE, A_W, SLC_TILE), BF16),
                        pltpu.VMEM((G_B, t // SLC_TILE, A_W, SLC_TILE), BF16),
                        pltpu.VMEM((G_B, n_wtiles, A_W, Q_BLOCK), BF16),
                        pltpu.VMEM((G_B, n_wtiles, A_W, Q_BLOCK), BF16),
                        pltpu.VMEM((G_B, A_W, Q_BLOCK), BF16),
                        pltpu.VMEM((G_B, HPG * Q_BLOCK, A_W), BF16),
                        pltpu.VMEM((G_B, HPG * Q_BLOCK, WINDOW + Q_BLOCK), F32),
                        pltpu.VMEM((G_B, HPG * Q_BLOCK, WINDOW + Q_BLOCK), BF16),
                        pltpu.VMEM((G_B, HPG * Q_BLOCK, LANES), F32),
                        pltpu.VMEM((G_B, HPG * Q_BLOCK, A_W), F32),
                        pltpu.VMEM((3 * H_B, Q_BLOCK, LANES), F32)],
        compiler_params=_params("parallel", "arbitrary"),
        name="nsa_prompt",
    )(q, gates, cmp_t, kv_t, win_t, featc, fslc, fwin, fcmp, ovt, band, cbias)


def _nsa_prompt_consts(t):
    slopes = jnp.exp2(-8.0 * jnp.arange(1, H_B + 1, dtype=F32) / H_B)
    featc = jnp.zeros((H_B, A_W), F32).at[:, F_ONE].set(1.0)
    for k, term in enumerate(_split3(slopes * LOG2E)):
        featc = featc.at[:, F_POS + 2 * k].set(term.astype(F32)).at[:, F_POS + 2 * k + 1].set(term.astype(F32))

    def pos_rows(pos):
        f = np.zeros((A_W - DH_B, pos.shape[0]), np.float32)
        hi = (pos // 256) * 256
        lo = pos % 256
        for k in range(3):
            f[F_POS - DH_B + 2 * k] = hi
            f[F_POS - DH_B + 2 * k + 1] = lo
        return f

    nsel = t // L_SEL
    spos = np.arange(t)
    fslc = pos_rows(spos)
    for j in range(nsel):
        fslc[F_SEL - DH_B + j] = np.where(spos // L_SEL == j, -BIG, 0.0)
    wcol = np.arange(WINDOW + t)
    fwin = pos_rows(wcol)
    fwin[F_ONE - DH_B] = np.where(wcol < WINDOW, -BIG, 0.0)
    fcmp = pos_rows(np.arange(Q_BLOCK) * S_CMP + (L_CMP - 1))
    u = np.arange(WINDOW + Q_BLOCK)[None, :]
    tl = np.arange(Q_BLOCK)[:, None]
    band = np.where((u >= tl) & (u <= tl + WINDOW), 0.0, -BIG).astype(np.float32)
    uc = np.arange(SLC_TILE)[None, None, :]
    cpos = (np.arange(SLC_TILE // Q_BLOCK) * Q_BLOCK)[:, None, None] + np.arange(Q_BLOCK)[None, :, None]
    cbias = np.where(uc <= cpos, 0.0, -BIG).astype(np.float32)
    return (featc, jnp.asarray(fslc, dtype=BF16), jnp.asarray(fwin, dtype=BF16),
            jnp.asarray(fcmp, dtype=BF16), _overlap_t(nsel), jnp.asarray(band), jnp.asarray(cbias))


def _nsa_sample_kernel(pt_ref, *refs, past, ts):
    pages = refs[:N_PAGES_ARG]
    (slope_ref, slopel_ref, q_ref, gate_ref, cmp_ref, kvn_ref, win_ref, winn_ref, eg_ref, ovt_ref, o_ref) = refs[N_PAGES_ARG:]
    rows = H_B * ts
    nsel = -(-(past + ts) // L_SEL)
    lane_g = lax.broadcasted_iota(jnp.int32, (1, GD), 1) // DH_B
    q = q_ref[...]
    qq = jnp.concatenate([jnp.where(lane_g == g, q[:, h * GD:(h + 1) * GD], 0.0)
                          for g in range(G_B) for h in range(HPG)], axis=0).astype(BF16)
    slope = slope_ref[...]
    t_row = lax.broadcasted_iota(jnp.int32, (rows, 1), 0) % ts
    tpos = (past + t_row).astype(F32)
    kc_t = cmp_ref[0, 0]
    vc_t = cmp_ref[0, 1]
    ncb = kc_t.shape[1]
    res = {}
    n_f = lax.broadcasted_iota(jnp.int32, (1, ncb), 1).astype(F32)
    dist_c = tpos - (n_f * S_CMP + (L_CMP - 1.0))
    nrow = ((nsel + 1 + 7) // 8) * 8
    cur_t = (past + lax.broadcasted_iota(jnp.int32, (nrow, rows), 1) % ts) // L_SEL

    def select_steps():
        p = _masked_softmax(_dot(qq, kc_t) - slope * dist_c, dist_c >= 0.0)
        yield
        res["o_cmp"] = _dot_nt(p.astype(BF16), vc_t)
        ps = []
        for g in range(G_B):
            blk = [p[(g * HPG + h) * ts:(g * HPG + h + 1) * ts] for h in range(HPG)]
            ps.extend([(blk[0] + blk[1]) + (blk[2] + blk[3])] * HPG)
        psum = jnp.concatenate(ps, axis=0)
        imp_t = _block_importance_t(psum, ovt_ref[...])
        yield
        for sel in _topk_mask_steps(imp_t[0:nrow], cur_t, nsel, rows):
            res["sel"] = sel
            yield

    tpos_l = (past + lax.broadcasted_iota(jnp.int32, (1, rows), 1) % ts).astype(F32)
    slope_l = slopel_ref[...]
    key_i = lax.broadcasted_iota(jnp.int32, (PAGE_SIZE, rows), 0)
    bpp = PAGE_SIZE // L_SEL

    def rows_t(new_ref, lo):
        return _pad_rows(new_ref[:, lo:lo + GD], PAGE_SIZE).T.astype(BF16)

    def attend_steps(name, kt_all, vt_all, tiles, before_masks=None):
        s_all = lax.dot_general(kt_all, qq, (((0,), (1,)), ((), ())), preferred_element_type=F32)
        yield
        if before_masks is not None:
            yield from before_masks
        sms = []
        for i, (k0, n_real, blk) in enumerate(tiles):
            s = s_all[i * PAGE_SIZE:(i + 1) * PAGE_SIZE]
            dist = tpos_l - (k0 + key_i).astype(F32)
            dist_m = dist if n_real == PAGE_SIZE else jnp.where(key_i < n_real, dist, -1.0)
            if blk is not None:
                sel_t = res["sel"]
                mask = jnp.concatenate([jnp.broadcast_to(sel_t[blk + a:blk + a + 1], (L_SEL, rows)) for a in range(bpp)],
                                       axis=0)
                ok = jnp.where(dist_m >= 0.0, mask, 0.0) > 0.5
            else:
                ok = jnp.abs(dist_m - 0.5 * WINDOW) <= 0.5 * WINDOW
            sms.append(jnp.where(ok, s - slope_l * dist, NEG))
            if i % 2 == 1:
                yield
        m = functools.reduce(jnp.maximum, [jnp.max(x, axis=0, keepdims=True) for x in sms])
        es = []
        for i, x in enumerate(sms):
            es.append(jnp.exp(x - m))
            if i % 4 == 3:
                yield
        l = functools.reduce(lambda a, b: a + b, [jnp.sum(e, axis=0, keepdims=True) for e in es])
        o_t = _dot(vt_all, jnp.concatenate(es, axis=0).astype(BF16))
        res[name] = (o_t / jnp.maximum(l, 1e-30)).T

    kt_slc = jnp.concatenate([pg[0, 0:GD, :].astype(BF16) for pg in pages] + [rows_t(kvn_ref, 0)], axis=1)
    vt_slc = jnp.concatenate([pg[0, GD:2 * GD, :].astype(BF16) for pg in pages] + [rows_t(kvn_ref, GD)], axis=1)
    wbuf = win_ref.shape[2]
    kt_win = jnp.concatenate([win_ref[0, 0:GD, :].astype(BF16), rows_t(winn_ref, 0)], axis=1)
    vt_win = jnp.concatenate([win_ref[0, GD:2 * GD, :].astype(BF16), rows_t(winn_ref, GD)], axis=1)
    _round_robin(
        attend_steps("o_slc", kt_slc, vt_slc,
                     [(pi * PAGE_SIZE, PAGE_SIZE, pi * bpp) for pi in range(len(pages))] + [(past, ts, past // L_SEL)],
                     before_masks=select_steps()),
        attend_steps("o_win", kt_win, vt_win,
                     [(past - wbuf + wi * PAGE_SIZE, PAGE_SIZE, None) for wi in range(wbuf // PAGE_SIZE)]
                     + [(past, ts, None)]))
    o_cmp, o_slc, o_win = res["o_cmp"], res["o_slc"], res["o_win"]

    hi, mid, lo = _split3(gate_ref[...])
    eg = eg_ref[...]
    gexp = _dot(hi, eg) + _dot(mid, eg) + _dot(lo, eg)

    def gate_rows(br):
        return jnp.concatenate([gexp[:, br * D_MODEL + h * GD:br * D_MODEL + (h + 1) * GD]
                                for _ in range(G_B) for h in range(HPG)], axis=0)

    comb = gate_rows(0) * o_cmp + gate_rows(1) * o_slc + gate_rows(2) * o_win
    outs = []
    for h in range(HPG):
        acc = None
        for g in range(G_B):
            piece = jnp.where(lane_g == g, comb[(g * HPG + h) * ts:(g * HPG + h + 1) * ts], 0.0)
            acc = piece if acc is None else acc + piece
        outs.append(acc)
    o_ref[...] = jnp.concatenate(outs, axis=1).astype(o_ref.dtype)


def _nsa_sample_call(page_tab, cache_t, slope_rows, q, gates, cmp_t, kvn, win_t, winn, eg, ovt, b, ts):
    npg = page_tab.shape[1]
    assert npg == N_PAGES_ARG
    past = npg * PAGE_SIZE
    page_specs = [pl.BlockSpec((1, 2 * GD, PAGE_SIZE), functools.partial(lambda bi, pt, p: (pt[bi * N_PAGES_ARG + p], 1, 0), p=p))
                  for p in range(npg)]
    wbuf = win_t.shape[2]
    grid_spec = pltpu.PrefetchScalarGridSpec(
        num_scalar_prefetch=1,
        grid=(b,),
        in_specs=page_specs + [
            pl.BlockSpec(slope_rows.shape, lambda bi, pt: (0, 0), pipeline_mode=pl.Buffered(1)),
            pl.BlockSpec((1, slope_rows.shape[0]), lambda bi, pt: (0, 0), pipeline_mode=pl.Buffered(1)),
            pl.BlockSpec((ts, D_MODEL), lambda bi, pt: (bi, 0)),
            pl.BlockSpec((ts, LANES), lambda bi, pt: (bi, 0)),
            pl.BlockSpec((1, 2, GD, cmp_t.shape[3]), lambda bi, pt: (bi, 0, 0, 0)),
            pl.BlockSpec((ts, 2 * GD), lambda bi, pt: (bi, 1)),
            pl.BlockSpec((1, 2 * GD, wbuf), lambda bi, pt: (bi, 0, 0)),
            pl.BlockSpec((ts, 2 * GD), lambda bi, pt: (bi, 0)),
            pl.BlockSpec(eg.shape, lambda bi, pt: (0, 0), pipeline_mode=pl.Buffered(1)),
            pl.BlockSpec(ovt.shape, lambda bi, pt: (0, 0), pipeline_mode=pl.Buffered(1))],
        out_specs=pl.BlockSpec((ts, D_MODEL), lambda bi, pt: (bi, 0)),
    )
    return pl.pallas_call(
        functools.partial(_nsa_sample_kernel, past=past, ts=ts),
        grid_spec=grid_spec,
        out_shape=jax.ShapeDtypeStruct((b * ts, D_MODEL), F32),
        compiler_params=_params("parallel"),
        name="nsa_sample",
    )(page_tab.reshape(-1), *([cache_t] * npg), slope_rows, slope_rows.reshape(1, -1), q, gates, cmp_t, kvn, win_t, winn, eg, ovt)


def _overlap_t(nsel):
    n = np.arange(LANES)[None, :] * S_CMP
    j = np.arange(LANES)[:, None] * L_SEL
    ov = (n <= j + L_SEL - 1) & (n + L_CMP - 1 >= j) & (np.arange(LANES)[None, :] < LANES - 1) & (np.arange(LANES)[:, None] < nsel)
    return jnp.asarray(ov.astype(np.float32), dtype=BF16)


def _gate_expand():
    e = np.zeros((LANES, 3 * D_MODEL), np.float32)
    for br in range(3):
        for hh in range(H_B):
            e[br * H_B + hh, br * D_MODEL + hh * DH_B:br * D_MODEL + (hh + 1) * DH_B] = 1.0
    return jnp.asarray(e, dtype=BF16)


def _cmp_weights(pe, w1, w2):
    r_n = L_CMP // S_CMP
    pe_t = jnp.tile(pe.reshape(r_n, S_CMP, 1, DH_B), (1, 1, G_B, 1)).reshape(r_n, S_CMP, GD)
    eye = jnp.eye(G_B, dtype=F32)
    w1r = w1.reshape(r_n, S_CMP, DH_B, CMP_HID)
    w1b = jnp.einsum("rsdh,ab->rsadbh", w1r, jnp.eye(2, dtype=F32)).reshape(r_n, S_CMP * 2 * DH_B, 2 * CMP_HID)
    w2bt = jnp.einsum("hd,gk->kdgh", w2, eye).reshape(GD, G_B * CMP_HID)
    return pe_t, w1b.astype(BF16), w2bt.astype(BF16)


def _rows_minor(x):
    b, r = x.shape[:2]
    return jnp.moveaxis(x, 1, -1).reshape(b, -1, r)


def _rows_major(x_t, feat_shape):
    b, _, r = x_t.shape
    return jnp.moveaxis(x_t.reshape((b,) + feat_shape + (r,)), -1, 1)


def kernel(x_prompt, x_sample, c_prompt, c_sample, state_ret, state_conv, cache_kv, state_win, page_table, w_ada, b_ada, g_mix, g_ffn, w_ffn_in, conv_w, conv_b, w_ffn_out, w_ret_in, w_ret_out, g_kv, w_ada_kv, b_ada_kv, w_kv, pe_ck, pe_cv, w_ck1, w_ck2, w_cv1, w_cv2, w_nsa_in, w_nsa_out, g_final):
    bp, tp, d = x_prompt.shape
    bs, ts, _ = x_sample.shape
    depth = w_ada.shape[0]
    assert depth == 2 and d == D_MODEL and tp % SLC_TILE == 0 and ts == 8
    npg = page_table.shape[1]
    past = npg * PAGE_SIZE
    f2 = 2 * D_FF
    hd = H_B * DH_B

    w_ada_all = jnp.concatenate([w_ada[0], w_ada[1], w_ada_kv], axis=1).astype(BF16)
    b_ada_all = jnp.concatenate([b_ada[0], b_ada[1], b_ada_kv], axis=0).reshape(1, -1)
    w_ret_in_b = w_ret_in[0].astype(BF16)
    w_ret_out_b = w_ret_out[0].astype(BF16)
    w_ffn_in_b = w_ffn_in.astype(BF16)
    w_ffn_out_b = w_ffn_out.astype(BF16)
    w_kv_b = w_kv.astype(BF16)
    w_kv_t = w_kv_b.T
    w_kv_rows, w_kv_win = w_kv_b[:, :4 * GD], w_kv_b[:, 4 * GD:]
    wn = w_nsa_in[0]
    wq_s = wn[:, :hd] * (DH_B ** -0.5)
    w_q_pad = jnp.pad((wq_s * LOG2E).reshape(d, H_B, DH_B),
                      ((0, 0), (0, 0), (0, A_W - DH_B))).reshape(d, H_B * A_W).astype(BF16)
    w_g_grp = jnp.pad(wn[:, hd:].reshape(d, G_B, HPG, 3).transpose(0, 1, 3, 2).reshape(d, G_B, 3 * HPG),
                      ((0, 0), (0, 0), (0, LANES - 3 * HPG))).reshape(d, G_B * LANES).astype(BF16)
    w_q_s = wq_s.reshape(d, G_B, HPG, DH_B).transpose(0, 2, 1, 3).reshape(d, hd).astype(BF16)
    w_g_s = jnp.pad(wn[:, hd:].reshape(d, G_B, HPG, 3).transpose(0, 3, 2, 1).reshape(d, 3 * H_B),
                    ((0, 0), (0, LANES - 3 * H_B))).astype(BF16)
    w_o_b = w_nsa_out[0].astype(BF16)
    w_o_s = w_nsa_out[0].reshape(G_B, HPG, DH_B, d).transpose(1, 0, 2, 3).reshape(hd, d).astype(BF16)
    pe_k, w1_k, w2_k = _cmp_weights(pe_ck, w_ck1, w_ck2)
    pe_v, w1_v, w2_v = _cmp_weights(pe_cv, w_cv1, w_cv2)
    pe_c = jnp.stack([pe_k, pe_v])
    w1_c = jnp.stack([w1_k, w1_v])
    w2t_c = jnp.stack([w2_k, w2_v])
    slopes = jnp.exp2(-8.0 * jnp.arange(1, H_B + 1, dtype=F32) / H_B)
    log_g = jnp.log1p(-jnp.exp2(-5.0 - jnp.arange(H_A, dtype=F32)))
    lg_tab = jnp.broadcast_to(log_g[:, None, None], (H_A, 1, LANES))

    c_all = jnp.concatenate([c_prompt, c_sample], axis=0)
    mod = _ada_call(c_all, w_ada_all, b_ada_all)

    def mods(table, per_row):
        def cols(k):
            return (table, k, per_row)
        return [[cols(l * 6 + k) for k in range(6)] for l in range(depth)] + [[cols(12), cols(13)]]

    def layer0(x, mod_l, s_ret, conv_buf, b, t, tm, act_dt):
        sh1, sc1, ga1, sh2, sc2, ga2 = mod_l
        tm_in = 2 * tm if t % (2 * tm) == 0 else tm
        pr = _nmm_call(x, g_mix[0], sh1, sc1, w_ret_in_b, t, tm_in, 1536, act_dt, name="ret_in")
        og, s_new = _ret_call(pr, s_ret, lg_tab, b, t, act_dt)
        x = _mm_res_call(og, w_ret_out_b, x, ga1, t, tm, name="ret_out")
        x, conv0 = _ffn_call(x, g_ffn[0], sh2, sc2, ga2, w_ffn_in_b[0], conv_w[0], conv_b[0], conv_buf,
                             w_ffn_out_b[0], b, t, tm, name="ffn0")
        return x, s_new, conv0

    def layer1_tail(x, o, w_o, mod_l, conv_buf, b, t, tm):
        _, _, ga1, sh2, sc2, ga2 = mod_l
        x = _mm_res_call(o, w_o, x, ga1, t, tm, name="nsa_out")
        return _ffn_call(x, g_ffn[1], sh2, sc2, ga2, w_ffn_in_b[1], conv_w[1], conv_b[1], conv_buf,
                         w_ffn_out_b[1], b, t, tm, g_final=g_final, name="ffn1")

    tm = 512
    mp = mods(mod[0:bp], False)
    x = x_prompt.reshape(bp * tp, d)
    x, ret_p, conv0_p = layer0(x, mp[0], None, jnp.zeros((bp, CONV_W - 1, f2), F32), bp, tp, tm, BF16)
    sh1, sc1 = mp[1][0], mp[1][1]
    kv_t, win_t = _kvt_call(x, g_kv, mp[2][0], mp[2][1], w_kv_t, bp, tp, tm)
    q = _nmm_call(x, g_mix[1], sh1, sc1, w_q_pad, tp, tm, hd, BF16, name="q_proj")
    gates = _nmm_call(x, g_mix[1], sh1, sc1, w_g_grp, tp, tm, G_B * LANES, F32, act="sigmoid", name="gate_proj")
    assert tp == N_PAGES_ARG * PAGE_SIZE
    ident = jnp.zeros((bp, N_PAGES_ARG), jnp.int32)
    cmp_t = _cmp_call(kv_t, lambda bi, p, pt: (bi, 0, p), ident, pe_c, w1_c, w2t_c)
    o = _nsa_prompt_call(q, gates, cmp_t, kv_t, win_t, _nsa_prompt_consts(tp), bp, tp)
    y_p, conv1_p = layer1_tail(x, o, w_o_b, mp[1], jnp.zeros((bp, CONV_W - 1, f2), F32), bp, tp, tm)
    kv_p = _rows_major(kv_t, (4, G_B, DH_B))
    win_p = _rows_major(win_t[:, :, tp - min(WINDOW, tp):], (2, G_B, DH_B))

    tm = 256
    ms = mods(jnp.repeat(mod[bp:bp + bs], ts, axis=0), True)
    x = x_sample.reshape(bs * ts, d)
    x, ret_s, conv0_s = layer0(x, ms[0], state_ret[0].astype(F32), state_conv[0], bs, ts, tm, F32)
    sh1, sc1 = ms[1][0], ms[1][1]
    kv_rows = _nmm_call(x, g_kv, ms[2][0], ms[2][1], w_kv_rows, ts, tm, 4 * GD, F32, name="kv_rows")
    win_rows = _nmm_call(x, g_kv, ms[2][0], ms[2][1], w_kv_win, ts, tm, 2 * GD, F32, name="win_rows")
    q = _nmm_call(x, g_mix[1], sh1, sc1, w_q_s, ts, tm, hd, F32, name="q_proj")
    gates = _nmm_call(x, g_mix[1], sh1, sc1, w_g_s, ts, tm, LANES, F32, act="sigmoid", name="gate_proj")
    cache_t = _rows_minor(cache_kv)
    state_win_t = _rows_minor(state_win)
    cmp_t = _cmp_call(cache_t, lambda bi, p, pt: (pt[bi * N_PAGES_ARG + p], 0, 0), page_table, pe_c, w1_c, w2t_c)
    slope_rows = jnp.repeat(slopes, ts).reshape(H_B * ts, 1)
    o = _nsa_sample_call(page_table, cache_t, slope_rows, q, gates, cmp_t, kv_rows, state_win_t, win_rows,
                         _gate_expand(), _overlap_t(-(-(past + ts) // L_SEL)), bs, ts)
    y_s, conv1_s = layer1_tail(x, o, w_o_s, ms[1], state_conv[1], bs, ts, tm)
    kv_s = kv_rows.reshape(bs, ts, 4, G_B, DH_B)
    win_new_t = jnp.concatenate([state_win_t, jnp.swapaxes(win_rows.reshape(bs, ts, 2 * GD), 1, 2)], axis=2)
    win_s = _rows_major(win_new_t[:, :, win_new_t.shape[2] - min(WINDOW, win_new_t.shape[2]):], (2, G_B, DH_B))

    return (y_p.reshape(bp, tp, d), y_s.reshape(bs, ts, d), ret_p[None], ret_s[None],
            jnp.stack([conv0_p, conv1_p]), jnp.stack([conv0_s, conv1_s]), kv_p, kv_s, win_p, win_s)
```

```python
import functools

import numpy as np
import jax
import jax.numpy as jnp
from jax import lax
from jax.experimental import pallas as pl
from jax.experimental.pallas import tpu as pltpu

F32 = jnp.float32
BF16 = jnp.bfloat16

D_MODEL = 1024
H_A = 4
DK_A = D_MODEL // H_A
DV_A = 2 * DK_A
RET_CHUNK = 128
H_B = 16
DH_B = D_MODEL // H_B
G_B = 4
HPG = H_B // G_B
GD = G_B * DH_B
L_CMP = 32
S_CMP = 16
L_SEL = 64
N_SEL = 16
WINDOW = 512
CMP_HID = 2 * DH_B
Q_BLOCK = 128
FORCE_BONUS = 1e4
D_FF = 2816
CONV_W = 3
EPS = 1e-6
PAGE_SIZE = 128
NEG = -1e30
BIG = 2.0 ** 100
M_INIT = -(2.0 ** 101)
LOG2E = 1.4426950408889634

LANES = 128
SUB = 8
VMEM_LIMIT = 56 * 1024 * 1024
FF_CHUNK = 256
SLC_TILE = 512
SM_ROWS = 64
N_PAGES_ARG = 16

F_POS = DH_B
F_ONE = DH_B + 6
F_SEL = DH_B + 8
A_W = 2 * DH_B

_NT = (((1,), (1,)), ((), ()))


def _sigmoid(x):
    return 1.0 / (1.0 + jnp.exp(-x))


def _silu(x):
    return x * _sigmoid(x)


def _dot(a, b):
    return jnp.dot(a, b, preferred_element_type=F32)


def _dot_nt(a, b):
    return lax.dot_general(a, b, _NT, preferred_element_type=F32)


def _split3(x):
    hi = x.astype(BF16)
    r = x - hi.astype(F32)
    mid = r.astype(BF16)
    lo = (r - mid.astype(F32)).astype(BF16)
    return hi, mid, lo


def _params(*sem):
    return pltpu.CompilerParams(dimension_semantics=sem, vmem_limit_bytes=VMEM_LIMIT)


def _const_spec(shape):
    nd = len(shape)
    return pl.BlockSpec(shape, lambda *a: (0,) * nd, pipeline_mode=pl.Buffered(1))


def _mod_arg(v, m, tm, rows_per_batch):
    table, col, per_row = v
    d = D_MODEL
    if per_row:
        assert table.shape[0] == m and m % tm == 0
        return table.reshape(m // tm, tm, table.shape[1]), (1, tm, d), (lambda i: (i, 0, col))
    assert rows_per_batch % tm == 0
    tpb = rows_per_batch // tm
    return table.reshape(table.shape[0], 1, table.shape[1]), (1, 1, d), (lambda i: (i // tpb, 0, col))


def _ada_kernel(c_ref, w_ref, b_ref, o_ref):
    c = c_ref[...]
    o_ref[...] = _dot(_silu(c).astype(BF16), w_ref[...]) + b_ref[...]


def _ada_call(c, w, b, tn=2048):
    m, d = c.shape
    n = w.shape[1]
    assert n % tn == 0
    return pl.pallas_call(
        _ada_kernel,
        grid=(n // tn,),
        in_specs=[pl.BlockSpec((m, d), lambda j: (0, 0)),
                  pl.BlockSpec((d, tn), lambda j: (0, j)),
                  pl.BlockSpec((1, tn), lambda j: (0, j))],
        out_specs=pl.BlockSpec((m, tn), lambda j: (0, j)),
        out_shape=jax.ShapeDtypeStruct((m, n), F32),
        compiler_params=_params("parallel"),
        name="ada",
    )(c, w, b)


def _norm_mod(x, g, sh, sc):
    y = x * lax.rsqrt(jnp.mean(x * x, axis=-1, keepdims=True) + EPS) * g
    return y * (1.0 + sc) + sh


def _nmm_kernel(x_ref, g_ref, sh_ref, sc_ref, w_ref, o_ref, h_scr, *, act):
    @pl.when(pl.program_id(1) == 0)
    def _():
        h_scr[...] = _norm_mod(x_ref[...], g_ref[...], sh_ref[0], sc_ref[0]).astype(BF16)

    acc = _dot(h_scr[...], w_ref[...])
    if act == "sigmoid":
        acc = _sigmoid(acc)
    o_ref[...] = acc.astype(o_ref.dtype)


def _nmm_call(x, g, sh, sc, w, rows_per_batch, tm, tn, out_dtype, act=None, name="nmm"):
    m, d = x.shape
    n = w.shape[1]
    assert m % tm == 0 and n % tn == 0
    sh_a, mshape, sh_map = _mod_arg(sh, m, tm, rows_per_batch)
    sc_a, _, sc_map = _mod_arg(sc, m, tm, rows_per_batch)
    return pl.pallas_call(
        functools.partial(_nmm_kernel, act=act),
        grid=(m // tm, n // tn),
        in_specs=[pl.BlockSpec((tm, d), lambda i, j: (i, 0)),
                  pl.BlockSpec((1, d), lambda i, j: (0, 0)),
                  pl.BlockSpec(mshape, lambda i, j: sh_map(i)),
                  pl.BlockSpec(mshape, lambda i, j: sc_map(i)),
                  pl.BlockSpec((d, tn), lambda i, j: (0, j))],
        out_specs=pl.BlockSpec((tm, tn), lambda i, j: (i, j)),
        out_shape=jax.ShapeDtypeStruct((m, n), out_dtype),
        scratch_shapes=[pltpu.VMEM((tm, d), BF16)],
        compiler_params=_params("parallel", "arbitrary"),
        name=name,
    )(x, g.reshape(1, d), sh_a, sc_a, w)


def _kvt_kernel(x_ref, g_ref, sh_ref, sc_ref, wt_ref, kv_ref, win_ref):
    h = _norm_mod(x_ref[...], g_ref[...], sh_ref[0], sc_ref[0]).astype(BF16)
    n_kv = kv_ref.shape[1]
    kv_ref[0] = _dot_nt(wt_ref[0:n_kv, :], h)
    win_ref[0] = _dot_nt(wt_ref[n_kv:, :], h)


def _kvt_call(x, g, sh, sc, w_t, b, t, tm):
    m, d = x.shape
    n = w_t.shape[0]
    nt = t // tm
    sh_a, mshape, sh_map = _mod_arg(sh, m, tm, t)
    sc_a, _, sc_map = _mod_arg(sc, m, tm, t)
    return pl.pallas_call(
        _kvt_kernel,
        grid=(b, nt),
        in_specs=[pl.BlockSpec((tm, d), lambda bi, ti: (bi * nt + ti, 0)),
                  pl.BlockSpec((1, d), lambda bi, ti: (0, 0)),
                  pl.BlockSpec(mshape, lambda bi, ti: sh_map(bi * nt + ti)),
                  pl.BlockSpec(mshape, lambda bi, ti: sc_map(bi * nt + ti)),
                  _const_spec((n, d))],
        out_specs=[pl.BlockSpec((1, 4 * GD, tm), lambda bi, ti: (bi, 0, ti)),
                   pl.BlockSpec((1, 2 * GD, tm), lambda bi, ti: (bi, 0, ti))],
        out_shape=[jax.ShapeDtypeStruct((b, 4 * GD, t), F32),
                   jax.ShapeDtypeStruct((b, 2 * GD, t), F32)],
        compiler_params=_params("parallel", "parallel"),
        name="kv_proj_t",
    )(x, g.reshape(1, d), sh_a, sc_a, w_t)


def _mm_res_kernel(a_ref, w_ref, res_ref, ga_ref, o_ref):
    y = _dot(a_ref[...].astype(BF16), w_ref[...])
    o_ref[...] = res_ref[...] + ga_ref[0] * y


def _mm_res_call(a, w, res, gate, rows_per_batch, tm, name="mm_res"):
    m, k = a.shape
    d = w.shape[1]
    ga_a, mshape, mmap = _mod_arg(gate, m, tm, rows_per_batch)
    return pl.pallas_call(
        _mm_res_kernel,
        grid=(m // tm,),
        in_specs=[pl.BlockSpec((tm, k), lambda i: (i, 0)),
                  _const_spec((k, d)),
                  pl.BlockSpec((tm, d), lambda i: (i, 0)),
                  pl.BlockSpec(mshape, lambda i: mmap(i))],
        out_specs=pl.BlockSpec((tm, d), lambda i: (i, 0)),
        out_shape=jax.ShapeDtypeStruct((m, d), F32),
        compiler_params=_params("parallel"),
        name=name,
    )(a, w, res, ga_a)


def _ret_kernel(*refs, c, nc, has_s0):
    if has_s0:
        lg_ref, q_ref, k_ref, v_ref, g_ref, s0_ref, o_ref, so_ref, s_scr = refs
    else:
        lg_ref, q_ref, k_ref, v_ref, g_ref, o_ref, so_ref, s_scr = refs
        s0_ref = None
    n = pl.program_id(1)

    @pl.when(n == 0)
    def _():
        if has_s0:
            s_scr[...] = s0_ref[0]
        else:
            s_scr[...] = jnp.zeros_like(s_scr)

    cp = max(c, RET_CHUNK)

    def padded(a):
        a = a.astype(F32)
        if cp > c:
            a = jnp.concatenate([a, jnp.zeros((cp - c, a.shape[1]), F32)], axis=0)
        return a

    i = lax.broadcasted_iota(jnp.int32, (cp, 1), 0).astype(F32)
    j = lax.broadcasted_iota(jnp.int32, (1, cp), 1).astype(F32)
    diff = i - j
    s_olds = [s_scr[h] for h in range(H_A)]
    s_news = [None] * H_A

    def head_steps(h):
        lg = lg_ref[h][:, 0:1]
        q = padded(q_ref[:, h * DK_A:(h + 1) * DK_A]).astype(BF16)
        kf = padded(k_ref[:, h * DK_A:(h + 1) * DK_A]) * (DK_A ** -0.5)
        v = padded(v_ref[:, h * DV_A:(h + 1) * DV_A]).astype(BF16)
        dmask = jnp.where(diff >= 0, jnp.exp(jnp.maximum(diff, 0.0) * lg), 0.0)
        scores = _dot_nt(q, kf.astype(BF16)) * dmask
        yield
        s_old = s_olds[h]
        o = _dot(scores.astype(BF16), v) + _dot(q, s_old.astype(BF16)) * jnp.exp((i + 1.0) * lg)
        yield
        w = jnp.exp((c - 1.0 - i) * lg)
        kw_t = (kf * w).T.astype(BF16)
        s_news[h] = jnp.exp(c * lg) * s_old + _dot(kw_t, v)
        yield
        of = o * lax.rsqrt(jnp.mean(o * o, axis=-1, keepdims=True) + EPS)
        gate = g_ref[:, h * DV_A:(h + 1) * DV_A].astype(F32)
        o_ref[:, h * DV_A:(h + 1) * DV_A] = (of[0:c] * _silu(gate)).astype(o_ref.dtype)

    _round_robin(*[head_steps(h) for h in range(H_A)])
    for h in range(H_A):
        s_scr[h] = s_news[h]

    @pl.when(n == nc - 1)
    def _():
        so_ref[0] = s_scr[...]


def _ret_call(pr, s0, lg_tab, b, t, out_dtype):
    c = RET_CHUNK if t % RET_CHUNK == 0 else t
    nc = t // c
    has_s0 = s0 is not None
    nq, nv = H_A * DK_A, H_A * DV_A
    assert nv == 2 * nq
    in_specs = [pl.BlockSpec((H_A, 1, LANES), lambda bi, n: (0, 0, 0)),
                pl.BlockSpec((c, nq), lambda bi, n: (bi * nc + n, 0)),
                pl.BlockSpec((c, nq), lambda bi, n: (bi * nc + n, 1)),
                pl.BlockSpec((c, nv), lambda bi, n: (bi * nc + n, 1)),
                pl.BlockSpec((c, nv), lambda bi, n: (bi * nc + n, 2))]
    args = [lg_tab, pr, pr, pr, pr]
    if has_s0:
        in_specs.append(pl.BlockSpec((1, H_A, DK_A, DV_A), lambda bi, n: (bi, 0, 0, 0)))
        args.append(s0)
    return pl.pallas_call(
        functools.partial(_ret_kernel, c=c, nc=nc, has_s0=has_s0),
        grid=(b, nc),
        in_specs=in_specs,
        out_specs=[pl.BlockSpec((c, nv), lambda bi, n: (bi * nc + n, 0)),
                   pl.BlockSpec((1, H_A, DK_A, DV_A), lambda bi, n: (bi, 0, 0, 0))],
        out_shape=[jax.ShapeDtypeStruct((b * t, nv), out_dtype),
                   jax.ShapeDtypeStruct((b, H_A, DK_A, DV_A), F32)],
        scratch_shapes=[pltpu.VMEM((H_A, DK_A, DV_A), F32)],
        compiler_params=_params("parallel", "arbitrary"),
        name="retention",
    )(*args)


def _ffn_kernel(*refs, t_in, nbt, final):
    if final:
        (x_ref, g_ref, sh_ref, sc_ref, ga_ref, win_ref, cw_ref, cb_ref, cbuf_ref, wout_ref, gf_ref,
         o_ref, nc_ref, h_scr, carry_scr, act_scr, shift_scr) = refs
    else:
        (x_ref, g_ref, sh_ref, sc_ref, ga_ref, win_ref, cw_ref, cb_ref, cbuf_ref, wout_ref,
         o_ref, nc_ref, h_scr, carry_scr, act_scr, shift_scr) = refs
        gf_ref = None
    tm = x_ref.shape[0]

    @pl.when(pl.program_id(1) == 0)
    def _():
        carry_scr[...] = cbuf_ref[...]

    x = x_ref[...]
    h_scr[...] = _norm_mod(x, g_ref[...], sh_ref[0], sc_ref[0]).astype(BF16)
    cwd = FF_CHUNK
    tpos = lax.broadcasted_iota(jnp.int32, (nbt, t_in, cwd), 1)
    is0 = tpos == 0
    is1 = tpos == 1

    def conv_rolled(off):
        u = _dot(h_scr[...], win_ref[:, off:off + cwd]).reshape(nbt, t_in, cwd)
        prev = carry_scr[:, :, off:off + cwd]
        p0 = prev[:, 0:1, :]
        p1 = prev[:, 1:2, :]
        r1 = pltpu.roll(u, 1, axis=1)
        r2 = pltpu.roll(u, 2, axis=1)
        um1 = jnp.where(is0, p1, r1)
        um2 = jnp.where(is0, p0, jnp.where(is1, p1, r2))
        newc = r2[:, 0:2, :]
        carry_scr[:, :, off:off + cwd] = newc
        nc_ref[:, :, off:off + cwd] = newc
        cw = cw_ref[:, off:off + cwd]
        z = cb_ref[:, off:off + cwd] + cw[0:1] * um2
        z = z + cw[1:2] * um1
        z = z + cw[2:3] * u
        return z.reshape(tm, cwd)

    def up_shifted(off, slot):
        buf = shift_scr.at[slot]
        buf[SUB:SUB + tm, :] = _dot(h_scr[...], win_ref[:, off:off + cwd])
        buf[SUB - 2:SUB, :] = carry_scr[0, :, off:off + cwd]
        newc = buf[SUB + tm - 2:SUB + tm, :]
        carry_scr[0, :, off:off + cwd] = newc
        nc_ref[0, :, off:off + cwd] = newc

    def conv_shifted(off, slot):
        buf = shift_scr.at[slot]
        cw = cw_ref[:, off:off + cwd]
        z = cb_ref[:, off:off + cwd] + cw[0:1] * buf[SUB - 2:SUB - 2 + tm, :]
        z = z + cw[1:2] * buf[SUB - 1:SUB - 1 + tm, :]
        z = z + cw[2:3] * buf[SUB:SUB + tm, :]
        return z

    n_slots = shift_scr.shape[0]
    n_ch = D_FF // cwd

    def up_chunk(ch):
        up_shifted(ch * cwd, (2 * ch) % n_slots)
        up_shifted(D_FF + ch * cwd, (2 * ch + 1) % n_slots)

    if nbt == 1:
        up_chunk(0)
    for ch in range(n_ch):
        if nbt == 1:
            if ch + 1 < n_ch:
                up_chunk(ch + 1)
            za = conv_shifted(ch * cwd, (2 * ch) % n_slots)
            zg = conv_shifted(D_FF + ch * cwd, (2 * ch + 1) % n_slots)
        else:
            za = conv_rolled(ch * cwd)
            zg = conv_rolled(D_FF + ch * cwd)
        act_scr[:, ch * cwd:(ch + 1) * cwd] = (_silu(zg) * za).astype(BF16)
    xn = x + ga_ref[0] * _dot(act_scr[...], wout_ref[...])
    if final:
        xn = xn * lax.rsqrt(jnp.mean(xn * xn, axis=-1, keepdims=True) + EPS) * gf_ref[...]
    o_ref[...] = xn


def _ffn_call(x, g, sh, sc, ga, w_in, cw, cb, cbuf, w_out, b, t, tm, g_final=None, name="ffn"):
    m, d = x.shape
    f2 = w_in.shape[1]
    if t % tm == 0:
        t_in, nbt, nb_tiles, nt = tm, 1, b, t // tm
    else:
        assert tm % t == 0 and m % tm == 0
        t_in, nbt, nb_tiles, nt = t, tm // t, m // tm, 1
    sh_a, mshape, sh_map = _mod_arg(sh, m, tm, t)
    sc_a, _, sc_map = _mod_arg(sc, m, tm, t)
    ga_a, _, ga_map = _mod_arg(ga, m, tm, t)
    final = g_final is not None
    in_specs = [pl.BlockSpec((tm, d), lambda bi, ti: (bi * nt + ti, 0)),
                pl.BlockSpec((1, d), lambda bi, ti: (0, 0)),
                pl.BlockSpec(mshape, lambda bi, ti: sh_map(bi * nt + ti)),
                pl.BlockSpec(mshape, lambda bi, ti: sc_map(bi * nt + ti)),
                pl.BlockSpec(mshape, lambda bi, ti: ga_map(bi * nt + ti)),
                _const_spec((d, f2)),
                _const_spec((CONV_W, f2)),
                _const_spec((1, f2)),
                pl.BlockSpec((nbt, CONV_W - 1, f2), lambda bi, ti: (bi, 0, 0)),
                _const_spec((D_FF, d))]
    args = [x, g.reshape(1, d), sh_a, sc_a, ga_a, w_in, cw, cb.reshape(1, f2), cbuf, w_out]
    if final:
        in_specs.append(pl.BlockSpec((1, d), lambda bi, ti: (0, 0)))
        args.append(g_final.reshape(1, d))
    return pl.pallas_call(
        functools.partial(_ffn_kernel, t_in=t_in, nbt=nbt, final=final),
        grid=(nb_tiles, nt),
        in_specs=in_specs,
        out_specs=[pl.BlockSpec((tm, d), lambda bi, ti: (bi * nt + ti, 0)),
                   pl.BlockSpec((nbt, CONV_W - 1, f2), lambda bi, ti: (bi, 0, 0))],
        out_shape=[jax.ShapeDtypeStruct((m, d), F32),
                   jax.ShapeDtypeStruct((b, CONV_W - 1, f2), F32)],
        scratch_shapes=[pltpu.VMEM((tm, d), BF16),
                        pltpu.VMEM((nbt, CONV_W - 1, f2), F32),
                        pltpu.VMEM((tm, D_FF), BF16),
                        pltpu.VMEM((4, SUB + tm, FF_CHUNK), F32)],
        compiler_params=_params("parallel", "arbitrary"),
        name=name,
    )(*args)


def _cmp_kernel(pt_ref, *refs):
    pages = refs[:N_PAGES_ARG]
    pe_ref, w1_ref, w2t_ref, o_ref = refs[N_PAGES_ARG:]
    npc = PAGE_SIZE // S_CMP
    r_n = L_CMP // S_CMP
    for kind in range(2):
        rows = jnp.concatenate([pg[0, kind * GD:(kind + 1) * GD, :].T for pg in pages], axis=0)
        rows = jnp.swapaxes(rows.reshape(len(pages) * npc, S_CMP, GD), 0, 1)
        pieces = [rows[s] for s in range(S_CMP)]
        z = None
        for r in range(r_n):
            ys = []
            for gp in range(G_B // 2):
                ls = slice(gp * LANES, (gp + 1) * LANES)
                a = jnp.concatenate([(pieces[s][:, ls] + pe_ref[kind, r, s:s + 1, ls]).astype(BF16)
                                     for s in range(S_CMP)], axis=1)
                ys.append(_dot(a, w1_ref[kind, r]))
            y = jnp.concatenate(ys, axis=1)
            if r > 0:
                y = pltpu.roll(y, y.shape[0] - r, axis=0)
            z = y if z is None else z + y
        o_ref[0, kind] = _dot_nt(w2t_ref[kind], _silu(z).astype(BF16)).astype(o_ref.dtype)


def _cmp_call(rows_t, page_map, page_tab, pe, w1, w2t):
    b, npg = page_tab.shape
    assert npg == N_PAGES_ARG
    n_pieces = npg * PAGE_SIZE // S_CMP
    page_specs = [pl.BlockSpec((1, 2 * GD, PAGE_SIZE), functools.partial(lambda bi, pt, p: page_map(bi, p, pt), p=p))
                  for p in range(npg)]
    grid_spec = pltpu.PrefetchScalarGridSpec(
        num_scalar_prefetch=1,
        grid=(b,),
        in_specs=page_specs + [
            pl.BlockSpec(pe.shape, lambda bi, pt: (0, 0, 0, 0), pipeline_mode=pl.Buffered(1)),
            pl.BlockSpec(w1.shape, lambda bi, pt: (0, 0, 0, 0), pipeline_mode=pl.Buffered(1)),
            pl.BlockSpec(w2t.shape, lambda bi, pt: (0, 0, 0), pipeline_mode=pl.Buffered(1))],
        out_specs=pl.BlockSpec((1, 2, GD, n_pieces), lambda bi, pt: (bi, 0, 0, 0)),
    )
    return pl.pallas_call(
        _cmp_kernel,
        grid_spec=grid_spec,
        out_shape=jax.ShapeDtypeStruct((b, 2, GD, n_pieces), BF16),
        compiler_params=_params("parallel"),
        name="compress",
    )(page_tab.reshape(-1), *([rows_t] * npg), pe, w1, w2t)


def _masked_softmax(s, ok, base2=False):
    sm = jnp.where(ok, s, NEG)
    m = jnp.max(sm, axis=-1, keepdims=True)
    e = jnp.where(ok, (jnp.exp2 if base2 else jnp.exp)(sm - m), 0.0)
    return e / jnp.maximum(jnp.sum(e, axis=-1, keepdims=True), 1e-30)


def _topk_mask_steps(imp_t, cur, nsel, ncol):
    rows = imp_t.shape[0]
    jj = lax.broadcasted_iota(jnp.int32, (rows, ncol), 0)
    forced = (jj == 0) | (jj == cur) | (jj == cur - 1)
    score = jnp.where(jj <= cur, imp_t + FORCE_BONUS * forced.astype(F32), NEG)
    rank = jnp.zeros((rows, ncol), F32)
    for i in range(nsel):
        row = score[i:i + 1, :]
        lower = jnp.where(jj > i, 1.0, 0.0)
        rank = rank + jnp.where(row > score, 1.0, jnp.where(row == score, lower, 0.0))
        if i % 8 == 7 and i + 1 < nsel:
            yield None
    yield jnp.where((rank < float(min(N_SEL, nsel))) & (jj < nsel), 1.0, 0.0)


def _topk_mask_t(imp_t, cur, nsel, ncol):
    out = None
    for out in _topk_mask_steps(imp_t, cur, nsel, ncol):
        pass
    return out


def _round_robin(*streams):
    live = list(streams)
    while live:
        for s in list(live):
            try:
                next(s)
            except StopIteration:
                live.remove(s)


def _eye_bf16(n):
    return jnp.where(lax.broadcasted_iota(jnp.int32, (n, n), 0) == lax.broadcasted_iota(jnp.int32, (n, n), 1),
                     1.0, 0.0).astype(BF16)


def _pad_rows(a, rows):
    if a.shape[0] == rows:
        return a
    return jnp.concatenate([a, jnp.zeros((rows - a.shape[0],) + a.shape[1:], a.dtype)], axis=0)


def _online_step(sm, v, m_old, l_old, acc_old, v_transposed):
    m_new = jnp.maximum(m_old, jnp.max(sm, axis=-1, keepdims=True))
    alpha = jnp.exp(m_old - m_new)
    e = jnp.where(sm > 0.5 * NEG, jnp.exp(sm - m_new), 0.0)
    l_new = alpha * l_old + jnp.sum(e, axis=-1, keepdims=True)
    pv = _dot_nt(e.astype(BF16), v) if v_transposed else _dot(e.astype(BF16), v)
    return m_new, l_new, alpha * acc_old + pv


def _block_importance_t(ps_sum, ovt):
    hi, mid, lo = _split3(ps_sum)
    return _dot_nt(ovt, hi) + _dot_nt(ovt, mid) + _dot_nt(ovt, lo)


def _nsa_prompt_kernel(q_ref, gate_ref, cmp_ref, kv_ref, win_ref, featc_ref, fslc_ref, fwin_ref, fcmp_ref,
                       ovt_ref, band_ref, cbias_ref, o_ref,
                       kb_slc, vt_slc, kb_win, vt_win, kb_cmp, a_scr, s_scr, e_scr, m_scr, acc_scr, gb_scr, *, t_len):
    qb = pl.program_id(1)
    s0 = qb * Q_BLOCK
    nsel = t_len // L_SEL
    n_st = t_len // SLC_TILE
    n_wt = WINDOW // Q_BLOCK
    n_wtiles = kb_win.shape[1]
    rows_g = HPG * Q_BLOCK

    def ones_rows(width):
        r = lax.broadcasted_iota(jnp.int32, (A_W - DH_B, width), 0)
        return jnp.where(r == 0, 1.0, 0.0).astype(BF16)

    @pl.when(qb == 0)
    def _():
        for g in range(G_B):
            for kt in range(n_st):
                cs = slice(kt * SLC_TILE, (kt + 1) * SLC_TILE)
                kb_slc[g, kt, 0:DH_B, :] = kv_ref[0, g * DH_B:(g + 1) * DH_B, cs].astype(BF16)
                kb_slc[g, kt, DH_B:A_W, :] = fslc_ref[:, cs]
            for wt in range(n_wtiles):
                cs = slice((wt - n_wt) * Q_BLOCK, (wt - n_wt + 1) * Q_BLOCK)
                if wt < n_wt:
                    kb_win[g, wt, 0:DH_B, :] = jnp.zeros((DH_B, Q_BLOCK), BF16)
                else:
                    kb_win[g, wt, 0:DH_B, :] = win_ref[0, g * DH_B:(g + 1) * DH_B, cs].astype(BF16)
                kb_win[g, wt, DH_B:A_W, :] = fwin_ref[:, wt * Q_BLOCK:(wt + 1) * Q_BLOCK]
            kb_cmp[g, 0:DH_B, :] = cmp_ref[0, 0, g * DH_B:(g + 1) * DH_B, :]
            kb_cmp[g, DH_B:A_W, :] = fcmp_ref[...]
            for kt in range(n_st):
                cs = slice(kt * SLC_TILE, (kt + 1) * SLC_TILE)
                vt_slc[g, kt, 0:DH_B, :] = kv_ref[0, GD + g * DH_B:GD + (g + 1) * DH_B, cs].astype(BF16)
                vt_slc[g, kt, DH_B:A_W, :] = ones_rows(SLC_TILE)
            for wt in range(n_wtiles):
                cs = slice((wt - n_wt) * Q_BLOCK, (wt - n_wt + 1) * Q_BLOCK)
                if wt < n_wt:
                    vt_win[g, wt, 0:DH_B, :] = jnp.zeros((DH_B, Q_BLOCK), BF16)
                else:
                    vt_win[g, wt, 0:DH_B, :] = win_ref[0, GD + g * DH_B:GD + (g + 1) * DH_B, cs].astype(BF16)
                vt_win[g, wt, DH_B:A_W, :] = ones_rows(Q_BLOCK)

    lane = lax.broadcasted_iota(jnp.int32, (1, A_W), 1)
    ags = [jnp.concatenate([jnp.where(lane < DH_B, q_ref[:, (g * HPG + hl) * A_W:(g * HPG + hl + 1) * A_W],
                                      featc_ref[g * HPG + hl:g * HPG + hl + 1, :].astype(BF16))
                            for hl in range(HPG)], axis=0) for g in range(G_B)]

    ovt = ovt_ref[...]
    eye = _eye_bf16(Q_BLOCK)
    t_loc = lax.broadcasted_iota(jnp.int32, (Q_BLOCK, 1), 0)
    tpos = (s0 + t_loc).astype(F32)
    n_f = lax.broadcasted_iota(jnp.int32, (1, Q_BLOCK), 1).astype(F32)
    ok_c = (tpos - (n_f * S_CMP + (L_CMP - 1.0))) >= 0.0
    nrow = ((nsel + 7) // 8) * 8
    groups = range(G_B)

    def softmax_steps(g, width, bias_fn, out):
        m_old_all = m_scr[g]
        m_news, alphas = [], []
        for rb in range(rows_g // SM_ROWS):
            rows = slice(rb * SM_ROWS, (rb + 1) * SM_ROWS)
            s = s_scr[g, rows, 0:width]
            if bias_fn is not None:
                s = bias_fn(s, (rb * SM_ROWS) % Q_BLOCK)
            m_old = m_old_all[rows]
            m_new = jnp.maximum(m_old, jnp.max(s, axis=-1, keepdims=True))
            e = jnp.exp2(s - jnp.concatenate([m_new] * (width // LANES), axis=1))
            e_scr[g, rows, 0:width] = e.astype(BF16)
            m_news.append(m_new)
            alphas.append(jnp.exp2(m_old - m_new))
            if rb % 2 == 1:
                yield
        m_scr[g] = jnp.concatenate(m_news, axis=0)
        out.append(jnp.concatenate(alphas, axis=0))

    def reset_stats():
        m_scr[...] = jnp.full_like(m_scr, M_INIT)
        acc_scr[...] = jnp.zeros_like(acc_scr)

    def attend_steps(width, a_fn, key_fn, val_fn, bias_fn):
        s_scr[0, :, 0:width] = _dot(a_fn(0), key_fn(0))
        yield
        for g in groups:
            if g + 1 < G_B:
                s_scr[g + 1, :, 0:width] = _dot(a_fn(g + 1), key_fn(g + 1))
            alpha = []
            yield from softmax_steps(g, width, bias_fn, alpha)
            acc_scr[g] = alpha[0] * acc_scr[g] + _dot_nt(e_scr[g, :, 0:width], val_fn(g))
            yield

    def attend(*args):
        for _ in attend_steps(*args):
            pass

    def branch_out(g):
        acc = acc_scr[g]
        return acc[:, 0:DH_B] / jnp.maximum(acc[:, DH_B:DH_B + 1], 1e-30)

    reset_stats()
    win_stream = attend_steps(WINDOW + Q_BLOCK,
                              lambda g: ags[g],
                              lambda g: jnp.concatenate([kb_win[g, qb + w] for w in range(n_wt + 1)], axis=1),
                              lambda g: jnp.concatenate([vt_win[g, qb + w] for w in range(n_wt + 1)], axis=1),
                              lambda s, t0: s + band_ref[t0:t0 + SM_ROWS, :])

    cur_t = (s0 + lax.broadcasted_iota(jnp.int32, (nrow, Q_BLOCK), 1)) // L_SEL
    o_cmp, ags_sel = [None] * G_B, [None] * G_B

    def sel_stream(g):
        s_c = _dot(ags[g], kb_cmp[g]).reshape(HPG, Q_BLOCK, Q_BLOCK)
        yield
        p3 = _masked_softmax(s_c, ok_c[None], base2=True)
        yield
        o_cmp[g] = _dot_nt(p3.reshape(rows_g, Q_BLOCK).astype(BF16), cmp_ref[0, 1, g * DH_B:(g + 1) * DH_B, :])
        imp_t = _block_importance_t((p3[0] + p3[1]) + (p3[2] + p3[3]), ovt)
        yield
        sel_t = None
        for sel_t in _topk_mask_steps(imp_t[0:nrow], cur_t, nsel, Q_BLOCK):
            yield
        nsel_pad = jnp.concatenate([jnp.zeros((F_SEL, Q_BLOCK), F32), 1.0 - sel_t,
                                    jnp.zeros((A_W - F_SEL - nrow, Q_BLOCK), F32)], axis=0).astype(BF16)
        nsl = _dot_nt(eye, nsel_pad).astype(BF16)
        ags_sel[g] = ags[g] + jnp.concatenate([nsl] * HPG, axis=0)
        yield

    def gate_stream():
        for g in groups:
            gt = gate_ref[:, g * LANES:(g + 1) * LANES]
            for c in range(3 * HPG):
                gb_scr[g * 3 * HPG + c] = jnp.broadcast_to(gt[:, c:c + 1], (Q_BLOCK, LANES))
                if c % 2 == 1:
                    yield

    _round_robin(win_stream, *[sel_stream(g) for g in groups], gate_stream())
    o_win = [branch_out(g) for g in groups]
    for g in groups:
        a_scr[g] = ags_sel[g]

    def causal_bias(kti):
        c = qb % (SLC_TILE // Q_BLOCK)

        def fn(s, t0):
            return s + cbias_ref[c, t0:t0 + SM_ROWS, :]
        return fn

    def slc_tile(kti, diag):
        attend(SLC_TILE, lambda g: a_scr[g], lambda g: kb_slc[g, kti], lambda g: vt_slc[g, kti],
               causal_bias(kti) if diag else None)

    reset_stats()
    kd = s0 // SLC_TILE

    def slc_full(kti, c):
        slc_tile(kti, False)
        return c

    lax.fori_loop(0, kd, slc_full, 0)
    slc_tile(kd, True)
    o_slc = [branch_out(g) for g in groups]

    pieces = []
    for g in groups:
        for hl in range(HPG):
            r = slice(hl * Q_BLOCK, (hl + 1) * Q_BLOCK)
            gb = [gb_scr[g * 3 * HPG + br * HPG + hl, :, 0:DH_B] for br in range(3)]
            comb = gb[0] * o_cmp[g][r]
            comb = comb + gb[1] * o_slc[g][r]
            comb = comb + gb[2] * o_win[g][r]
            pieces.append(comb)
    o_ref[...] = jnp.concatenate(pieces, axis=1).astype(o_ref.dtype)


def _nsa_prompt_call(q, gates, cmp_t, kv_t, win_t, consts, b, t):
    featc, fslc, fwin, fcmp, ovt, band, cbias = consts
    nqb = t // Q_BLOCK
    n_wtiles = (WINDOW + t) // Q_BLOCK
    return pl.pallas_call(
        functools.partial(_nsa_prompt_kernel, t_len=t),
        grid=(b, nqb),
        in_specs=[pl.BlockSpec((Q_BLOCK, H_B * A_W), lambda bi, qi: (bi * nqb + qi, 0)),
                  pl.BlockSpec((Q_BLOCK, G_B * LANES), lambda bi, qi: (bi * nqb + qi, 0)),
                  pl.BlockSpec((1, 2, GD, cmp_t.shape[3]), lambda bi, qi: (bi, 0, 0, 0)),
                  pl.BlockSpec((1, 2 * GD, t), lambda bi, qi: (bi, 1, 0)),
                  pl.BlockSpec((1, 2 * GD, t), lambda bi, qi: (bi, 0, 0)),
                  _const_spec(featc.shape), _const_spec(fslc.shape), _const_spec(fwin.shape),
                  _const_spec(fcmp.shape), _const_spec(ovt.shape), _const_spec(band.shape), _const_spec(cbias.shape)],
        out_specs=pl.BlockSpec((Q_BLOCK, D_MODEL), lambda bi, qi: (bi * nqb + qi, 0)),
        out_shape=jax.ShapeDtypeStruct((b * t, D_MODEL), BF16),
        scratch_shapes=[pltpu.VMEM((G_B, t // SLC_TILE, A_W, SLC_TILE), BF16),
                        pltpu.VMEM((G_B, t // SLC_TILE, A_W, SLC_TILE), BF16),
                        pltpu.VMEM((G_B, n_wtiles, A_W, Q_BLOCK), BF16),
                        pltpu.VMEM((G_B, n_wtiles, A_W, Q_BLOCK), BF16),
                        pltpu.VMEM((G_B, A_W, Q_BLOCK), BF16),
                        pltpu.VMEM((G_B, HPG * Q_BLOCK, A_W), BF16),
                        pltpu.VMEM((G_B, HPG * Q_BLOCK, WINDOW + Q_BLOCK), F32),
                        pltpu.VMEM((G_B, HPG * Q_BLOCK, WINDOW + Q_BLOCK), BF16),
                        pltpu.VMEM((G_B, HPG * Q_BLOCK, LANES), F32),
                        pltpu.VMEM((G_B, HPG * Q_BLOCK, A_W), F32),
                        pltpu.VMEM((3 * H_B, Q_BLOCK, LANES), F32)],
        compiler_params=_params("parallel", "arbitrary"),
        name="nsa_prompt",
    )(q, gates, cmp_t, kv_t, win_t, featc, fslc, fwin, fcmp, ovt, band, cbias)


def _nsa_prompt_consts(t):
    slopes = jnp.exp2(-8.0 * jnp.arange(1, H_B + 1, dtype=F32) / H_B)
    featc = jnp.zeros((H_B, A_W), F32).at[:, F_ONE].set(1.0)
    for k, term in enumerate(_split3(slopes * LOG2E)):
        featc = featc.at[:, F_POS + 2 * k].set(term.astype(F32)).at[:, F_POS + 2 * k + 1].set(term.astype(F32))

    def pos_rows(pos):
        f = np.zeros((A_W - DH_B, pos.shape[0]), np.float32)
        hi = (pos // 256) * 256
        lo = pos % 256
        for k in range(3):
            f[F_POS - DH_B + 2 * k] = hi
            f[F_POS - DH_B + 2 * k + 1] = lo
        return f

    nsel = t // L_SEL
    spos = np.arange(t)
    fslc = pos_rows(spos)
    for j in range(nsel):
        fslc[F_SEL - DH_B + j] = np.where(spos // L_SEL == j, -BIG, 0.0)
    wcol = np.arange(WINDOW + t)
    fwin = pos_rows(wcol)
    fwin[F_ONE - DH_B] = np.where(wcol < WINDOW, -BIG, 0.0)
    fcmp = pos_rows(np.arange(Q_BLOCK) * S_CMP + (L_CMP - 1))
    u = np.arange(WINDOW + Q_BLOCK)[None, :]
    tl = np.arange(Q_BLOCK)[:, None]
    band = np.where((u >= tl) & (u <= tl + WINDOW), 0.0, -BIG).astype(np.float32)
    uc = np.arange(SLC_TILE)[None, None, :]
    cpos = (np.arange(SLC_TILE // Q_BLOCK) * Q_BLOCK)[:, None, None] + np.arange(Q_BLOCK)[None, :, None]
    cbias = np.where(uc <= cpos, 0.0, -BIG).astype(np.float32)
    return (featc, jnp.asarray(fslc, dtype=BF16), jnp.asarray(fwin, dtype=BF16),
            jnp.asarray(fcmp, dtype=BF16), _overlap_t(nsel), jnp.asarray(band), jnp.asarray(cbias))


def _nsa_sample_kernel(pt_ref, *refs, past, ts):
    pages = refs[:N_PAGES_ARG]
    (slope_ref, slopel_ref, q_ref, gate_ref, cmp_ref, kvn_ref, win_ref, winn_ref, eg_ref, ovt_ref, o_ref) = refs[N_PAGES_ARG:]
    rows = H_B * ts
    nsel = -(-(past + ts) // L_SEL)
    lane_g = lax.broadcasted_iota(jnp.int32, (1, GD), 1) // DH_B
    q = q_ref[...]
    qq = jnp.concatenate([jnp.where(lane_g == g, q[:, h * GD:(h + 1) * GD], 0.0)
                          for g in range(G_B) for h in range(HPG)], axis=0).astype(BF16)
    slope = slope_ref[...]
    t_row = lax.broadcasted_iota(jnp.int32, (rows, 1), 0) % ts
    tpos = (past + t_row).astype(F32)
    kc_t = cmp_ref[0, 0]
    vc_t = cmp_ref[0, 1]
    ncb = kc_t.shape[1]
    res = {}
    n_f = lax.broadcasted_iota(jnp.int32, (1, ncb), 1).astype(F32)
    dist_c = tpos - (n_f * S_CMP + (L_CMP - 1.0))
    nrow = ((nsel + 1 + 7) // 8) * 8
    cur_t = (past + lax.broadcasted_iota(jnp.int32, (nrow, rows), 1) % ts) // L_SEL

    def select_steps():
        p = _masked_softmax(_dot(qq, kc_t) - slope * dist_c, dist_c >= 0.0)
        yield
        res["o_cmp"] = _dot_nt(p.astype(BF16), vc_t)
        ps = []
        for g in range(G_B):
            blk = [p[(g * HPG + h) * ts:(g * HPG + h + 1) * ts] for h in range(HPG)]
            ps.extend([(blk[0] + blk[1]) + (blk[2] + blk[3])] * HPG)
        psum = jnp.concatenate(ps, axis=0)
        imp_t = _block_importance_t(psum, ovt_ref[...])
        yield
        for sel in _topk_mask_steps(imp_t[0:nrow], cur_t, nsel, rows):
            res["sel"] = sel
            yield

    tpos_l = (past + lax.broadcasted_iota(jnp.int32, (1, rows), 1) % ts).astype(F32)
    slope_l = slopel_ref[...]
    key_i = lax.broadcasted_iota(jnp.int32, (PAGE_SIZE, rows), 0)
    bpp = PAGE_SIZE // L_SEL

    def rows_t(new_ref, lo):
        return _pad_rows(new_ref[:, lo:lo + GD], PAGE_SIZE).T.astype(BF16)

    def attend_steps(name, kt_all, vt_all, tiles, before_masks=None):
        s_all = lax.dot_general(kt_all, qq, (((0,), (1,)), ((), ())), preferred_element_type=F32)
        yield
        if before_masks is not None:
            yield from before_masks
        sms = []
        for i, (k0, n_real, blk) in enumerate(tiles):
            s = s_all[i * PAGE_SIZE:(i + 1) * PAGE_SIZE]
            dist = tpos_l - (k0 + key_i).astype(F32)
            dist_m = dist if n_real == PAGE_SIZE else jnp.where(key_i < n_real, dist, -1.0)
            if blk is not None:
                sel_t = res["sel"]
                mask = jnp.concatenate([jnp.broadcast_to(sel_t[blk + a:blk + a + 1], (L_SEL, rows)) for a in range(bpp)],
                                       axis=0)
                ok = jnp.where(dist_m >= 0.0, mask, 0.0) > 0.5
            else:
                ok = jnp.abs(dist_m - 0.5 * WINDOW) <= 0.5 * WINDOW
            sms.append(jnp.where(ok, s - slope_l * dist, NEG))
            if i % 2 == 1:
                yield
        m = functools.reduce(jnp.maximum, [jnp.max(x, axis=0, keepdims=True) for x in sms])
        es = []
        for i, x in enumerate(sms):
            es.append(jnp.exp(x - m))
            if i % 4 == 3:
                yield
        l = functools.reduce(lambda a, b: a + b, [jnp.sum(e, axis=0, keepdims=True) for e in es])
        o_t = _dot(vt_all, jnp.concatenate(es, axis=0).astype(BF16))
        res[name] = (o_t / jnp.maximum(l, 1e-30)).T

    kt_slc = jnp.concatenate([pg[0, 0:GD, :].astype(BF16) for pg in pages] + [rows_t(kvn_ref, 0)], axis=1)
    vt_slc = jnp.concatenate([pg[0, GD:2 * GD, :].astype(BF16) for pg in pages] + [rows_t(kvn_ref, GD)], axis=1)
    wbuf = win_ref.shape[2]
    kt_win = jnp.concatenate([win_ref[0, 0:GD, :].astype(BF16), rows_t(winn_ref, 0)], axis=1)
    vt_win = jnp.concatenate([win_ref[0, GD:2 * GD, :].astype(BF16), rows_t(winn_ref, GD)], axis=1)
    _round_robin(
        attend_steps("o_slc", kt_slc, vt_slc,
                     [(pi * PAGE_SIZE, PAGE_SIZE, pi * bpp) for pi in range(len(pages))] + [(past, ts, past // L_SEL)],
                     before_masks=select_steps()),
        attend_steps("o_win", kt_win, vt_win,
                     [(past - wbuf + wi * PAGE_SIZE, PAGE_SIZE, None) for wi in range(wbuf // PAGE_SIZE)]
                     + [(past, ts, None)]))
    o_cmp, o_slc, o_win = res["o_cmp"], res["o_slc"], res["o_win"]

    hi, mid, lo = _split3(gate_ref[...])
    eg = eg_ref[...]
    gexp = _dot(hi, eg) + _dot(mid, eg) + _dot(lo, eg)

    def gate_rows(br):
        return jnp.concatenate([gexp[:, br * D_MODEL + h * GD:br * D_MODEL + (h + 1) * GD]
                                for _ in range(G_B) for h in range(HPG)], axis=0)

    comb = gate_rows(0) * o_cmp + gate_rows(1) * o_slc + gate_rows(2) * o_win
    outs = []
    for h in range(HPG):
        acc = None
        for g in range(G_B):
            piece = jnp.where(lane_g == g, comb[(g * HPG + h) * ts:(g * HPG + h + 1) * ts], 0.0)
            acc = piece if acc is None else acc + piece
        outs.append(acc)
    o_ref[...] = jnp.concatenate(outs, axis=1).astype(o_ref.dtype)


def _nsa_sample_call(page_tab, cache_t, slope_rows, q, gates, cmp_t, kvn, win_t, winn, eg, ovt, b, ts):
    npg = page_tab.shape[1]
    assert npg == N_PAGES_ARG
    past = npg * PAGE_SIZE
    page_specs = [pl.BlockSpec((1, 2 * GD, PAGE_SIZE), functools.partial(lambda bi, pt, p: (pt[bi * N_PAGES_ARG + p], 1, 0), p=p))
                  for p in range(npg)]
    wbuf = win_t.shape[2]
    grid_spec = pltpu.PrefetchScalarGridSpec(
        num_scalar_prefetch=1,
        grid=(b,),
        in_specs=page_specs + [
            pl.BlockSpec(slope_rows.shape, lambda bi, pt: (0, 0), pipeline_mode=pl.Buffered(1)),
            pl.BlockSpec((1, slope_rows.shape[0]), lambda bi, pt: (0, 0), pipeline_mode=pl.Buffered(1)),
            pl.BlockSpec((ts, D_MODEL), lambda bi, pt: (bi, 0)),
            pl.BlockSpec((ts, LANES), lambda bi, pt: (bi, 0)),
            pl.BlockSpec((1, 2, GD, cmp_t.shape[3]), lambda bi, pt: (bi, 0, 0, 0)),
            pl.BlockSpec((ts, 2 * GD), lambda bi, pt: (bi, 1)),
            pl.BlockSpec((1, 2 * GD, wbuf), lambda bi, pt: (bi, 0, 0)),
            pl.BlockSpec((ts, 2 * GD), lambda bi, pt: (bi, 0)),
            pl.BlockSpec(eg.shape, lambda bi, pt: (0, 0), pipeline_mode=pl.Buffered(1)),
            pl.BlockSpec(ovt.shape, lambda bi, pt: (0, 0), pipeline_mode=pl.Buffered(1))],
        out_specs=pl.BlockSpec((ts, D_MODEL), lambda bi, pt: (bi, 0)),
    )
    return pl.pallas_call(
        functools.partial(_nsa_sample_kernel, past=past, ts=ts),
        grid_spec=grid_spec,
        out_shape=jax.ShapeDtypeStruct((b * ts, D_MODEL), F32),
        compiler_params=_params("parallel"),
        name="nsa_sample",
    )(page_tab.reshape(-1), *([cache_t] * npg), slope_rows, slope_rows.reshape(1, -1), q, gates, cmp_t, kvn, win_t, winn, eg, ovt)


def _overlap_t(nsel):
    n = np.arange(LANES)[None, :] * S_CMP
    j = np.arange(LANES)[:, None] * L_SEL
    ov = (n <= j + L_SEL - 1) & (n + L_CMP - 1 >= j) & (np.arange(LANES)[None, :] < LANES - 1) & (np.arange(LANES)[:, None] < nsel)
    return jnp.asarray(ov.astype(np.float32), dtype=BF16)


def _gate_expand():
    e = np.zeros((LANES, 3 * D_MODEL), np.float32)
    for br in range(3):
        for hh in range(H_B):
            e[br * H_B + hh, br * D_MODEL + hh * DH_B:br * D_MODEL + (hh + 1) * DH_B] = 1.0
    return jnp.asarray(e, dtype=BF16)


def _cmp_weights(pe, w1, w2):
    r_n = L_CMP // S_CMP
    pe_t = jnp.tile(pe.reshape(r_n, S_CMP, 1, DH_B), (1, 1, G_B, 1)).reshape(r_n, S_CMP, GD)
    eye = jnp.eye(G_B, dtype=F32)
    w1r = w1.reshape(r_n, S_CMP, DH_B, CMP_HID)
    w1b = jnp.einsum("rsdh,ab->rsadbh", w1r, jnp.eye(2, dtype=F32)).reshape(r_n, S_CMP * 2 * DH_B, 2 * CMP_HID)
    w2bt = jnp.einsum("hd,gk->kdgh", w2, eye).reshape(GD, G_B * CMP_HID)
    return pe_t, w1b.astype(BF16), w2bt.astype(BF16)


def _rows_minor(x):
    b, r = x.shape[:2]
    return jnp.moveaxis(x, 1, -1).reshape(b, -1, r)


def _rows_major(x_t, feat_shape):
    b, _, r = x_t.shape
    return jnp.moveaxis(x_t.reshape((b,) + feat_shape + (r,)), -1, 1)


def kernel(x_prompt, x_sample, c_prompt, c_sample, state_ret, state_conv, cache_kv, state_win, page_table, w_ada, b_ada, g_mix, g_ffn, w_ffn_in, conv_w, conv_b, w_ffn_out, w_ret_in, w_ret_out, g_kv, w_ada_kv, b_ada_kv, w_kv, pe_ck, pe_cv, w_ck1, w_ck2, w_cv1, w_cv2, w_nsa_in, w_nsa_out, g_final):
    bp, tp, d = x_prompt.shape
    bs, ts, _ = x_sample.shape
    depth = w_ada.shape[0]
    assert depth == 2 and d == D_MODEL and tp % SLC_TILE == 0 and ts == 8
    npg = page_table.shape[1]
    past = npg * PAGE_SIZE
    f2 = 2 * D_FF
    hd = H_B * DH_B

    w_ada_all = jnp.concatenate([w_ada[0], w_ada[1], w_ada_kv], axis=1).astype(BF16)
    b_ada_all = jnp.concatenate([b_ada[0], b_ada[1], b_ada_kv], axis=0).reshape(1, -1)
    w_ret_in_b = w_ret_in[0].astype(BF16)
    w_ret_out_b = w_ret_out[0].astype(BF16)
    w_ffn_in_b = w_ffn_in.astype(BF16)
    w_ffn_out_b = w_ffn_out.astype(BF16)
    w_kv_b = w_kv.astype(BF16)
    w_kv_t = w_kv_b.T
    w_kv_rows, w_kv_win = w_kv_b[:, :4 * GD], w_kv_b[:, 4 * GD:]
    wn = w_nsa_in[0]
    wq_s = wn[:, :hd] * (DH_B ** -0.5)
    w_q_pad = jnp.pad((wq_s * LOG2E).reshape(d, H_B, DH_B),
                      ((0, 0), (0, 0), (0, A_W - DH_B))).reshape(d, H_B * A_W).astype(BF16)
    w_g_grp = jnp.pad(wn[:, hd:].reshape(d, G_B, HPG, 3).transpose(0, 1, 3, 2).reshape(d, G_B, 3 * HPG),
                      ((0, 0), (0, 0), (0, LANES - 3 * HPG))).reshape(d, G_B * LANES).astype(BF16)
    w_q_s = wq_s.reshape(d, G_B, HPG, DH_B).transpose(0, 2, 1, 3).reshape(d, hd).astype(BF16)
    w_g_s = jnp.pad(wn[:, hd:].reshape(d, G_B, HPG, 3).transpose(0, 3, 2, 1).reshape(d, 3 * H_B),
                    ((0, 0), (0, LANES - 3 * H_B))).astype(BF16)
    w_o_b = w_nsa_out[0].astype(BF16)
    w_o_s = w_nsa_out[0].reshape(G_B, HPG, DH_B, d).transpose(1, 0, 2, 3).reshape(hd, d).astype(BF16)
    pe_k, w1_k, w2_k = _cmp_weights(pe_ck, w_ck1, w_ck2)
    pe_v, w1_v, w2_v = _cmp_weights(pe_cv, w_cv1, w_cv2)
    pe_c = jnp.stack([pe_k, pe_v])
    w1_c = jnp.stack([w1_k, w1_v])
    w2t_c = jnp.stack([w2_k, w2_v])
    slopes = jnp.exp2(-8.0 * jnp.arange(1, H_B + 1, dtype=F32) / H_B)
    log_g = jnp.log1p(-jnp.exp2(-5.0 - jnp.arange(H_A, dtype=F32)))
    lg_tab = jnp.broadcast_to(log_g[:, None, None], (H_A, 1, LANES))

    c_all = jnp.concatenate([c_prompt, c_sample], axis=0)
    mod = _ada_call(c_all, w_ada_all, b_ada_all)

    def mods(table, per_row):
        def cols(k):
            return (table, k, per_row)
        return [[cols(l * 6 + k) for k in range(6)] for l in range(depth)] + [[cols(12), cols(13)]]

    def layer0(x, mod_l, s_ret, conv_buf, b, t, tm, act_dt):
        sh1, sc1, ga1, sh2, sc2, ga2 = mod_l
        tm_in = 2 * tm if t % (2 * tm) == 0 else tm
        pr = _nmm_call(x, g_mix[0], sh1, sc1, w_ret_in_b, t, tm_in, 1536, act_dt, name="ret_in")
        og, s_new = _ret_call(pr, s_ret, lg_tab, b, t, act_dt)
        x = _mm_res_call(og, w_ret_out_b, x, ga1, t, tm_in, name="ret_out")
        x, conv0 = _ffn_call(x, g_ffn[0], sh2, sc2, ga2, w_ffn_in_b[0], conv_w[0], conv_b[0], conv_buf,
                             w_ffn_out_b[0], b, t, tm, name="ffn0")
        return x, s_new, conv0

    def layer1_tail(x, o, w_o, mod_l, conv_buf, b, t, tm):
        _, _, ga1, sh2, sc2, ga2 = mod_l
        x = _mm_res_call(o, w_o, x, ga1, t, 2 * tm if t % (2 * tm) == 0 else tm, name="nsa_out")
        return _ffn_call(x, g_ffn[1], sh2, sc2, ga2, w_ffn_in_b[1], conv_w[1], conv_b[1], conv_buf,
                         w_ffn_out_b[1], b, t, tm, g_final=g_final, name="ffn1")

    tm = 512
    mp = mods(mod[0:bp], False)
    x = x_prompt.reshape(bp * tp, d)
    x, ret_p, conv0_p = layer0(x, mp[0], None, jnp.zeros((bp, CONV_W - 1, f2), F32), bp, tp, tm, BF16)
    sh1, sc1 = mp[1][0], mp[1][1]
    tm_p = 2 * tm
    kv_t, win_t = _kvt_call(x, g_kv, mp[2][0], mp[2][1], w_kv_t, bp, tp, tm_p)
    q = _nmm_call(x, g_mix[1], sh1, sc1, w_q_pad, tp, tm_p, hd, BF16, name="q_proj")
    gates = _nmm_call(x, g_mix[1], sh1, sc1, w_g_grp, tp, tm_p, G_B * LANES, F32, act="sigmoid", name="gate_proj")
    assert tp == N_PAGES_ARG * PAGE_SIZE
    ident = jnp.zeros((bp, N_PAGES_ARG), jnp.int32)
    cmp_t = _cmp_call(kv_t, lambda bi, p, pt: (bi, 0, p), ident, pe_c, w1_c, w2t_c)
    o = _nsa_prompt_call(q, gates, cmp_t, kv_t, win_t, _nsa_prompt_consts(tp), bp, tp)
    y_p, conv1_p = layer1_tail(x, o, w_o_b, mp[1], jnp.zeros((bp, CONV_W - 1, f2), F32), bp, tp, tm)
    kv_p = _rows_major(kv_t, (4, G_B, DH_B))
    win_p = _rows_major(win_t[:, :, tp - min(WINDOW, tp):], (2, G_B, DH_B))

    tm = 256
    ms = mods(jnp.repeat(mod[bp:bp + bs], ts, axis=0), True)
    x = x_sample.reshape(bs * ts, d)
    x, ret_s, conv0_s = layer0(x, ms[0], state_ret[0].astype(F32), state_conv[0], bs, ts, tm, F32)
    sh1, sc1 = ms[1][0], ms[1][1]
    kv_rows = _nmm_call(x, g_kv, ms[2][0], ms[2][1], w_kv_rows, ts, tm, 4 * GD, F32, name="kv_rows")
    win_rows = _nmm_call(x, g_kv, ms[2][0], ms[2][1], w_kv_win, ts, tm, 2 * GD, F32, name="win_rows")
    q = _nmm_call(x, g_mix[1], sh1, sc1, w_q_s, ts, tm, hd, F32, name="q_proj")
    gates = _nmm_call(x, g_mix[1], sh1, sc1, w_g_s, ts, tm, LANES, F32, act="sigmoid", name="gate_proj")
    cache_t = _rows_minor(cache_kv)
    state_win_t = _rows_minor(state_win)
    cmp_t = _cmp_call(cache_t, lambda bi, p, pt: (pt[bi * N_PAGES_ARG + p], 0, 0), page_table, pe_c, w1_c, w2t_c)
    slope_rows = jnp.repeat(slopes, ts).reshape(H_B * ts, 1)
    o = _nsa_sample_call(page_table, cache_t, slope_rows, q, gates, cmp_t, kv_rows, state_win_t, win_rows,
                         _gate_expand(), _overlap_t(-(-(past + ts) // L_SEL)), bs, ts)
    y_s, conv1_s = layer1_tail(x, o, w_o_s, ms[1], state_conv[1], bs, ts, tm)
    kv_s = kv_rows.reshape(bs, ts, 4, G_B, DH_B)
    win_new_t = jnp.concatenate([state_win_t, jnp.swapaxes(win_rows.reshape(bs, ts, 2 * GD), 1, 2)], axis=2)
    win_s = _rows_major(win_new_t[:, :, win_new_t.shape[2] - min(WINDOW, win_new_t.shape[2]):], (2, G_B, DH_B))

    return (y_p.reshape(bp, tp, d), y_s.reshape(bs, ts, d), ret_p[None], ret_s[None],
            jnp.stack([conv0_p, conv1_p]), jnp.stack([conv0_s, conv1_s]), kv_p, kv_s, win_p, win_s)
```

```python
import functools

import numpy as np
import jax
import jax.numpy as jnp
from jax import lax
from jax.experimental import pallas as pl
from jax.experimental.pallas import tpu as pltpu

F32 = jnp.float32
BF16 = jnp.bfloat16

D_MODEL = 1024
H_A = 4
DK_A = D_MODEL // H_A
DV_A = 2 * DK_A
RET_CHUNK = 128
H_B = 16
DH_B = D_MODEL // H_B
G_B = 4
HPG = H_B // G_B
GD = G_B * DH_B
L_CMP = 32
S_CMP = 16
L_SEL = 64
N_SEL = 16
WINDOW = 512
CMP_HID = 2 * DH_B
Q_BLOCK = 128
FORCE_BONUS = 1e4
D_FF = 2816
CONV_W = 3
EPS = 1e-6
PAGE_SIZE = 128
NEG = -1e30
BIG = 2.0 ** 100
M_INIT = -(2.0 ** 101)
LOG2E = 1.4426950408889634

LANES = 128
SUB = 8
VMEM_LIMIT = 56 * 1024 * 1024
FF_CHUNK = 256
SLC_TILE = 512
SM_ROWS = 64
N_PAGES_ARG = 16
RET_IN_TN = 3072
ADA_TN = 2048

F_POS = DH_B
F_ONE = DH_B + 6
F_SEL = DH_B + 8
A_W = 2 * DH_B

_NT = (((1,), (1,)), ((), ()))


def _sigmoid(x):
    return 1.0 / (1.0 + jnp.exp(-x))


def _silu(x):
    return x * _sigmoid(x)


def _dot(a, b):
    return jnp.dot(a, b, preferred_element_type=F32)


def _dot_nt(a, b):
    return lax.dot_general(a, b, _NT, preferred_element_type=F32)


def _split3(x):
    hi = x.astype(BF16)
    r = x - hi.astype(F32)
    mid = r.astype(BF16)
    lo = (r - mid.astype(F32)).astype(BF16)
    return hi, mid, lo


def _params(*sem):
    return pltpu.CompilerParams(dimension_semantics=sem, vmem_limit_bytes=VMEM_LIMIT)


def _const_spec(shape):
    nd = len(shape)
    return pl.BlockSpec(shape, lambda *a: (0,) * nd, pipeline_mode=pl.Buffered(1))


def _mod_arg(v, m, tm, rows_per_batch):
    table, col, per_row = v
    d = D_MODEL
    if per_row:
        assert table.shape[0] == m and m % tm == 0
        return table.reshape(m // tm, tm, table.shape[1]), (1, tm, d), (lambda i: (i, 0, col))
    assert rows_per_batch % tm == 0
    tpb = rows_per_batch // tm
    return table.reshape(table.shape[0], 1, table.shape[1]), (1, 1, d), (lambda i: (i // tpb, 0, col))


def _ada_kernel(c_ref, w_ref, b_ref, o_ref):
    c = c_ref[...]
    o_ref[...] = _dot(_silu(c).astype(BF16), w_ref[...]) + b_ref[...]


def _ada_call(c, w, b, tn=ADA_TN):
    m, d = c.shape
    n = w.shape[1]
    assert n % tn == 0
    return pl.pallas_call(
        _ada_kernel,
        grid=(n // tn,),
        in_specs=[pl.BlockSpec((m, d), lambda j: (0, 0)),
                  pl.BlockSpec((d, tn), lambda j: (0, j)),
                  pl.BlockSpec((1, tn), lambda j: (0, j))],
        out_specs=pl.BlockSpec((m, tn), lambda j: (0, j)),
        out_shape=jax.ShapeDtypeStruct((m, n), F32),
        compiler_params=_params("parallel"),
        name="ada",
    )(c, w, b)


def _norm_mod(x, g, sh, sc):
    y = x * lax.rsqrt(jnp.mean(x * x, axis=-1, keepdims=True) + EPS) * g
    return y * (1.0 + sc) + sh


def _nmm_kernel(x_ref, g_ref, sh_ref, sc_ref, w_ref, o_ref, h_scr, *, act):
    @pl.when(pl.program_id(1) == 0)
    def _():
        h_scr[...] = _norm_mod(x_ref[...], g_ref[...], sh_ref[0], sc_ref[0]).astype(BF16)

    acc = _dot(h_scr[...], w_ref[...])
    if act == "sigmoid":
        acc = _sigmoid(acc)
    o_ref[...] = acc.astype(o_ref.dtype)


def _nmm_call(x, g, sh, sc, w, rows_per_batch, tm, tn, out_dtype, act=None, name="nmm"):
    m, d = x.shape
    n = w.shape[1]
    assert m % tm == 0 and n % tn == 0
    sh_a, mshape, sh_map = _mod_arg(sh, m, tm, rows_per_batch)
    sc_a, _, sc_map = _mod_arg(sc, m, tm, rows_per_batch)
    return pl.pallas_call(
        functools.partial(_nmm_kernel, act=act),
        grid=(m // tm, n // tn),
        in_specs=[pl.BlockSpec((tm, d), lambda i, j: (i, 0)),
                  pl.BlockSpec((1, d), lambda i, j: (0, 0)),
                  pl.BlockSpec(mshape, lambda i, j: sh_map(i)),
                  pl.BlockSpec(mshape, lambda i, j: sc_map(i)),
                  pl.BlockSpec((d, tn), lambda i, j: (0, j))],
        out_specs=pl.BlockSpec((tm, tn), lambda i, j: (i, j)),
        out_shape=jax.ShapeDtypeStruct((m, n), out_dtype),
        scratch_shapes=[pltpu.VMEM((tm, d), BF16)],
        compiler_params=_params("parallel", "arbitrary"),
        name=name,
    )(x, g.reshape(1, d), sh_a, sc_a, w)


def _kvt_kernel(x_ref, g_ref, sh_ref, sc_ref, wt_ref, kv_ref, win_ref):
    h = _norm_mod(x_ref[...], g_ref[...], sh_ref[0], sc_ref[0]).astype(BF16)
    n_kv = kv_ref.shape[1]
    kv_ref[0] = _dot_nt(wt_ref[0:n_kv, :], h)
    win_ref[0] = _dot_nt(wt_ref[n_kv:, :], h)


def _kvt_call(x, g, sh, sc, w_t, b, t, tm):
    m, d = x.shape
    n = w_t.shape[0]
    nt = t // tm
    sh_a, mshape, sh_map = _mod_arg(sh, m, tm, t)
    sc_a, _, sc_map = _mod_arg(sc, m, tm, t)
    return pl.pallas_call(
        _kvt_kernel,
        grid=(b, nt),
        in_specs=[pl.BlockSpec((tm, d), lambda bi, ti: (bi * nt + ti, 0)),
                  pl.BlockSpec((1, d), lambda bi, ti: (0, 0)),
                  pl.BlockSpec(mshape, lambda bi, ti: sh_map(bi * nt + ti)),
                  pl.BlockSpec(mshape, lambda bi, ti: sc_map(bi * nt + ti)),
                  _const_spec((n, d))],
        out_specs=[pl.BlockSpec((1, 4 * GD, tm), lambda bi, ti: (bi, 0, ti)),
                   pl.BlockSpec((1, 2 * GD, tm), lambda bi, ti: (bi, 0, ti))],
        out_shape=[jax.ShapeDtypeStruct((b, 4 * GD, t), F32),
                   jax.ShapeDtypeStruct((b, 2 * GD, t), F32)],
        compiler_params=_params("parallel", "parallel"),
        name="kv_proj_t",
    )(x, g.reshape(1, d), sh_a, sc_a, w_t)


def _mm_res_kernel(a_ref, w_ref, res_ref, ga_ref, o_ref):
    y = _dot(a_ref[...].astype(BF16), w_ref[...])
    o_ref[...] = res_ref[...] + ga_ref[0] * y


def _mm_res_call(a, w, res, gate, rows_per_batch, tm, name="mm_res"):
    m, k = a.shape
    d = w.shape[1]
    ga_a, mshape, mmap = _mod_arg(gate, m, tm, rows_per_batch)
    return pl.pallas_call(
        _mm_res_kernel,
        grid=(m // tm,),
        in_specs=[pl.BlockSpec((tm, k), lambda i: (i, 0)),
                  _const_spec((k, d)),
                  pl.BlockSpec((tm, d), lambda i: (i, 0)),
                  pl.BlockSpec(mshape, lambda i: mmap(i))],
        out_specs=pl.BlockSpec((tm, d), lambda i: (i, 0)),
        out_shape=jax.ShapeDtypeStruct((m, d), F32),
        compiler_params=_params("parallel"),
        name=name,
    )(a, w, res, ga_a)


def _ret_kernel(*refs, c, nc, has_s0):
    if has_s0:
        lg_ref, q_ref, k_ref, v_ref, g_ref, s0_ref, o_ref, so_ref, s_scr = refs
    else:
        lg_ref, q_ref, k_ref, v_ref, g_ref, o_ref, so_ref, s_scr = refs
        s0_ref = None
    n = pl.program_id(1)

    @pl.when(n == 0)
    def _():
        if has_s0:
            s_scr[...] = s0_ref[0]
        else:
            s_scr[...] = jnp.zeros_like(s_scr)

    cp = max(c, RET_CHUNK)

    def padded(a):
        a = a.astype(F32)
        if cp > c:
            a = jnp.concatenate([a, jnp.zeros((cp - c, a.shape[1]), F32)], axis=0)
        return a

    i = lax.broadcasted_iota(jnp.int32, (cp, 1), 0).astype(F32)
    j = lax.broadcasted_iota(jnp.int32, (1, cp), 1).astype(F32)
    diff = i - j
    s_olds = [s_scr[h] for h in range(H_A)]
    s_news = [None] * H_A

    def head_steps(h):
        lg = lg_ref[h][:, 0:1]
        q = padded(q_ref[:, h * DK_A:(h + 1) * DK_A]).astype(BF16)
        kf = padded(k_ref[:, h * DK_A:(h + 1) * DK_A]) * (DK_A ** -0.5)
        v = padded(v_ref[:, h * DV_A:(h + 1) * DV_A]).astype(BF16)
        dmask = jnp.where(diff >= 0, jnp.exp(jnp.maximum(diff, 0.0) * lg), 0.0)
        scores = _dot_nt(q, kf.astype(BF16)) * dmask
        yield
        s_old = s_olds[h]
        o = _dot(scores.astype(BF16), v) + _dot(q, s_old.astype(BF16)) * jnp.exp((i + 1.0) * lg)
        yield
        w = jnp.exp((c - 1.0 - i) * lg)
        kw_t = (kf * w).T.astype(BF16)
        s_news[h] = jnp.exp(c * lg) * s_old + _dot(kw_t, v)
        yield
        of = o * lax.rsqrt(jnp.mean(o * o, axis=-1, keepdims=True) + EPS)
        gate = g_ref[:, h * DV_A:(h + 1) * DV_A].astype(F32)
        o_ref[:, h * DV_A:(h + 1) * DV_A] = (of[0:c] * _silu(gate)).astype(o_ref.dtype)

    _round_robin(*[head_steps(h) for h in range(H_A)])
    for h in range(H_A):
        s_scr[h] = s_news[h]

    @pl.when(n == nc - 1)
    def _():
        so_ref[0] = s_scr[...]


def _ret_call(pr, s0, lg_tab, b, t, out_dtype):
    c = RET_CHUNK if t % RET_CHUNK == 0 else t
    nc = t // c
    has_s0 = s0 is not None
    nq, nv = H_A * DK_A, H_A * DV_A
    assert nv == 2 * nq
    in_specs = [pl.BlockSpec((H_A, 1, LANES), lambda bi, n: (0, 0, 0)),
                pl.BlockSpec((c, nq), lambda bi, n: (bi * nc + n, 0)),
                pl.BlockSpec((c, nq), lambda bi, n: (bi * nc + n, 1)),
                pl.BlockSpec((c, nv), lambda bi, n: (bi * nc + n, 1)),
                pl.BlockSpec((c, nv), lambda bi, n: (bi * nc + n, 2))]
    args = [lg_tab, pr, pr, pr, pr]
    if has_s0:
        in_specs.append(pl.BlockSpec((1, H_A, DK_A, DV_A), lambda bi, n: (bi, 0, 0, 0)))
        args.append(s0)
    return pl.pallas_call(
        functools.partial(_ret_kernel, c=c, nc=nc, has_s0=has_s0),
        grid=(b, nc),
        in_specs=in_specs,
        out_specs=[pl.BlockSpec((c, nv), lambda bi, n: (bi * nc + n, 0)),
                   pl.BlockSpec((1, H_A, DK_A, DV_A), lambda bi, n: (bi, 0, 0, 0))],
        out_shape=[jax.ShapeDtypeStruct((b * t, nv), out_dtype),
                   jax.ShapeDtypeStruct((b, H_A, DK_A, DV_A), F32)],
        scratch_shapes=[pltpu.VMEM((H_A, DK_A, DV_A), F32)],
        compiler_params=_params("parallel", "arbitrary"),
        name="retention",
    )(*args)


def _ffn_kernel(*refs, t_in, nbt, final):
    if final:
        (x_ref, g_ref, sh_ref, sc_ref, ga_ref, win_ref, cw_ref, cb_ref, cbuf_ref, wout_ref, gf_ref,
         o_ref, nc_ref, h_scr, carry_scr, act_scr, shift_scr) = refs
    else:
        (x_ref, g_ref, sh_ref, sc_ref, ga_ref, win_ref, cw_ref, cb_ref, cbuf_ref, wout_ref,
         o_ref, nc_ref, h_scr, carry_scr, act_scr, shift_scr) = refs
        gf_ref = None
    tm = x_ref.shape[0]

    @pl.when(pl.program_id(1) == 0)
    def _():
        carry_scr[...] = cbuf_ref[...]

    x = x_ref[...]
    h_scr[...] = _norm_mod(x, g_ref[...], sh_ref[0], sc_ref[0]).astype(BF16)
    cwd = FF_CHUNK
    tpos = lax.broadcasted_iota(jnp.int32, (nbt, t_in, cwd), 1)
    is0 = tpos == 0
    is1 = tpos == 1

    def conv_rolled(off):
        u = _dot(h_scr[...], win_ref[:, off:off + cwd]).reshape(nbt, t_in, cwd)
        prev = carry_scr[:, :, off:off + cwd]
        p0 = prev[:, 0:1, :]
        p1 = prev[:, 1:2, :]
        r1 = pltpu.roll(u, 1, axis=1)
        r2 = pltpu.roll(u, 2, axis=1)
        um1 = jnp.where(is0, p1, r1)
        um2 = jnp.where(is0, p0, jnp.where(is1, p1, r2))
        newc = r2[:, 0:2, :]
        carry_scr[:, :, off:off + cwd] = newc
        nc_ref[:, :, off:off + cwd] = newc
        cw = cw_ref[:, off:off + cwd]
        z = cb_ref[:, off:off + cwd] + cw[0:1] * um2
        z = z + cw[1:2] * um1
        z = z + cw[2:3] * u
        return z.reshape(tm, cwd)

    def up_shifted(off, slot):
        buf = shift_scr.at[slot]
        buf[SUB:SUB + tm, :] = _dot(h_scr[...], win_ref[:, off:off + cwd])
        buf[SUB - 2:SUB, :] = carry_scr[0, :, off:off + cwd]
        newc = buf[SUB + tm - 2:SUB + tm, :]
        carry_scr[0, :, off:off + cwd] = newc
        nc_ref[0, :, off:off + cwd] = newc

    def conv_shifted(off, slot):
        buf = shift_scr.at[slot]
        cw = cw_ref[:, off:off + cwd]
        z = cb_ref[:, off:off + cwd] + cw[0:1] * buf[SUB - 2:SUB - 2 + tm, :]
        z = z + cw[1:2] * buf[SUB - 1:SUB - 1 + tm, :]
        z = z + cw[2:3] * buf[SUB:SUB + tm, :]
        return z

    n_slots = shift_scr.shape[0]
    n_ch = D_FF // cwd

    def up_chunk(ch):
        up_shifted(ch * cwd, (2 * ch) % n_slots)
        up_shifted(D_FF + ch * cwd, (2 * ch + 1) % n_slots)

    if nbt == 1:
        up_chunk(0)
    for ch in range(n_ch):
        if nbt == 1:
            if ch + 1 < n_ch:
                up_chunk(ch + 1)
            za = conv_shifted(ch * cwd, (2 * ch) % n_slots)
            zg = conv_shifted(D_FF + ch * cwd, (2 * ch + 1) % n_slots)
        else:
            za = conv_rolled(ch * cwd)
            zg = conv_rolled(D_FF + ch * cwd)
        act_scr[:, ch * cwd:(ch + 1) * cwd] = (_silu(zg) * za).astype(BF16)
    xn = x + ga_ref[0] * _dot(act_scr[...], wout_ref[...])
    if final:
        xn = xn * lax.rsqrt(jnp.mean(xn * xn, axis=-1, keepdims=True) + EPS) * gf_ref[...]
    o_ref[...] = xn


def _ffn_call(x, g, sh, sc, ga, w_in, cw, cb, cbuf, w_out, b, t, tm, g_final=None, name="ffn"):
    m, d = x.shape
    f2 = w_in.shape[1]
    if t % tm == 0:
        t_in, nbt, nb_tiles, nt = tm, 1, b, t // tm
    else:
        assert tm % t == 0 and m % tm == 0
        t_in, nbt, nb_tiles, nt = t, tm // t, m // tm, 1
    sh_a, mshape, sh_map = _mod_arg(sh, m, tm, t)
    sc_a, _, sc_map = _mod_arg(sc, m, tm, t)
    ga_a, _, ga_map = _mod_arg(ga, m, tm, t)
    final = g_final is not None
    in_specs = [pl.BlockSpec((tm, d), lambda bi, ti: (bi * nt + ti, 0)),
                pl.BlockSpec((1, d), lambda bi, ti: (0, 0)),
                pl.BlockSpec(mshape, lambda bi, ti: sh_map(bi * nt + ti)),
                pl.BlockSpec(mshape, lambda bi, ti: sc_map(bi * nt + ti)),
                pl.BlockSpec(mshape, lambda bi, ti: ga_map(bi * nt + ti)),
                _const_spec((d, f2)),
                _const_spec((CONV_W, f2)),
                _const_spec((1, f2)),
                pl.BlockSpec((nbt, CONV_W - 1, f2), lambda bi, ti: (bi, 0, 0)),
                _const_spec((D_FF, d))]
    args = [x, g.reshape(1, d), sh_a, sc_a, ga_a, w_in, cw, cb.reshape(1, f2), cbuf, w_out]
    if final:
        in_specs.append(pl.BlockSpec((1, d), lambda bi, ti: (0, 0)))
        args.append(g_final.reshape(1, d))
    return pl.pallas_call(
        functools.partial(_ffn_kernel, t_in=t_in, nbt=nbt, final=final),
        grid=(nb_tiles, nt),
        in_specs=in_specs,
        out_specs=[pl.BlockSpec((tm, d), lambda bi, ti: (bi * nt + ti, 0)),
                   pl.BlockSpec((nbt, CONV_W - 1, f2), lambda bi, ti: (bi, 0, 0))],
        out_shape=[jax.ShapeDtypeStruct((m, d), F32),
                   jax.ShapeDtypeStruct((b, CONV_W - 1, f2), F32)],
        scratch_shapes=[pltpu.VMEM((tm, d), BF16),
                        pltpu.VMEM((nbt, CONV_W - 1, f2), F32),
                        pltpu.VMEM((tm, D_FF), BF16),
                        pltpu.VMEM((4, SUB + tm, FF_CHUNK), F32)],
        compiler_params=_params("parallel", "arbitrary"),
        name=name,
    )(*args)


def _cmp_kernel(pt_ref, *refs):
    pages = refs[:N_PAGES_ARG]
    pe_ref, w1_ref, w2t_ref, o_ref = refs[N_PAGES_ARG:]
    npc = PAGE_SIZE // S_CMP
    r_n = L_CMP // S_CMP
    for kind in range(2):
        rows = jnp.concatenate([pg[0, kind * GD:(kind + 1) * GD, :].T for pg in pages], axis=0)
        rows = jnp.swapaxes(rows.reshape(len(pages) * npc, S_CMP, GD), 0, 1)
        pieces = [rows[s] for s in range(S_CMP)]
        z = None
        for r in range(r_n):
            ys = []
            for gp in range(G_B // 2):
                ls = slice(gp * LANES, (gp + 1) * LANES)
                a = jnp.concatenate([(pieces[s][:, ls] + pe_ref[kind, r, s:s + 1, ls]).astype(BF16)
                                     for s in range(S_CMP)], axis=1)
                ys.append(_dot(a, w1_ref[kind, r]))
            y = jnp.concatenate(ys, axis=1)
            if r > 0:
                y = pltpu.roll(y, y.shape[0] - r, axis=0)
            z = y if z is None else z + y
        o_ref[0, kind] = _dot_nt(w2t_ref[kind], _silu(z).astype(BF16)).astype(o_ref.dtype)


def _cmp_call(rows_t, page_map, page_tab, pe, w1, w2t):
    b, npg = page_tab.shape
    assert npg == N_PAGES_ARG
    n_pieces = npg * PAGE_SIZE // S_CMP
    page_specs = [pl.BlockSpec((1, 2 * GD, PAGE_SIZE), functools.partial(lambda bi, pt, p: page_map(bi, p, pt), p=p))
                  for p in range(npg)]
    grid_spec = pltpu.PrefetchScalarGridSpec(
        num_scalar_prefetch=1,
        grid=(b,),
        in_specs=page_specs + [
            pl.BlockSpec(pe.shape, lambda bi, pt: (0, 0, 0, 0), pipeline_mode=pl.Buffered(1)),
            pl.BlockSpec(w1.shape, lambda bi, pt: (0, 0, 0, 0), pipeline_mode=pl.Buffered(1)),
            pl.BlockSpec(w2t.shape, lambda bi, pt: (0, 0, 0), pipeline_mode=pl.Buffered(1))],
        out_specs=pl.BlockSpec((1, 2, GD, n_pieces), lambda bi, pt: (bi, 0, 0, 0)),
    )
    return pl.pallas_call(
        _cmp_kernel,
        grid_spec=grid_spec,
        out_shape=jax.ShapeDtypeStruct((b, 2, GD, n_pieces), BF16),
        compiler_params=_params("parallel"),
        name="compress",
    )(page_tab.reshape(-1), *([rows_t] * npg), pe, w1, w2t)


def _masked_softmax(s, ok, base2=False):
    sm = jnp.where(ok, s, NEG)
    m = jnp.max(sm, axis=-1, keepdims=True)
    e = jnp.where(ok, (jnp.exp2 if base2 else jnp.exp)(sm - m), 0.0)
    return e / jnp.maximum(jnp.sum(e, axis=-1, keepdims=True), 1e-30)


def _topk_mask_steps(imp_t, cur, nsel, ncol):
    rows = imp_t.shape[0]
    jj = lax.broadcasted_iota(jnp.int32, (rows, ncol), 0)
    forced = (jj == 0) | (jj == cur) | (jj == cur - 1)
    score = jnp.where(jj <= cur, imp_t + FORCE_BONUS * forced.astype(F32), NEG)
    rank = jnp.zeros((rows, ncol), F32)
    for i in range(nsel):
        row = score[i:i + 1, :]
        lower = jnp.where(jj > i, 1.0, 0.0)
        rank = rank + jnp.where(row > score, 1.0, jnp.where(row == score, lower, 0.0))
        if i % 8 == 7 and i + 1 < nsel:
            yield None
    yield jnp.where((rank < float(min(N_SEL, nsel))) & (jj < nsel), 1.0, 0.0)


def _topk_mask_t(imp_t, cur, nsel, ncol):
    out = None
    for out in _topk_mask_steps(imp_t, cur, nsel, ncol):
        pass
    return out


def _round_robin(*streams):
    live = list(streams)
    while live:
        for s in list(live):
            try:
                next(s)
            except StopIteration:
                live.remove(s)


def _eye_bf16(n):
    return jnp.where(lax.broadcasted_iota(jnp.int32, (n, n), 0) == lax.broadcasted_iota(jnp.int32, (n, n), 1),
                     1.0, 0.0).astype(BF16)


def _pad_rows(a, rows):
    if a.shape[0] == rows:
        return a
    return jnp.concatenate([a, jnp.zeros((rows - a.shape[0],) + a.shape[1:], a.dtype)], axis=0)


def _online_step(sm, v, m_old, l_old, acc_old, v_transposed):
    m_new = jnp.maximum(m_old, jnp.max(sm, axis=-1, keepdims=True))
    alpha = jnp.exp(m_old - m_new)
    e = jnp.where(sm > 0.5 * NEG, jnp.exp(sm - m_new), 0.0)
    l_new = alpha * l_old + jnp.sum(e, axis=-1, keepdims=True)
    pv = _dot_nt(e.astype(BF16), v) if v_transposed else _dot(e.astype(BF16), v)
    return m_new, l_new, alpha * acc_old + pv


def _block_importance_t(ps_sum, ovt):
    hi, mid, lo = _split3(ps_sum)
    return _dot_nt(ovt, hi) + _dot_nt(ovt, mid) + _dot_nt(ovt, lo)


def _nsa_prompt_kernel(q_ref, gate_ref, cmp_ref, kv_ref, win_ref, featc_ref, fslc_ref, fwin_ref, fcmp_ref,
                       ovt_ref, band_ref, cbias_ref, o_ref,
                       kb_slc, vt_slc, kb_win, vt_win, kb_cmp, a_scr, s_scr, e_scr, m_scr, acc_scr, gb_scr, *, t_len):
    qb = pl.program_id(1)
    s0 = qb * Q_BLOCK
    nsel = t_len // L_SEL
    n_st = t_len // SLC_TILE
    n_wt = WINDOW // Q_BLOCK
    n_wtiles = kb_win.shape[1]
    rows_g = HPG * Q_BLOCK

    def ones_rows(width):
        r = lax.broadcasted_iota(jnp.int32, (A_W - DH_B, width), 0)
        return jnp.where(r == 0, 1.0, 0.0).astype(BF16)

    @pl.when(qb == 0)
    def _():
        for g in range(G_B):
            for kt in range(n_st):
                cs = slice(kt * SLC_TILE, (kt + 1) * SLC_TILE)
                kb_slc[g, kt, 0:DH_B, :] = kv_ref[0, g * DH_B:(g + 1) * DH_B, cs].astype(BF16)
                kb_slc[g, kt, DH_B:A_W, :] = fslc_ref[:, cs]
            for wt in range(n_wtiles):
                cs = slice((wt - n_wt) * Q_BLOCK, (wt - n_wt + 1) * Q_BLOCK)
                if wt < n_wt:
                    kb_win[g, wt, 0:DH_B, :] = jnp.zeros((DH_B, Q_BLOCK), BF16)
                else:
                    kb_win[g, wt, 0:DH_B, :] = win_ref[0, g * DH_B:(g + 1) * DH_B, cs].astype(BF16)
                kb_win[g, wt, DH_B:A_W, :] = fwin_ref[:, wt * Q_BLOCK:(wt + 1) * Q_BLOCK]
            kb_cmp[g, 0:DH_B, :] = cmp_ref[0, 0, g * DH_B:(g + 1) * DH_B, :]
            kb_cmp[g, DH_B:A_W, :] = fcmp_ref[...]
            for kt in range(n_st):
                cs = slice(kt * SLC_TILE, (kt + 1) * SLC_TILE)
                vt_slc[g, kt, 0:DH_B, :] = kv_ref[0, GD + g * DH_B:GD + (g + 1) * DH_B, cs].astype(BF16)
                vt_slc[g, kt, DH_B:A_W, :] = ones_rows(SLC_TILE)
            for wt in range(n_wtiles):
                cs = slice((wt - n_wt) * Q_BLOCK, (wt - n_wt + 1) * Q_BLOCK)
                if wt < n_wt:
                    vt_win[g, wt, 0:DH_B, :] = jnp.zeros((DH_B, Q_BLOCK), BF16)
                else:
                    vt_win[g, wt, 0:DH_B, :] = win_ref[0, GD + g * DH_B:GD + (g + 1) * DH_B, cs].astype(BF16)
                vt_win[g, wt, DH_B:A_W, :] = ones_rows(Q_BLOCK)

    lane = lax.broadcasted_iota(jnp.int32, (1, A_W), 1)
    ags = [jnp.concatenate([jnp.where(lane < DH_B, q_ref[:, (g * HPG + hl) * A_W:(g * HPG + hl + 1) * A_W],
                                      featc_ref[g * HPG + hl:g * HPG + hl + 1, :].astype(BF16))
                            for hl in range(HPG)], axis=0) for g in range(G_B)]

    ovt = ovt_ref[...]
    eye = _eye_bf16(Q_BLOCK)
    t_loc = lax.broadcasted_iota(jnp.int32, (Q_BLOCK, 1), 0)
    tpos = (s0 + t_loc).astype(F32)
    n_f = lax.broadcasted_iota(jnp.int32, (1, Q_BLOCK), 1).astype(F32)
    ok_c = (tpos - (n_f * S_CMP + (L_CMP - 1.0))) >= 0.0
    nrow = ((nsel + 7) // 8) * 8
    groups = range(G_B)

    def softmax_steps(g, width, bias_fn, out):
        m_old_all = m_scr[g]
        m_news, alphas = [], []
        for rb in range(rows_g // SM_ROWS):
            rows = slice(rb * SM_ROWS, (rb + 1) * SM_ROWS)
            s = s_scr[g, rows, 0:width]
            if bias_fn is not None:
                s = bias_fn(s, (rb * SM_ROWS) % Q_BLOCK)
            m_old = m_old_all[rows]
            m_new = jnp.maximum(m_old, jnp.max(s, axis=-1, keepdims=True))
            e = jnp.exp2(s - jnp.concatenate([m_new] * (width // LANES), axis=1))
            e_scr[g, rows, 0:width] = e.astype(BF16)
            m_news.append(m_new)
            alphas.append(jnp.exp2(m_old - m_new))
            if rb % 2 == 1:
                yield
        m_scr[g] = jnp.concatenate(m_news, axis=0)
        out.append(jnp.concatenate(alphas, axis=0))

    def reset_stats():
        m_scr[...] = jnp.full_like(m_scr, M_INIT)
        acc_scr[...] = jnp.zeros_like(acc_scr)

    def attend_steps(width, a_fn, key_fn, val_fn, bias_fn):
        s_scr[0, :, 0:width] = _dot(a_fn(0), key_fn(0))
        yield
        for g in groups:
            if g + 1 < G_B:
                s_scr[g + 1, :, 0:width] = _dot(a_fn(g + 1), key_fn(g + 1))
            alpha = []
            yield from softmax_steps(g, width, bias_fn, alpha)
            acc_scr[g] = alpha[0] * acc_scr[g] + _dot_nt(e_scr[g, :, 0:width], val_fn(g))
            yield

    def attend(*args):
        for _ in attend_steps(*args):
            pass

    def branch_out(g):
        acc = acc_scr[g]
        return acc[:, 0:DH_B] / jnp.maximum(acc[:, DH_B:DH_B + 1], 1e-30)

    reset_stats()
    win_stream = attend_steps(WINDOW + Q_BLOCK,
                              lambda g: ags[g],
                              lambda g: jnp.concatenate([kb_win[g, qb + w] for w in range(n_wt + 1)], axis=1),
                              lambda g: jnp.concatenate([vt_win[g, qb + w] for w in range(n_wt + 1)], axis=1),
                              lambda s, t0: s + band_ref[t0:t0 + SM_ROWS, :])

    cur_t = (s0 + lax.broadcasted_iota(jnp.int32, (nrow, Q_BLOCK), 1)) // L_SEL
    o_cmp, ags_sel = [None] * G_B, [None] * G_B

    def sel_stream(g):
        s_c = _dot(ags[g], kb_cmp[g]).reshape(HPG, Q_BLOCK, Q_BLOCK)
        yield
        p3 = _masked_softmax(s_c, ok_c[None], base2=True)
        yield
        o_cmp[g] = _dot_nt(p3.reshape(rows_g, Q_BLOCK).astype(BF16), cmp_ref[0, 1, g * DH_B:(g + 1) * DH_B, :])
        imp_t = _block_importance_t((p3[0] + p3[1]) + (p3[2] + p3[3]), ovt)
        yield
        sel_t = None
        for sel_t in _topk_mask_steps(imp_t[0:nrow], cur_t, nsel, Q_BLOCK):
            yield
        nsel_pad = jnp.concatenate([jnp.zeros((F_SEL, Q_BLOCK), F32), 1.0 - sel_t,
                                    jnp.zeros((A_W - F_SEL - nrow, Q_BLOCK), F32)], axis=0).astype(BF16)
        nsl = _dot_nt(eye, nsel_pad).astype(BF16)
        ags_sel[g] = ags[g] + jnp.concatenate([nsl] * HPG, axis=0)
        yield

    def gate_stream():
        for g in groups:
            gt = gate_ref[:, g * LANES:(g + 1) * LANES]
            for c in range(3 * HPG):
                gb_scr[g * 3 * HPG + c] = jnp.broadcast_to(gt[:, c:c + 1], (Q_BLOCK, LANES))
                if c % 2 == 1:
                    yield

    _round_robin(win_stream, *[sel_stream(g) for g in groups], gate_stream())
    o_win = [branch_out(g) for g in groups]
    for g in groups:
        a_scr[g] = ags_sel[g]

    def causal_bias(kti):
        c = qb % (SLC_TILE // Q_BLOCK)

        def fn(s, t0):
            return s + cbias_ref[c, t0:t0 + SM_ROWS, :]
        return fn

    def slc_tile(kti, diag):
        attend(SLC_TILE, lambda g: a_scr[g], lambda g: kb_slc[g, kti], lambda g: vt_slc[g, kti],
               causal_bias(kti) if diag else None)

    reset_stats()
    kd = s0 // SLC_TILE

    def slc_full(kti, c):
        slc_tile(kti, False)
        return c

    lax.fori_loop(0, kd, slc_full, 0)
    slc_tile(kd, True)
    o_slc = [branch_out(g) for g in groups]

    pieces = []
    for g in groups:
        for hl in range(HPG):
            r = slice(hl * Q_BLOCK, (hl + 1) * Q_BLOCK)
            gb = [gb_scr[g * 3 * HPG + br * HPG + hl, :, 0:DH_B] for br in range(3)]
            comb = gb[0] * o_cmp[g][r]
            comb = comb + gb[1] * o_slc[g][r]
            comb = comb + gb[2] * o_win[g][r]
            pieces.append(comb)
    o_ref[...] = jnp.concatenate(pieces, axis=1).astype(o_ref.dtype)


def _nsa_prompt_call(q, gates, cmp_t, kv_t, win_t, consts, b, t):
    featc, fslc, fwin, fcmp, ovt, band, cbias = consts
    nqb = t // Q_BLOCK
    n_wtiles = (WINDOW + t) // Q_BLOCK
    return pl.pallas_call(
        functools.partial(_nsa_prompt_kernel, t_len=t),
        grid=(b, nqb),
        in_specs=[pl.BlockSpec((Q_BLOCK, H_B * A_W), lambda bi, qi: (bi * nqb + qi, 0)),
                  pl.BlockSpec((Q_BLOCK, G_B * LANES), lambda bi, qi: (bi * nqb + qi, 0)),
                  pl.BlockSpec((1, 2, GD, cmp_t.shape[3]), lambda bi, qi: (bi, 0, 0, 0)),
                  pl.BlockSpec((1, 2 * GD, t), lambda bi, qi: (bi, 1, 0)),
                  pl.BlockSpec((1, 2 * GD, t), lambda bi, qi: (bi, 0, 0)),
                  _const_spec(featc.shape), _const_spec(fslc.shape), _const_spec(fwin.shape),
                  _const_spec(fcmp.shape), _const_spec(ovt.shape), _const_spec(band.shape), _const_spec(cbias.shape)],
        out_specs=pl.BlockSpec((Q_BLOCK, D_MODEL), lambda bi, qi: (bi * nqb + qi, 0)),
        out_shape=jax.ShapeDtypeStruct((b * t, D_MODEL), BF16),
        scratch_shapes=[pltpu.VMEM((G_B, t // SLC_TILE, A_W, SLC_TILE), BF16),
                        pltpu.VMEM((G_B, t // SLC_TILE, A_W, SLC_TILE), BF16),
                        pltpu.VMEM((G_B, n_wtiles, A_W, Q_BLOCK), BF16),
                        pltpu.VMEM((G_B, n_wtiles, A_W, Q_BLOCK), BF16),
                        pltpu.VMEM((G_B, A_W, Q_BLOCK), BF16),
                        pltpu.VMEM((G_B, HPG * Q_BLOCK, A_W), BF16),
                        pltpu.VMEM((G_B, HPG * Q_BLOCK, WINDOW + Q_BLOCK), F32),
                        pltpu.VMEM((G_B, HPG * Q_BLOCK, WINDOW + Q_BLOCK), BF16),
                        pltpu.VMEM((G_B, HPG * Q_BLOCK, LANES), F32),
                        pltpu.VMEM((G_B, HPG * Q_BLOCK, A_W), F32),
                        pltpu.VMEM((3 * H_B, Q_BLOCK, LANES), F32)],
        compiler_params=_params("parallel", "arbitrary"),
        name="nsa_prompt",
    )(q, gates, cmp_t, kv_t, win_t, featc, fslc, fwin, fcmp, ovt, band, cbias)


def _nsa_prompt_consts(t):
    slopes = jnp.exp2(-8.0 * jnp.arange(1, H_B + 1, dtype=F32) / H_B)
    featc = jnp.zeros((H_B, A_W), F32).at[:, F_ONE].set(1.0)
    for k, term in enumerate(_split3(slopes * LOG2E)):
        featc = featc.at[:, F_POS + 2 * k].set(term.astype(F32)).at[:, F_POS + 2 * k + 1].set(term.astype(F32))

    def pos_rows(pos):
        f = np.zeros((A_W - DH_B, pos.shape[0]), np.float32)
        hi = (pos // 256) * 256
        lo = pos % 256
        for k in range(3):
            f[F_POS - DH_B + 2 * k] = hi
            f[F_POS - DH_B + 2 * k + 1] = lo
        return f

    nsel = t // L_SEL
    spos = np.arange(t)
    fslc = pos_rows(spos)
    for j in range(nsel):
        fslc[F_SEL - DH_B + j] = np.where(spos // L_SEL == j, -BIG, 0.0)
    wcol = np.arange(WINDOW + t)
    fwin = pos_rows(wcol)
    fwin[F_ONE - DH_B] = np.where(wcol < WINDOW, -BIG, 0.0)
    fcmp = pos_rows(np.arange(Q_BLOCK) * S_CMP + (L_CMP - 1))
    u = np.arange(WINDOW + Q_BLOCK)[None, :]
    tl = np.arange(Q_BLOCK)[:, None]
    band = np.where((u >= tl) & (u <= tl + WINDOW), 0.0, -BIG).astype(np.float32)
    uc = np.arange(SLC_TILE)[None, None, :]
    cpos = (np.arange(SLC_TILE // Q_BLOCK) * Q_BLOCK)[:, None, None] + np.arange(Q_BLOCK)[None, :, None]
    cbias = np.where(uc <= cpos, 0.0, -BIG).astype(np.float32)
    return (featc, jnp.asarray(fslc, dtype=BF16), jnp.asarray(fwin, dtype=BF16),
            jnp.asarray(fcmp, dtype=BF16), _overlap_t(nsel), jnp.asarray(band), jnp.asarray(cbias))


def _nsa_sample_kernel(pt_ref, *refs, past, ts):
    pages = refs[:N_PAGES_ARG]
    (slope_ref, slopel_ref, q_ref, gate_ref, cmp_ref, kvn_ref, win_ref, winn_ref, eg_ref, ovt_ref, o_ref) = refs[N_PAGES_ARG:]
    rows = H_B * ts
    nsel = -(-(past + ts) // L_SEL)
    lane_g = lax.broadcasted_iota(jnp.int32, (1, GD), 1) // DH_B
    q = q_ref[...]
    qq = jnp.concatenate([jnp.where(lane_g == g, q[:, h * GD:(h + 1) * GD], 0.0)
                          for g in range(G_B) for h in range(HPG)], axis=0).astype(BF16)
    slope = slope_ref[...]
    t_row = lax.broadcasted_iota(jnp.int32, (rows, 1), 0) % ts
    tpos = (past + t_row).astype(F32)
    kc_t = cmp_ref[0, 0]
    vc_t = cmp_ref[0, 1]
    ncb = kc_t.shape[1]
    res = {}
    n_f = lax.broadcasted_iota(jnp.int32, (1, ncb), 1).astype(F32)
    dist_c = tpos - (n_f * S_CMP + (L_CMP - 1.0))
    nrow = ((nsel + 1 + 7) // 8) * 8
    cur_t = (past + lax.broadcasted_iota(jnp.int32, (nrow, rows), 1) % ts) // L_SEL

    def select_steps():
        p = _masked_softmax(_dot(qq, kc_t) - slope * dist_c, dist_c >= 0.0)
        yield
        res["o_cmp"] = _dot_nt(p.astype(BF16), vc_t)
        ps = []
        for g in range(G_B):
            blk = [p[(g * HPG + h) * ts:(g * HPG + h + 1) * ts] for h in range(HPG)]
            ps.extend([(blk[0] + blk[1]) + (blk[2] + blk[3])] * HPG)
        psum = jnp.concatenate(ps, axis=0)
        imp_t = _block_importance_t(psum, ovt_ref[...])
        yield
        for sel in _topk_mask_steps(imp_t[0:nrow], cur_t, nsel, rows):
            res["sel"] = sel
            yield

    tpos_l = (past + lax.broadcasted_iota(jnp.int32, (1, rows), 1) % ts).astype(F32)
    slope_l = slopel_ref[...]
    key_i = lax.broadcasted_iota(jnp.int32, (PAGE_SIZE, rows), 0)
    bpp = PAGE_SIZE // L_SEL

    def rows_t(new_ref, lo):
        return _pad_rows(new_ref[:, lo:lo + GD], PAGE_SIZE).T.astype(BF16)

    def attend_steps(name, kt_all, vt_all, tiles, before_masks=None):
        s_all = lax.dot_general(kt_all, qq, (((0,), (1,)), ((), ())), preferred_element_type=F32)
        yield
        if before_masks is not None:
            yield from before_masks
        sms = []
        for i, (k0, n_real, blk) in enumerate(tiles):
            s = s_all[i * PAGE_SIZE:(i + 1) * PAGE_SIZE]
            dist = tpos_l - (k0 + key_i).astype(F32)
            dist_m = dist if n_real == PAGE_SIZE else jnp.where(key_i < n_real, dist, -1.0)
            if blk is not None:
                sel_t = res["sel"]
                mask = jnp.concatenate([jnp.broadcast_to(sel_t[blk + a:blk + a + 1], (L_SEL, rows)) for a in range(bpp)],
                                       axis=0)
                ok = jnp.where(dist_m >= 0.0, mask, 0.0) > 0.5
            else:
                ok = jnp.abs(dist_m - 0.5 * WINDOW) <= 0.5 * WINDOW
            sms.append(jnp.where(ok, s - slope_l * dist, NEG))
            if i % 2 == 1:
                yield
        m = functools.reduce(jnp.maximum, [jnp.max(x, axis=0, keepdims=True) for x in sms])
        es = []
        for i, x in enumerate(sms):
            es.append(jnp.exp(x - m))
            if i % 4 == 3:
                yield
        l = functools.reduce(lambda a, b: a + b, [jnp.sum(e, axis=0, keepdims=True) for e in es])
        o_t = _dot(vt_all, jnp.concatenate(es, axis=0).astype(BF16))
        res[name] = (o_t / jnp.maximum(l, 1e-30)).T

    kt_slc = jnp.concatenate([pg[0, 0:GD, :].astype(BF16) for pg in pages] + [rows_t(kvn_ref, 0)], axis=1)
    vt_slc = jnp.concatenate([pg[0, GD:2 * GD, :].astype(BF16) for pg in pages] + [rows_t(kvn_ref, GD)], axis=1)
    wbuf = win_ref.shape[2]
    kt_win = jnp.concatenate([win_ref[0, 0:GD, :].astype(BF16), rows_t(winn_ref, 0)], axis=1)
    vt_win = jnp.concatenate([win_ref[0, GD:2 * GD, :].astype(BF16), rows_t(winn_ref, GD)], axis=1)
    _round_robin(
        attend_steps("o_slc", kt_slc, vt_slc,
                     [(pi * PAGE_SIZE, PAGE_SIZE, pi * bpp) for pi in range(len(pages))] + [(past, ts, past // L_SEL)],
                     before_masks=select_steps()),
        attend_steps("o_win", kt_win, vt_win,
                     [(past - wbuf + wi * PAGE_SIZE, PAGE_SIZE, None) for wi in range(wbuf // PAGE_SIZE)]
                     + [(past, ts, None)]))
    o_cmp, o_slc, o_win = res["o_cmp"], res["o_slc"], res["o_win"]

    hi, mid, lo = _split3(gate_ref[...])
    eg = eg_ref[...]
    gexp = _dot(hi, eg) + _dot(mid, eg) + _dot(lo, eg)

    def gate_rows(br):
        return jnp.concatenate([gexp[:, br * D_MODEL + h * GD:br * D_MODEL + (h + 1) * GD]
                                for _ in range(G_B) for h in range(HPG)], axis=0)

    comb = gate_rows(0) * o_cmp + gate_rows(1) * o_slc + gate_rows(2) * o_win
    outs = []
    for h in range(HPG):
        acc = None
        for g in range(G_B):
            piece = jnp.where(lane_g == g, comb[(g * HPG + h) * ts:(g * HPG + h + 1) * ts], 0.0)
            acc = piece if acc is None else acc + piece
        outs.append(acc)
    o_ref[...] = jnp.concatenate(outs, axis=1).astype(o_ref.dtype)


def _nsa_sample_call(page_tab, cache_t, slope_rows, q, gates, cmp_t, kvn, win_t, winn, eg, ovt, b, ts):
    npg = page_tab.shape[1]
    assert npg == N_PAGES_ARG
    past = npg * PAGE_SIZE
    page_specs = [pl.BlockSpec((1, 2 * GD, PAGE_SIZE), functools.partial(lambda bi, pt, p: (pt[bi * N_PAGES_ARG + p], 1, 0), p=p))
                  for p in range(npg)]
    wbuf = win_t.shape[2]
    grid_spec = pltpu.PrefetchScalarGridSpec(
        num_scalar_prefetch=1,
        grid=(b,),
        in_specs=page_specs + [
            pl.BlockSpec(slope_rows.shape, lambda bi, pt: (0, 0), pipeline_mode=pl.Buffered(1)),
            pl.BlockSpec((1, slope_rows.shape[0]), lambda bi, pt: (0, 0), pipeline_mode=pl.Buffered(1)),
            pl.BlockSpec((ts, D_MODEL), lambda bi, pt: (bi, 0)),
            pl.BlockSpec((ts, LANES), lambda bi, pt: (bi, 0)),
            pl.BlockSpec((1, 2, GD, cmp_t.shape[3]), lambda bi, pt: (bi, 0, 0, 0)),
            pl.BlockSpec((ts, 2 * GD), lambda bi, pt: (bi, 1)),
            pl.BlockSpec((1, 2 * GD, wbuf), lambda bi, pt: (bi, 0, 0)),
            pl.BlockSpec((ts, 2 * GD), lambda bi, pt: (bi, 0)),
            pl.BlockSpec(eg.shape, lambda bi, pt: (0, 0), pipeline_mode=pl.Buffered(1)),
            pl.BlockSpec(ovt.shape, lambda bi, pt: (0, 0), pipeline_mode=pl.Buffered(1))],
        out_specs=pl.BlockSpec((ts, D_MODEL), lambda bi, pt: (bi, 0)),
    )
    return pl.pallas_call(
        functools.partial(_nsa_sample_kernel, past=past, ts=ts),
        grid_spec=grid_spec,
        out_shape=jax.ShapeDtypeStruct((b * ts, D_MODEL), F32),
        compiler_params=_params("parallel"),
        name="nsa_sample",
    )(page_tab.reshape(-1), *([cache_t] * npg), slope_rows, slope_rows.reshape(1, -1), q, gates, cmp_t, kvn, win_t, winn, eg, ovt)


def _overlap_t(nsel):
    n = np.arange(LANES)[None, :] * S_CMP
    j = np.arange(LANES)[:, None] * L_SEL
    ov = (n <= j + L_SEL - 1) & (n + L_CMP - 1 >= j) & (np.arange(LANES)[None, :] < LANES - 1) & (np.arange(LANES)[:, None] < nsel)
    return jnp.asarray(ov.astype(np.float32), dtype=BF16)


def _gate_expand():
    e = np.zeros((LANES, 3 * D_MODEL), np.float32)
    for br in range(3):
        for hh in range(H_B):
            e[br * H_B + hh, br * D_MODEL + hh * DH_B:br * D_MODEL + (hh + 1) * DH_B] = 1.0
    return jnp.asarray(e, dtype=BF16)


def _cmp_weights(pe, w1, w2):
    r_n = L_CMP // S_CMP
    pe_t = jnp.tile(pe.reshape(r_n, S_CMP, 1, DH_B), (1, 1, G_B, 1)).reshape(r_n, S_CMP, GD)
    eye = jnp.eye(G_B, dtype=F32)
    w1r = w1.reshape(r_n, S_CMP, DH_B, CMP_HID)
    w1b = jnp.einsum("rsdh,ab->rsadbh", w1r, jnp.eye(2, dtype=F32)).reshape(r_n, S_CMP * 2 * DH_B, 2 * CMP_HID)
    w2bt = jnp.einsum("hd,gk->kdgh", w2, eye).reshape(GD, G_B * CMP_HID)
    return pe_t, w1b.astype(BF16), w2bt.astype(BF16)


def _rows_minor(x):
    b, r = x.shape[:2]
    return jnp.moveaxis(x, 1, -1).reshape(b, -1, r)


def _rows_major(x_t, feat_shape):
    b, _, r = x_t.shape
    return jnp.moveaxis(x_t.reshape((b,) + feat_shape + (r,)), -1, 1)


def kernel(x_prompt, x_sample, c_prompt, c_sample, state_ret, state_conv, cache_kv, state_win, page_table, w_ada, b_ada, g_mix, g_ffn, w_ffn_in, conv_w, conv_b, w_ffn_out, w_ret_in, w_ret_out, g_kv, w_ada_kv, b_ada_kv, w_kv, pe_ck, pe_cv, w_ck1, w_ck2, w_cv1, w_cv2, w_nsa_in, w_nsa_out, g_final):
    bp, tp, d = x_prompt.shape
    bs, ts, _ = x_sample.shape
    depth = w_ada.shape[0]
    assert depth == 2 and d == D_MODEL and tp % SLC_TILE == 0 and ts == 8
    npg = page_table.shape[1]
    past = npg * PAGE_SIZE
    f2 = 2 * D_FF
    hd = H_B * DH_B

    w_ada_all = jnp.concatenate([w_ada[0], w_ada[1], w_ada_kv], axis=1).astype(BF16)
    b_ada_all = jnp.concatenate([b_ada[0], b_ada[1], b_ada_kv], axis=0).reshape(1, -1)
    w_ret_in_b = w_ret_in[0].astype(BF16)
    w_ret_out_b = w_ret_out[0].astype(BF16)
    w_ffn_in_b = w_ffn_in.astype(BF16)
    w_ffn_out_b = w_ffn_out.astype(BF16)
    w_kv_b = w_kv.astype(BF16)
    w_kv_t = w_kv_b.T
    w_kv_rows, w_kv_win = w_kv_b[:, :4 * GD], w_kv_b[:, 4 * GD:]
    wn = w_nsa_in[0]
    wq_s = wn[:, :hd] * (DH_B ** -0.5)
    w_q_pad = jnp.pad((wq_s * LOG2E).reshape(d, H_B, DH_B),
                      ((0, 0), (0, 0), (0, A_W - DH_B))).reshape(d, H_B * A_W).astype(BF16)
    w_g_grp = jnp.pad(wn[:, hd:].reshape(d, G_B, HPG, 3).transpose(0, 1, 3, 2).reshape(d, G_B, 3 * HPG),
                      ((0, 0), (0, 0), (0, LANES - 3 * HPG))).reshape(d, G_B * LANES).astype(BF16)
    w_q_s = wq_s.reshape(d, G_B, HPG, DH_B).transpose(0, 2, 1, 3).reshape(d, hd).astype(BF16)
    w_g_s = jnp.pad(wn[:, hd:].reshape(d, G_B, HPG, 3).transpose(0, 3, 2, 1).reshape(d, 3 * H_B),
                    ((0, 0), (0, LANES - 3 * H_B))).astype(BF16)
    w_o_b = w_nsa_out[0].astype(BF16)
    w_o_s = w_nsa_out[0].reshape(G_B, HPG, DH_B, d).transpose(1, 0, 2, 3).reshape(hd, d).astype(BF16)
    pe_k, w1_k, w2_k = _cmp_weights(pe_ck, w_ck1, w_ck2)
    pe_v, w1_v, w2_v = _cmp_weights(pe_cv, w_cv1, w_cv2)
    pe_c = jnp.stack([pe_k, pe_v])
    w1_c = jnp.stack([w1_k, w1_v])
    w2t_c = jnp.stack([w2_k, w2_v])
    slopes = jnp.exp2(-8.0 * jnp.arange(1, H_B + 1, dtype=F32) / H_B)
    log_g = jnp.log1p(-jnp.exp2(-5.0 - jnp.arange(H_A, dtype=F32)))
    lg_tab = jnp.broadcast_to(log_g[:, None, None], (H_A, 1, LANES))

    c_all = jnp.concatenate([c_prompt, c_sample], axis=0)
    mod = _ada_call(c_all, w_ada_all, b_ada_all)

    def mods(table, per_row):
        def cols(k):
            return (table, k, per_row)
        return [[cols(l * 6 + k) for k in range(6)] for l in range(depth)] + [[cols(12), cols(13)]]

    def layer0(x, mod_l, s_ret, conv_buf, b, t, tm, act_dt):
        sh1, sc1, ga1, sh2, sc2, ga2 = mod_l
        tm_in = 2 * tm if t % (2 * tm) == 0 else tm
        pr = _nmm_call(x, g_mix[0], sh1, sc1, w_ret_in_b, t, tm_in, RET_IN_TN, act_dt, name="ret_in")
        og, s_new = _ret_call(pr, s_ret, lg_tab, b, t, act_dt)
        x = _mm_res_call(og, w_ret_out_b, x, ga1, t, tm_in, name="ret_out")
        x, conv0 = _ffn_call(x, g_ffn[0], sh2, sc2, ga2, w_ffn_in_b[0], conv_w[0], conv_b[0], conv_buf,
                             w_ffn_out_b[0], b, t, tm, name="ffn0")
        return x, s_new, conv0

    def layer1_tail(x, o, w_o, mod_l, conv_buf, b, t, tm):
        _, _, ga1, sh2, sc2, ga2 = mod_l
        x = _mm_res_call(o, w_o, x, ga1, t, 2 * tm if t % (2 * tm) == 0 else tm, name="nsa_out")
        return _ffn_call(x, g_ffn[1], sh2, sc2, ga2, w_ffn_in_b[1], conv_w[1], conv_b[1], conv_buf,
                         w_ffn_out_b[1], b, t, tm, g_final=g_final, name="ffn1")

    tm = 512
    mp = mods(mod[0:bp], False)
    x = x_prompt.reshape(bp * tp, d)
    x, ret_p, conv0_p = layer0(x, mp[0], None, jnp.zeros((bp, CONV_W - 1, f2), F32), bp, tp, tm, BF16)
    sh1, sc1 = mp[1][0], mp[1][1]
    tm_p = 2 * tm
    kv_t, win_t = _kvt_call(x, g_kv, mp[2][0], mp[2][1], w_kv_t, bp, tp, tm_p)
    q = _nmm_call(x, g_mix[1], sh1, sc1, w_q_pad, tp, tm_p, H_B * A_W, BF16, name="q_proj")
    gates = _nmm_call(x, g_mix[1], sh1, sc1, w_g_grp, tp, tm_p, G_B * LANES, F32, act="sigmoid", name="gate_proj")
    assert tp == N_PAGES_ARG * PAGE_SIZE
    ident = jnp.zeros((bp, N_PAGES_ARG), jnp.int32)
    cmp_t = _cmp_call(kv_t, lambda bi, p, pt: (bi, 0, p), ident, pe_c, w1_c, w2t_c)
    o = _nsa_prompt_call(q, gates, cmp_t, kv_t, win_t, _nsa_prompt_consts(tp), bp, tp)
    y_p, conv1_p = layer1_tail(x, o, w_o_b, mp[1], jnp.zeros((bp, CONV_W - 1, f2), F32), bp, tp, tm)
    kv_p = _rows_major(kv_t, (4, G_B, DH_B))
    win_p = _rows_major(win_t[:, :, tp - min(WINDOW, tp):], (2, G_B, DH_B))

    tm = 256
    ms = mods(jnp.repeat(mod[bp:bp + bs], ts, axis=0), True)
    x = x_sample.reshape(bs * ts, d)
    x, ret_s, conv0_s = layer0(x, ms[0], state_ret[0].astype(F32), state_conv[0], bs, ts, tm, F32)
    sh1, sc1 = ms[1][0], ms[1][1]
    kv_rows = _nmm_call(x, g_kv, ms[2][0], ms[2][1], w_kv_rows, ts, tm, 4 * GD, F32, name="kv_rows")
    win_rows = _nmm_call(x, g_kv, ms[2][0], ms[2][1], w_kv_win, ts, tm, 2 * GD, F32, name="win_rows")
    q = _nmm_call(x, g_mix[1], sh1, sc1, w_q_s, ts, tm, hd, F32, name="q_proj")
    gates = _nmm_call(x, g_mix[1], sh1, sc1, w_g_s, ts, tm, LANES, F32, act="sigmoid", name="gate_proj")
    cache_t = _rows_minor(cache_kv)
    state_win_t = _rows_minor(state_win)
    cmp_t = _cmp_call(cache_t, lambda bi, p, pt: (pt[bi * N_PAGES_ARG + p], 0, 0), page_table, pe_c, w1_c, w2t_c)
    slope_rows = jnp.repeat(slopes, ts).reshape(H_B * ts, 1)
    o = _nsa_sample_call(page_table, cache_t, slope_rows, q, gates, cmp_t, kv_rows, state_win_t, win_rows,
                         _gate_expand(), _overlap_t(-(-(past + ts) // L_SEL)), bs, ts)
    y_s, conv1_s = layer1_tail(x, o, w_o_s, ms[1], state_conv[1], bs, ts, tm)
    kv_s = kv_rows.reshape(bs, ts, 4, G_B, DH_B)
    win_new_t = jnp.concatenate([state_win_t, jnp.swapaxes(win_rows.reshape(bs, ts, 2 * GD), 1, 2)], axis=2)
    win_s = _rows_major(win_new_t[:, :, win_new_t.shape[2] - min(WINDOW, win_new_t.shape[2]):], (2, G_B, DH_B))

    return (y_p.reshape(bp, tp, d), y_s.reshape(bs, ts, d), ret_p[None], ret_s[None],
            jnp.stack([conv0_p, conv1_p]), jnp.stack([conv0_s, conv1_s]), kv_p, kv_s, win_p, win_s)
```
